```python
import jax, jax.numpy as jnp
from jax import lax
import numpy as np

D_MODEL = 1024
BATCH = 8
SEQ = 2048
DEPTH = 2

GRID_W = 64
CTX_LEN = 256
RET_HEADS = 4
RET_DIM = 64
RET_W = RET_HEADS * RET_DIM
RET_CHUNK = 128
MLA_HEADS = 8
MLA_NOPE = 64
MLA_ROPE = 32
MLA_V = 64
MLA_Q_RANK = 256
MLA_KV_RANK = 128
MLA_W = MLA_HEADS * MLA_V
POOL_GROUPS = 4
POOL_WINDOWS = (2, 4, 8, 16)
POOL_W = D_MODEL - RET_W - MLA_W
POOL_GDIM = POOL_W // POOL_GROUPS
MIX_W = RET_W + MLA_W + POOL_W
IN_SIZES = (RET_W, RET_W, RET_W, RET_W, MLA_Q_RANK, MLA_KV_RANK, MLA_ROPE, POOL_W)
IN_W = sum(IN_SIZES)
D_FF = 4 * D_MODEL
Q_BLOCK = 128
ROPE_BASE = 10000.0
EPS = 1e-6

kernel_name = "hybrid_retention_mla_pool_dit"


def rmsnorm(x, g):
    xf = x.astype(jnp.float32)
    y = xf * lax.rsqrt(jnp.mean(xf * xf, axis=-1, keepdims=True) + EPS)
    return (y * g.astype(jnp.float32)).astype(x.dtype)


def modulate(h, shift, scale):
    return h * (1.0 + scale) + shift


def head_norm(o):
    of = o.astype(jnp.float32)
    mu = jnp.mean(of, axis=-1, keepdims=True)
    var = jnp.mean(jnp.square(of - mu), axis=-1, keepdims=True)
    return (of - mu) * lax.rsqrt(var + EPS)


def flip(a):
    return jnp.flip(a, axis=1)


def split_proj(p):
    idx, acc = [], 0
    for s in IN_SIZES[:-1]:
        acc += s
        idx.append(acc)
    return jnp.split(p, idx, axis=-1)


def axial_rope_tables(length, dim):
    rows = length // GRID_W
    row = jnp.repeat(jnp.arange(rows), GRID_W).astype(jnp.float32)
    col = jnp.tile(jnp.arange(GRID_W), rows).astype(jnp.float32)
    n_freq = dim // 4
    inv = ROPE_BASE ** (-jnp.arange(n_freq, dtype=jnp.float32) / n_freq)
    ang = jnp.concatenate([row[:, None] * inv, col[:, None] * inv], axis=-1)
    return jnp.cos(ang), jnp.sin(ang)


def apply_rope(x, cos, sin):
    shape = (x.shape[1],) + (1,) * (x.ndim - 3) + (cos.shape[-1],)
    cos = cos.reshape(shape).astype(x.dtype)
    sin = sin.reshape(shape).astype(x.dtype)
    x1, x2 = jnp.split(x, 2, axis=-1)
    return jnp.concatenate([x1 * cos - x2 * sin, x1 * sin + x2 * cos], axis=-1)


def ret_states(k, v, log_g, s0):
    B, L, H, dk = k.shape
    n = L // RET_CHUNK
    kc = k.reshape(B, n, RET_CHUNK, H, dk)
    vc = v.reshape(B, n, RET_CHUNK, H, -1)
    pos = jnp.arange(RET_CHUNK, dtype=jnp.float32)
    w_k = jnp.exp(log_g[:, None] * (RET_CHUNK - 1.0 - pos)[None, :])
    u = jnp.einsum('bnjhd,hj,bnjhe->nbhde', kc, w_k, vc)
    g_chunk = jnp.exp(log_g * RET_CHUNK)[None, :, None, None]

    def step(s, u_c):
        return g_chunk * s + u_c, s

    s_fin, s_start = lax.scan(step, s0, u)
    return jnp.moveaxis(s_start, 0, 1), s_fin


def ret_outputs(q, k, v, log_g, s_start, inclusive):
    B, L, H, dk = q.shape
    n = L // RET_CHUNK
    qc = q.reshape(B, n, RET_CHUNK, H, dk)
    kc = k.reshape(B, n, RET_CHUNK, H, dk)
    vc = v.reshape(B, n, RET_CHUNK, H, -1)
    pos = jnp.arange(RET_CHUNK, dtype=jnp.float32)
    diff = pos[:, None] - pos[None, :]
    mask = (diff >= 0) if inclusive else (diff > 0)
    d_in = jnp.where(mask[None], jnp.exp(log_g[:, None, None] * jnp.where(mask, diff, 0.0)[None]), 0.0)
    s = jnp.einsum('bnihd,bnjhd->bnhij', qc, kc) * d_in
    o = jnp.einsum('bnhij,bnjhe->bnihe', s, vc)
    w_q = jnp.exp(log_g[:, None] * (pos + 1.0)[None, :])
    o = o + jnp.einsum('bnihd,hi,bnhde->bnihe', qc, w_q, s_start)
    return o.reshape(B, L, H, -1)


def retention_bidir(q, k, v, lg, s0_f, s0_b):
    st_f, fin_f = ret_states(k, v, lg[0], s0_f)
    qb, kb, vb = flip(q), flip(k), flip(v)
    st_b, fin_b = ret_states(kb, vb, lg[1], s0_b)
    o = ret_outputs(q, k, v, lg[0], st_f, True) + flip(ret_outputs(qb, kb, vb, lg[1], st_b, False))
    return o, fin_f, fin_b


def retention_out(o, g):
    B, L = o.shape[:2]
    return jax.nn.silu(g) * head_norm(o).reshape(B, L, RET_W).astype(g.dtype)


def mla_q(cq, q_norm, w_uq):
    B, L, _ = cq.shape
    q = (rmsnorm(cq, q_norm) @ w_uq).reshape(B, L, MLA_HEADS, MLA_NOPE + MLA_ROPE)
    return q[..., :MLA_NOPE], q[..., MLA_NOPE:]


def mla_kv(ckv, kv_norm, w_ukv):
    B, L, _ = ckv.shape
    kv = (rmsnorm(ckv, kv_norm) @ w_ukv).reshape(B, L, MLA_HEADS, MLA_NOPE + MLA_V)
    return kv[..., :MLA_NOPE], kv[..., MLA_NOPE:]


def mla_attend(q_nope, q_pe, k_nope, k_pe, v):
    scale = (MLA_NOPE + MLA_ROPE) ** -0.5
    s = (jnp.einsum('bqhd,bkhd->bhqk', q_nope, k_nope)
         + jnp.einsum('bqhr,bkr->bhqk', q_pe, k_pe)) * scale
    p = jax.nn.softmax(s.astype(jnp.float32), axis=-1).astype(v.dtype)
    return jnp.einsum('bhqk,bkhe->bqhe', p, v)


def blocked_attend(q_nope, q_pe, k_nope, k_pe, v):
    B, L, H, _ = q_nope.shape
    nb = L // Q_BLOCK
    qn = jnp.moveaxis(q_nope.reshape(B, nb, Q_BLOCK, H, -1), 1, 0)
    qp = jnp.moveaxis(q_pe.reshape(B, nb, Q_BLOCK, H, -1), 1, 0)
    o = lax.map(lambda a: mla_attend(a[0], a[1], k_nope, k_pe, v), (qn, qp))
    return jnp.moveaxis(o, 0, 1).reshape(B, L, H, -1)


def pool_branch(u, w_pool, pool_scale):
    B, L, C = u.shape
    win = jnp.repeat(jnp.asarray(POOL_WINDOWS, jnp.int32), POOL_GDIM)
    t = jnp.arange(L, dtype=jnp.int32)[:, None]
    lo = jnp.clip(t - win // 2, 0, L)
    hi = jnp.clip(t - win // 2 + win, 0, L)
    cs = jnp.pad(jnp.cumsum(u.astype(jnp.float32), axis=1), ((0, 0), (1, 0), (0, 0)))
    tot = (jnp.take_along_axis(cs, jnp.broadcast_to(hi[None], (B, L, C)), axis=1)
           - jnp.take_along_axis(cs, jnp.broadcast_to(lo[None], (B, L, C)), axis=1))
    pooled = (tot / (hi - lo).astype(jnp.float32) - u.astype(jnp.float32)).astype(u.dtype)
    y = jnp.einsum('blgc,gcd->blgd', pooled.reshape(B, L, POOL_GROUPS, POOL_GDIM), w_pool)
    return y.reshape(B, L, C) * pool_scale


def token_mix(hx, hc, w_in, q_norm, w_uq, kv_norm, w_ukv, decay_logit, w_pool, pool_scale, w_out, need_ctx):
    B, L, _ = hx.shape
    rq_l, rk_l, rv_l, rg_l, cq_l, ckv_l, kpe_l, u_l = split_proj(hx @ w_in)
    rq_c, rk_c, rv_c, rg_c, cq_c, ckv_c, kpe_c, u_c = split_proj(hc @ w_in)

    def ret_heads(a):
        return a.reshape(a.shape[0], a.shape[1], RET_HEADS, RET_DIM)
    k_scale = RET_DIM ** -0.5
    cos_r, sin_r = axial_rope_tables(L, RET_DIM)
    ql = apply_rope(ret_heads(rq_l), cos_r, sin_r)
    kl = apply_rope(ret_heads(rk_l), cos_r, sin_r) * k_scale
    vl = ret_heads(rv_l)
    qc, kc, vc = ret_heads(rq_c), ret_heads(rk_c) * k_scale, ret_heads(rv_c)
    lg = jax.nn.log_sigmoid(decay_logit.astype(jnp.float32))
    s_zero = jnp.zeros((B, RET_HEADS, RET_DIM, RET_DIM), jnp.float32)
    if need_ctx:
        o_rc, fin_f, fin_b = retention_bidir(qc, kc, vc, lg, s_zero, s_zero)
    else:
        _, fin_f = ret_states(kc, vc, lg[0], s_zero)
        _, fin_b = ret_states(flip(kc), flip(vc), lg[1], s_zero)
    o_rl, _, _ = retention_bidir(ql, kl, vl, lg, fin_f, fin_b)
    ret_l = retention_out(o_rl, rg_l)

    cos_m, sin_m = axial_rope_tables(L, MLA_ROPE)
    qn_l, qp_l = mla_q(cq_l, q_norm, w_uq)
    qp_l = apply_rope(qp_l, cos_m, sin_m)
    kn_l, v_l = mla_kv(ckv_l, kv_norm, w_ukv)
    kpe_l = apply_rope(kpe_l, cos_m, sin_m)
    kn_c, v_c = mla_kv(ckv_c, kv_norm, w_ukv)
    kn_all = jnp.concatenate([kn_c, kn_l], axis=1)
    kpe_all = jnp.concatenate([kpe_c, kpe_l], axis=1)
    v_all = jnp.concatenate([v_c, v_l], axis=1)
    mla_l = blocked_attend(qn_l, qp_l, kn_all, kpe_all, v_all).reshape(B, L, MLA_W)

    pool_l = pool_branch(u_l, w_pool, pool_scale)

    out_l = jnp.concatenate([ret_l, mla_l, pool_l], axis=-1) @ w_out
    if not need_ctx:
        return out_l, None
    Lc = hc.shape[1]
    qn_c, qp_c = mla_q(cq_c, q_norm, w_uq)
    mla_c = mla_attend(qn_c, qp_c, kn_c, kpe_c, v_c).reshape(B, Lc, MLA_W)
    out_c = jnp.concatenate([retention_out(o_rc, rg_c), mla_c, pool_branch(u_c, w_pool, pool_scale)], axis=-1) @ w_out
    return out_l, out_c


def sq_relu_mlp(h, w1, w2):
    return jnp.square(jax.nn.relu(h @ w1)) @ w2


def setup_inputs(seed: int = 0) -> dict:
    key = jax.random.key(seed)
    ks = jax.random.split(key, 20)
    f32 = jnp.float32

    def nrm(k, shape, fan_in):
        return jax.random.normal(k, shape, f32) * (fan_in ** -0.5)

    def gain(k, shape):
        return 1.0 + 0.02 * jax.random.normal(k, shape, f32)

    base = 1.0 - 2.0 ** (-5.0 - jnp.arange(RET_HEADS, dtype=f32))
    logit = jnp.log(base) - jnp.log1p(-base)
    ret_decay_logit = jnp.broadcast_to(logit, (DEPTH, 2, RET_HEADS)) + 0.1 * jax.random.normal(ks[12], (DEPTH, 2, RET_HEADS), f32)
    return {
        "x": jax.random.normal(ks[0], (BATCH, SEQ, D_MODEL), f32),
        "c": jax.random.normal(ks[1], (BATCH, D_MODEL), f32),
        "ctx": jax.random.normal(ks[2], (BATCH, CTX_LEN, D_MODEL), f32),
        "c_ctx": jax.random.normal(ks[3], (D_MODEL,), f32),
        "w_ada": nrm(ks[4], (DEPTH, D_MODEL, 6 * D_MODEL), D_MODEL),
        "b_ada": 0.01 * jax.random.normal(ks[5], (DEPTH, 6 * D_MODEL), f32),
        "norm_mix": gain(ks[6], (DEPTH, D_MODEL)),
        "w_in": nrm(ks[7], (DEPTH, D_MODEL, IN_W), D_MODEL),
        "q_norm": gain(ks[8], (DEPTH, MLA_Q_RANK)),
        "w_uq": nrm(ks[9], (DEPTH, MLA_Q_RANK, MLA_HEADS * (MLA_NOPE + MLA_ROPE)), MLA_Q_RANK),
        "kv_norm": gain(ks[10], (DEPTH, MLA_KV_RANK)),
        "w_ukv": nrm(ks[11], (DEPTH, MLA_KV_RANK, MLA_HEADS * (MLA_NOPE + MLA_V)), MLA_KV_RANK),
        "ret_decay_logit": ret_decay_logit,
        "w_pool": nrm(ks[13], (DEPTH, POOL_GROUPS, POOL_GDIM, POOL_GDIM), POOL_GDIM),
        "pool_scale": 1.0 + 0.1 * jax.random.normal(ks[14], (DEPTH, POOL_W), f32),
        "w_out": nrm(ks[15], (DEPTH, MIX_W, D_MODEL), MIX_W),
        "norm_mlp": gain(ks[16], (DEPTH, D_MODEL)),
        "w_ff1": nrm(ks[17], (DEPTH, D_MODEL, D_FF), D_MODEL),
        "w_ff2": nrm(ks[18], (DEPTH, D_FF, D_MODEL), D_FF),
        "norm_final": gain(ks[19], (D_MODEL,)),
    }


def reference(x, c, ctx, c_ctx, w_ada, b_ada, norm_mix, w_in, q_norm, w_uq, kv_norm, w_ukv,
              ret_decay_logit, w_pool, pool_scale, w_out, norm_mlp, w_ff1, w_ff2, norm_final):
    h = ctx
    for l in range(DEPTH):
        last = l == DEPTH - 1
        mod_x = (jax.nn.silu(c) @ w_ada[l] + b_ada[l])[:, None, :]
        mod_c = jax.nn.silu(c_ctx) @ w_ada[l] + b_ada[l]
        sh1, sc1, g1, sh2, sc2, g2 = jnp.split(mod_x, 6, axis=-1)
        csh1, csc1, cg1, csh2, csc2, cg2 = jnp.split(mod_c, 6, axis=-1)

        hx = modulate(rmsnorm(x, norm_mix[l]), sh1, sc1)
        hc = modulate(rmsnorm(h, norm_mix[l]), csh1, csc1)
        ox, oc = token_mix(hx, hc, w_in[l], q_norm[l], w_uq[l], kv_norm[l], w_ukv[l],
                           ret_decay_logit[l], w_pool[l], pool_scale[l], w_out[l], not last)
        x = x + g1 * ox
        x = x + g2 * sq_relu_mlp(modulate(rmsnorm(x, norm_mlp[l]), sh2, sc2), w_ff1[l], w_ff2[l])
        if not last:
            h = h + cg1 * oc
            h = h + cg2 * sq_relu_mlp(modulate(rmsnorm(h, norm_mlp[l]), csh2, csc2), w_ff1[l], w_ff2[l])
    return rmsnorm(x, norm_final)
```

```python
import functools

import jax
import jax.numpy as jnp
import numpy as np
from jax import lax
from jax.experimental import pallas as pl
from jax.experimental.pallas import tpu as pltpu

GRID_W = 64
RET_HEADS = 4
RET_DIM = 64
RET_W = RET_HEADS * RET_DIM
RET_CHUNK = 128
MLA_HEADS = 8
MLA_NOPE = 64
MLA_ROPE = 32
MLA_V = 64
MLA_Q_RANK = 256
MLA_KV_RANK = 128
MLA_W = MLA_HEADS * MLA_V
POOL_GROUPS = 4
POOL_WINDOWS = (2, 4, 8, 16)
POOL_GDIM = 64
POOL_W = POOL_GROUPS * POOL_GDIM
ROPE_BASE = 10000.0
EPS = 1e-6

LANES = 128
ROW_TILE = 256
HEAD_PAD = 128
POOL_HALO = 16
VMEM_LIMIT = 56 * 1024 * 1024

COL_RET = 0
COL_CQ = 4 * RET_W
COL_U = COL_CQ + MLA_Q_RANK
COL_CKV = COL_U + POOL_W
COL_KPE = COL_CKV + MLA_KV_RANK
IN_COLS = COL_KPE + HEAD_PAD

F32 = jnp.float32
BF16 = jnp.bfloat16


def _params(sem):
    return pltpu.CompilerParams(dimension_semantics=sem, vmem_limit_bytes=VMEM_LIMIT)


def _dot(a, b):
    return jnp.dot(a, b, preferred_element_type=F32)


def _dot_nt(a, b):
    return lax.dot_general(a, b, (((1,), (1,)), ((), ())), preferred_element_type=F32)


def _rmsnorm_mod(x, g, shift, scale):
    y = x * lax.rsqrt(jnp.mean(x * x, axis=-1, keepdims=True) + EPS) * g
    return y * (1.0 + scale) + shift


def _silu(x):
    return x * (1.0 / (1.0 + jnp.exp(-x)))


def _rope(y, c, sa, sb, half):
    n = y.shape[-1]
    return y * c + pltpu.roll(y, n - half, 1) * sa + pltpu.roll(y, half, 1) * sb


def _ada_kernel(c_ref, w_ref, b_ref, o_ref):
    s = _silu(c_ref[...]).astype(BF16)
    o_ref[0] = _dot(s, w_ref[0].astype(BF16)) + b_ref[0]


def _ada(cc, w_ada, b_ada):
    depth, d, n = w_ada.shape
    rows = cc.shape[0]
    tn = n // 4
    return pl.pallas_call(
        _ada_kernel,
        grid=(depth, n // tn),
        in_specs=[
            pl.BlockSpec((rows, d), lambda l, j: (0, 0)),
            pl.BlockSpec((1, d, tn), lambda l, j: (l, 0, j)),
            pl.BlockSpec((1, 1, tn), lambda l, j: (l, 0, j)),
        ],
        out_specs=pl.BlockSpec((1, rows, tn), lambda l, j: (l, 0, j)),
        out_shape=jax.ShapeDtypeStruct((depth, rows, n), F32),
        compiler_params=_params(("parallel", "parallel")),
        name="ada_mod",
    )(cc, w_ada, b_ada.reshape(depth, 1, n))


def _inproj_kernel(x_ref, mod_ref, g_ref, w_ref, o_ref):
    h = _rmsnorm_mod(x_ref[0], g_ref[...], mod_ref[0, 0, 0:1, :], mod_ref[0, 0, 1:2, :])
    o_ref[0] = _dot(h.astype(BF16), w_ref[...])


def _inproj(x_all, mod, g, w, n_lat_tiles):
    b, t, d = x_all.shape
    n = w.shape[1]
    return pl.pallas_call(
        _inproj_kernel,
        grid=(b, t // ROW_TILE),
        in_specs=[
            pl.BlockSpec((1, ROW_TILE, d), lambda i, j: (i, j, 0)),
            pl.BlockSpec((1, 1, 2, d), lambda i, j: (i, j // n_lat_tiles, 0, 0)),
            pl.BlockSpec((1, d), lambda i, j: (0, 0)),
            pl.BlockSpec((d, n), lambda i, j: (0, 0)),
        ],
        out_specs=pl.BlockSpec((1, ROW_TILE, n), lambda i, j: (i, j, 0)),
        out_shape=jax.ShapeDtypeStruct((b, t, n), F32),
        compiler_params=_params(("parallel", "parallel")),
        name="in_proj",
    )(x_all, mod, g, w)


def _ret_kernel(lg_ref, p_ref, c_ref, sa_ref, sb_ref, o_ref,
                q_s, k_s, uf_s, ub_s, sf_s, sb_s, *, n_lat, n_ctx, n_out):
    ch = RET_CHUNK
    pair = pl.program_id(1)
    lane = lax.broadcasted_iota(jnp.int32, (1, LANES), 1)
    lo = lane < RET_DIM
    row = lax.broadcasted_iota(jnp.int32, (ch, 1), 0).astype(F32)
    col = lax.broadcasted_iota(jnp.int32, (1, ch), 1).astype(F32)

    lf_a, lf_b = lg_ref[0, 2 * pair], lg_ref[0, 2 * pair + 1]
    lb_a, lb_b = lg_ref[1, 2 * pair], lg_ref[1, 2 * pair + 1]
    lf = jnp.where(lo, lf_a, lf_b)
    lb = jnp.where(lo, lb_a, lb_b)
    wqf = jnp.exp(lf * (row + 1.0))
    wqb = jnp.exp(lb * (ch - row))
    wkf = jnp.exp(lf * (ch - 1.0 - row))
    wkb = jnp.exp(lb * row)
    gf = jnp.exp(lf * ch)
    gb = jnp.exp(lb * ch)
    diff = row - col

    def decay(l_f, l_b):
        return jnp.where(diff >= 0, jnp.exp(l_f * jnp.maximum(diff, 0.0)),
                         jnp.exp(l_b * jnp.maximum(-diff, 0.0)))

    d_a = decay(lf_a, lb_a)
    d_b = decay(lf_b, lb_b)
    head_r = lax.broadcasted_iota(jnp.int32, (LANES, 1), 0) // RET_DIM
    blockdiag = (head_r == lane // RET_DIM).astype(F32)

    n_all = n_lat + n_ctx

    def phase1(n, carry):
        rows = pl.ds(pl.multiple_of(n * ch, ch), ch)
        c, sa, sb = c_ref[rows, :], sa_ref[rows, :], sb_ref[rows, :]
        q = _rope(p_ref[0, rows, 0:LANES], c, sa, sb, RET_DIM // 2)
        k = _rope(p_ref[0, rows, LANES:2 * LANES], c, sa, sb, RET_DIM // 2) * (RET_DIM ** -0.5)
        v = p_ref[0, rows, 2 * LANES:3 * LANES].astype(BF16)
        q_s[rows, :] = q
        k_s[rows, :] = k
        uf_s[n] = _dot((k * wkf).T.astype(BF16), v)
        ub_s[n] = _dot((k * wkb).T.astype(BF16), v)
        return carry

    lax.fori_loop(0, n_all, phase1, 0)

    order_f = list(range(n_lat, n_all)) + list(range(n_lat))
    order_b = list(range(n_all - 1, -1, -1))
    s = jnp.zeros((LANES, LANES), F32)
    for n in order_f:
        sf_s[n] = (s * blockdiag).astype(BF16)
        s = gf * s + uf_s[n]
    s = jnp.zeros((LANES, LANES), F32)
    for n in order_b:
        sb_s[n] = (s * blockdiag).astype(BF16)
        s = gb * s + ub_s[n]

    def phase3(n, carry):
        rows = pl.ds(pl.multiple_of(n * ch, ch), ch)
        q = q_s[rows, :]
        kb = k_s[rows, :].astype(BF16)
        vb = p_ref[0, rows, 2 * LANES:3 * LANES].astype(BF16)
        gate = p_ref[0, rows, 3 * LANES:4 * LANES]
        a_a = _dot_nt(jnp.where(lo, q, 0.0).astype(BF16), kb) * d_a
        a_b = _dot_nt(jnp.where(lo, 0.0, q).astype(BF16), kb) * d_b
        o = jnp.where(lo, _dot(a_a.astype(BF16), vb), _dot(a_b.astype(BF16), vb))
        o = o + _dot((q * wqf).astype(BF16), sf_s[n]) + _dot((q * wqb).astype(BF16), sb_s[n])
        inv = 1.0 / RET_DIM
        mu = jnp.where(lo, jnp.sum(jnp.where(lo, o, 0.0), axis=-1, keepdims=True),
                       jnp.sum(jnp.where(lo, 0.0, o), axis=-1, keepdims=True)) * inv
        dlt = o - mu
        sq = dlt * dlt
        var = jnp.where(lo, jnp.sum(jnp.where(lo, sq, 0.0), axis=-1, keepdims=True),
                        jnp.sum(jnp.where(lo, 0.0, sq), axis=-1, keepdims=True)) * inv
        o_ref[0, rows, :] = (_silu(gate) * (dlt * lax.rsqrt(var + EPS))).astype(o_ref.dtype)
        return carry

    lax.fori_loop(0, n_out, phase3, 0)


def _retention(lg, proj, tabs, n_lat_rows, out_rows):
    b, t, _ = proj.shape
    ch = RET_CHUNK
    n_lat, n_ctx, n_out = n_lat_rows // ch, (t - n_lat_rows) // ch, out_rows // ch
    n_all = n_lat + n_ctx
    kern = functools.partial(_ret_kernel, n_lat=n_lat, n_ctx=n_ctx, n_out=n_out)
    tab_spec = pl.BlockSpec((t, LANES), lambda i, p: (0, 0))
    return pl.pallas_call(
        kern,
        grid=(b, RET_HEADS // 2),
        in_specs=[
            pl.BlockSpec(memory_space=pltpu.SMEM),
            pl.BlockSpec((1, t, 4 * LANES), lambda i, p: (i, 0, p)),
            tab_spec, tab_spec, tab_spec,
        ],
        out_specs=pl.BlockSpec((1, out_rows, LANES), lambda i, p: (i, 0, p)),
        out_shape=jax.ShapeDtypeStruct((b, out_rows, RET_W), BF16),
        scratch_shapes=[
            pltpu.VMEM((t, LANES), F32), pltpu.VMEM((t, LANES), F32),
            pltpu.VMEM((n_all, LANES, LANES), F32), pltpu.VMEM((n_all, LANES, LANES), F32),
            pltpu.VMEM((n_all, LANES, LANES), BF16), pltpu.VMEM((n_all, LANES, LANES), BF16),
        ],
        compiler_params=_params(("parallel", "parallel")),
        name="retention",
    )(lg, proj, *tabs)


def _qup_kernel(cq_ref, g_ref, w_ref, c_ref, sa_ref, sb_ref, o_ref):
    x = cq_ref[0]
    y = x * lax.rsqrt(jnp.mean(x * x, axis=-1, keepdims=True) + EPS) * g_ref[...]
    q = _dot(y.astype(BF16), w_ref[...])
    scale = (MLA_NOPE + MLA_ROPE) ** -0.5
    c, sa, sb = c_ref[...], sa_ref[...], sb_ref[...]
    for h in range(MLA_HEADS):
        qh = _rope(q[:, h * HEAD_PAD:(h + 1) * HEAD_PAD], c, sa, sb, MLA_ROPE // 2)
        o_ref[0, h] = (qh * scale).astype(o_ref.dtype)


def _qup(proj, g, w, tabs, rows):
    b = proj.shape[0]
    tab_spec = pl.BlockSpec((ROW_TILE, HEAD_PAD), lambda i, j: (j, 0))
    return pl.pallas_call(
        _qup_kernel,
        grid=(b, rows // ROW_TILE),
        in_specs=[
            pl.BlockSpec((1, ROW_TILE, MLA_Q_RANK), lambda i, j: (i, j, COL_CQ // MLA_Q_RANK)),
            pl.BlockSpec((1, MLA_Q_RANK), lambda i, j: (0, 0)),
            pl.BlockSpec(w.shape, lambda i, j: (0, 0)),
            tab_spec, tab_spec, tab_spec,
        ],
        out_specs=pl.BlockSpec((1, MLA_HEADS, ROW_TILE, HEAD_PAD), lambda i, j: (i, 0, j, 0)),
        out_shape=jax.ShapeDtypeStruct((b, MLA_HEADS, rows, HEAD_PAD), BF16),
        compiler_params=_params(("parallel", "parallel")),
        name="mla_q_up",
    )(proj, g, w, *tabs)


def _kvup_kernel(ckv_ref, kpe_ref, g_ref, w_ref, c_ref, sa_ref, sb_ref, k_ref, v_ref):
    x = ckv_ref[0]
    y = x * lax.rsqrt(jnp.mean(x * x, axis=-1, keepdims=True) + EPS) * g_ref[...]
    kv = _dot(y.astype(BF16), w_ref[...])
    kpe = _rope(kpe_ref[0], c_ref[...], sa_ref[...], sb_ref[...], MLA_ROPE // 2)
    lane = lax.broadcasted_iota(jnp.int32, (1, HEAD_PAD), 1)
    ones_col = (lane == MLA_V).astype(F32)
    nh = MLA_HEADS
    for h in range(nh):
        k_ref[0, h] = (kv[:, h * HEAD_PAD:(h + 1) * HEAD_PAD] + kpe).astype(k_ref.dtype)
        v_ref[0, h] = (kv[:, (nh + h) * HEAD_PAD:(nh + h + 1) * HEAD_PAD] + ones_col).astype(v_ref.dtype)


def _kvup(proj, g, w, tabs):
    b, t, _ = proj.shape
    tab_spec = pl.BlockSpec((ROW_TILE, HEAD_PAD), lambda i, j: (j, 0))
    hv_spec = pl.BlockSpec((1, MLA_HEADS, ROW_TILE, HEAD_PAD), lambda i, j: (i, 0, j, 0))
    shape = jax.ShapeDtypeStruct((b, MLA_HEADS, t, HEAD_PAD), BF16)
    return pl.pallas_call(
        _kvup_kernel,
        grid=(b, t // ROW_TILE),
        in_specs=[
            pl.BlockSpec((1, ROW_TILE, MLA_KV_RANK), lambda i, j: (i, j, COL_CKV // MLA_KV_RANK)),
            pl.BlockSpec((1, ROW_TILE, HEAD_PAD), lambda i, j: (i, j, COL_KPE // HEAD_PAD)),
            pl.BlockSpec((1, MLA_KV_RANK), lambda i, j: (0, 0)),
            pl.BlockSpec(w.shape, lambda i, j: (0, 0)),
            tab_spec, tab_spec, tab_spec,
        ],
        out_specs=[hv_spec, hv_spec],
        out_shape=[shape, shape],
        compiler_params=_params(("parallel", "parallel")),
        name="mla_kv_up",
    )(proj, proj, g, w, *tabs)


def _attend(q, k, v):
    s = _dot_nt(q, k)
    p = jnp.exp(s - jnp.max(s, axis=-1, keepdims=True)).astype(BF16)
    o = _dot(p, v)
    return o * (1.0 / o[:, MLA_V:MLA_V + 1])


def _attn_kernel(q_ref, k_ref, v_ref, o_ref, *, n_lat_tiles, n_lat_rows):
    lo = lax.broadcasted_iota(jnp.int32, (1, HEAD_PAD), 1) < MLA_V

    def run(key_rows):
        for hp in range(MLA_HEADS // 2):
            outs = []
            for h in (2 * hp, 2 * hp + 1):
                outs.append(_attend(q_ref[0, h], k_ref[0, h, key_rows, :], v_ref[0, h, key_rows, :]))
            o_ref[0, :, hp * HEAD_PAD:(hp + 1) * HEAD_PAD] = jnp.where(
                lo, outs[0], pltpu.roll(outs[1], MLA_V, 1)).astype(o_ref.dtype)

    is_lat = pl.program_id(1) < n_lat_tiles

    @pl.when(is_lat)
    def _():
        run(slice(None))

    @pl.when(jnp.logical_not(is_lat))
    def _():
        run(slice(n_lat_rows, None))


def _attention(q, k, v, n_lat_rows):
    b, nh, rows, _ = q.shape
    t = k.shape[2]
    kern = functools.partial(_attn_kernel, n_lat_tiles=n_lat_rows // ROW_TILE, n_lat_rows=n_lat_rows)
    kv_spec = pl.BlockSpec((1, nh, t, HEAD_PAD), lambda i, j: (i, 0, 0, 0))
    return pl.pallas_call(
        kern,
        grid=(b, rows // ROW_TILE),
        in_specs=[pl.BlockSpec((1, nh, ROW_TILE, HEAD_PAD), lambda i, j: (i, 0, j, 0)), kv_spec, kv_spec],
        out_specs=pl.BlockSpec((1, ROW_TILE, MLA_W), lambda i, j: (i, j, 0)),
        out_shape=jax.ShapeDtypeStruct((b, rows, MLA_W), BF16),
        compiler_params=_params(("parallel", "arbitrary")),
        name="mla_attention",
    )(q, k, v)


def _pool_kernel(u_ref, w_ref, s_ref, o_ref, pad_s, *, segments, out_rows):
    lane = lax.broadcasted_iota(jnp.int32, (1, LANES), 1)
    lo = lane < POOL_GDIM
    halo = POOL_HALO
    for start, length in segments:
        if start >= out_rows:
            continue
        pad_s[0:halo, :] = jnp.zeros((halo, POOL_W), F32)
        pad_s[halo:halo + length, :] = u_ref[0, start:start + length, :]
        pad_s[halo + length:2 * halo + length, :] = jnp.zeros((halo, POOL_W), F32)
        for r0 in range(0, length, ROW_TILE):
            t = (lax.broadcasted_iota(jnp.int32, (ROW_TILE, 1), 0) + r0).astype(F32)
            cols = []
            for half in range(POOL_W // LANES):
                cs = slice(half * LANES, (half + 1) * LANES)

                def window(w):
                    tot = pad_s[halo + r0 - w // 2:halo + r0 - w // 2 + ROW_TILE, cs]
                    for d in range(-w // 2 + 1, w // 2):
                        tot = tot + pad_s[halo + r0 + d:halo + r0 + d + ROW_TILE, cs]
                    cnt = (jnp.clip(t - w // 2 + w, 0.0, float(length))
                           - jnp.clip(t - w // 2, 0.0, float(length)))
                    return tot / cnt

                w_lo, w_hi = POOL_WINDOWS[2 * half], POOL_WINDOWS[2 * half + 1]
                pooled = jnp.where(lo, window(w_lo), window(w_hi))
                cols.append(pooled - u_ref[0, start + r0:start + r0 + ROW_TILE, cs])
            pooled = jnp.concatenate(cols, axis=-1).astype(BF16)
            y = _dot(pooled, w_ref[...]) * s_ref[...]
            o_ref[0, start + r0:start + r0 + ROW_TILE, :] = y.astype(o_ref.dtype)


def _pool(proj, w_bd, scale, n_lat_rows, out_rows):
    b, t, _ = proj.shape
    segments = ((0, n_lat_rows), (n_lat_rows, t - n_lat_rows))
    kern = functools.partial(_pool_kernel, segments=segments, out_rows=out_rows)
    return pl.pallas_call(
        kern,
        grid=(b,),
        in_specs=[
            pl.BlockSpec((1, t, POOL_W), lambda i: (i, 0, COL_U // POOL_W)),
            pl.BlockSpec((POOL_W, POOL_W), lambda i: (0, 0)),
            pl.BlockSpec((1, POOL_W), lambda i: (0, 0)),
        ],
        out_specs=pl.BlockSpec((1, out_rows, POOL_W), lambda i: (i, 0, 0)),
        out_shape=jax.ShapeDtypeStruct((b, out_rows, POOL_W), BF16),
        scratch_shapes=[pltpu.VMEM((max(n_lat_rows, t - n_lat_rows) + 2 * POOL_HALO, POOL_W), F32)],
        compiler_params=_params(("parallel",)),
        name="pool",
    )(proj, w_bd, scale)


def _outproj_kernel(x_ref, r_ref, m_ref, p_ref, w_ref, gate_ref, o_ref):
    y = (_dot(r_ref[0], w_ref[0:RET_W, :])
         + _dot(m_ref[0], w_ref[RET_W:RET_W + MLA_W, :])
         + _dot(p_ref[0], w_ref[RET_W + MLA_W:, :]))
    o_ref[0] = x_ref[0] + gate_ref[0, 0] * y


def _outproj(x_all, ret, mla, pool, w, gate, n_lat_tiles, rows):
    b, _, d = x_all.shape

    def rt(width):
        return pl.BlockSpec((1, ROW_TILE, width), lambda i, j: (i, j, 0))

    return pl.pallas_call(
        _outproj_kernel,
        grid=(b, rows // ROW_TILE),
        in_specs=[
            rt(d), rt(RET_W), rt(MLA_W), rt(POOL_W),
            pl.BlockSpec(w.shape, lambda i, j: (0, 0)),
            pl.BlockSpec((1, 1, 1, d), lambda i, j: (i, j // n_lat_tiles, 0, 0)),
        ],
        out_specs=rt(d),
        out_shape=jax.ShapeDtypeStruct((b, rows, d), F32),
        compiler_params=_params(("parallel", "parallel")),
        name="out_proj",
    )(x_all, ret, mla, pool, w, gate)


def _mlp_kernel(x_ref, mod_ref, g_ref, w1_ref, w2_ref, gf_ref, o_ref, *, final_norm):
    x = x_ref[0]
    h = _rmsnorm_mod(x, g_ref[...], mod_ref[0, 0, 0:1, :], mod_ref[0, 0, 1:2, :])
    a = jnp.maximum(_dot(h.astype(BF16), w1_ref[...]), 0.0)
    y = x + mod_ref[0, 0, 2:3, :] * _dot((a * a).astype(BF16), w2_ref[...])
    if final_norm:
        y = y * lax.rsqrt(jnp.mean(y * y, axis=-1, keepdims=True) + EPS) * gf_ref[...]
    o_ref[0] = y


def _mlp(x_all, mod, g, w1, w2, g_final, n_lat_tiles, rows, final_norm):
    b, _, d = x_all.shape
    kern = functools.partial(_mlp_kernel, final_norm=final_norm)
    resident = dict(pipeline_mode=pl.Buffered(1))
    return pl.pallas_call(
        kern,
        grid=(b, rows // ROW_TILE),
        in_specs=[
            pl.BlockSpec((1, ROW_TILE, d), lambda i, j: (i, j, 0)),
            pl.BlockSpec((1, 1, 3, d), lambda i, j: (i, j // n_lat_tiles, 0, 0)),
            pl.BlockSpec((1, d), lambda i, j: (0, 0)),
            pl.BlockSpec(w1.shape, lambda i, j: (0, 0), **resident),
            pl.BlockSpec(w2.shape, lambda i, j: (0, 0), **resident),
            pl.BlockSpec((1, d), lambda i, j: (0, 0)),
        ],
        out_specs=pl.BlockSpec((1, ROW_TILE, d), lambda i, j: (i, j, 0)),
        out_shape=jax.ShapeDtypeStruct((b, rows, d), F32),
        compiler_params=_params(("parallel", "parallel")),
        name="mlp",
    )(x_all, mod, g, w1, w2, g_final)


def _in_proj_columns():
    src = np.full((IN_COLS,), 4 * RET_W + MLA_Q_RANK + MLA_KV_RANK + MLA_ROPE + POOL_W, np.int32)
    for p in range(RET_HEADS // 2):
        for part in range(4):
            for hh in range(2):
                h = 2 * p + hh
                dst = COL_RET + p * 4 * LANES + part * LANES + hh * RET_DIM
                src[dst:dst + RET_DIM] = part * RET_W + h * RET_DIM + np.arange(RET_DIM)
    base = 4 * RET_W
    src[COL_CQ:COL_CQ + MLA_Q_RANK] = base + np.arange(MLA_Q_RANK)
    src[COL_CKV:COL_CKV + MLA_KV_RANK] = base + MLA_Q_RANK + np.arange(MLA_KV_RANK)
    src[COL_KPE + MLA_NOPE:COL_KPE + MLA_NOPE + MLA_ROPE] = base + MLA_Q_RANK + MLA_KV_RANK + np.arange(MLA_ROPE)
    src[COL_U:COL_U + POOL_W] = base + MLA_Q_RANK + MLA_KV_RANK + MLA_ROPE + np.arange(POOL_W)
    return src


def _uq_columns():
    per = MLA_NOPE + MLA_ROPE
    src = np.full((MLA_HEADS * HEAD_PAD,), MLA_HEADS * per, np.int32)
    for h in range(MLA_HEADS):
        src[h * HEAD_PAD:h * HEAD_PAD + per] = h * per + np.arange(per)
    return src


def _ukv_columns():
    per = MLA_NOPE + MLA_V
    src = np.full((2 * MLA_HEADS * HEAD_PAD,), MLA_HEADS * per, np.int32)
    for h in range(MLA_HEADS):
        src[h * HEAD_PAD:h * HEAD_PAD + MLA_NOPE] = h * per + np.arange(MLA_NOPE)
        v0 = (MLA_HEADS + h) * HEAD_PAD
        src[v0:v0 + MLA_V] = h * per + MLA_NOPE + np.arange(MLA_V)
    return src


def _take_cols(w, src):
    w = jnp.concatenate([w, jnp.zeros((w.shape[0], 1), w.dtype)], axis=1)
    return jnp.take(w, jnp.asarray(src), axis=1).astype(BF16)


def _rope_tables(n_lat_rows, n_rows, dim, lane0, block):
    pos = np.arange(n_lat_rows)
    n_freq = dim // 4
    inv = ROPE_BASE ** (-np.arange(n_freq, dtype=np.float32) / n_freq)
    ang = np.concatenate([(pos // GRID_W).astype(np.float32)[:, None] * inv,
                          (pos % GRID_W).astype(np.float32)[:, None] * inv], axis=-1)
    ang = jnp.asarray(ang, F32)
    cos, sin = jnp.cos(ang), jnp.sin(ang)
    c = jnp.ones((n_rows, LANES), F32)
    sa = jnp.zeros((n_rows, LANES), F32)
    sb = jnp.zeros((n_rows, LANES), F32)
    half = dim // 2
    for l0 in range(lane0, LANES, block):
        c = c.at[:n_lat_rows, l0:l0 + dim].set(jnp.concatenate([cos, cos], axis=-1))
        sa = sa.at[:n_lat_rows, l0:l0 + half].set(-sin)
        sb = sb.at[:n_lat_rows, l0 + half:l0 + dim].set(sin)
        if block >= LANES:
            break
    return c, sa, sb


def _block_diag(w_pool):
    g, c, _ = w_pool.shape
    out = jnp.zeros((g * c, g * c), w_pool.dtype)
    for i in range(g):
        out = out.at[i * c:(i + 1) * c, i * c:(i + 1) * c].set(w_pool[i])
    return out


def kernel(x, c, ctx, c_ctx, w_ada, b_ada, norm_mix, w_in, q_norm, w_uq, kv_norm, w_ukv,
           ret_decay_logit, w_pool, pool_scale, w_out, norm_mlp, w_ff1, w_ff2, norm_final):
    b, n_lat, d = x.shape
    n_ctx = ctx.shape[1]
    t = n_lat + n_ctx
    depth = w_ada.shape[0]
    assert n_lat % ROW_TILE == 0 and n_ctx % ROW_TILE == 0 and n_lat % GRID_W == 0
    assert d == 4 * RET_W
    n_lat_tiles = n_lat // ROW_TILE

    in_cols, uq_cols, ukv_cols = _in_proj_columns(), _uq_columns(), _ukv_columns()
    ret_tabs = _rope_tables(n_lat, t, RET_DIM, 0, RET_DIM)
    mla_tabs = _rope_tables(n_lat, t, MLA_ROPE, MLA_NOPE, LANES)

    rows = -(-(b + 1) // 8) * 8
    cc = jnp.zeros((rows, d), F32).at[:b].set(c).at[b].set(c_ctx)
    mod = _ada(cc, w_ada, b_ada).reshape(depth, rows, 6, d)
    mod_lat = mod[:, :b]
    mod_ctx = jnp.broadcast_to(mod[:, b:b + 1], mod_lat.shape)
    mod_sel = jnp.stack([mod_lat, mod_ctx], axis=2)

    x_all = jnp.concatenate([x, ctx], axis=1)
    lg = jax.nn.log_sigmoid(ret_decay_logit.astype(F32))

    for l in range(depth):
        last = l == depth - 1
        rows_out = n_lat if last else t
        proj = _inproj(x_all, mod_sel[l, :, :, 0:2], norm_mix[l][None], _take_cols(w_in[l], in_cols), n_lat_tiles)
        ret = _retention(lg[l], proj, ret_tabs, n_lat, rows_out)
        q = _qup(proj, q_norm[l][None], _take_cols(w_uq[l], uq_cols), mla_tabs, rows_out)
        k, v = _kvup(proj, kv_norm[l][None], _take_cols(w_ukv[l], ukv_cols), mla_tabs)
        mla = _attention(q, k, v, n_lat)
        pool = _pool(proj, _block_diag(w_pool[l]).astype(BF16), pool_scale[l][None], n_lat, rows_out)
        x_all = _outproj(x_all, ret, mla, pool, w_out[l].astype(BF16), mod_sel[l, :, :, 2:3], n_lat_tiles, rows_out)
        x_all = _mlp(x_all, mod_sel[l, :, :, 3:6], norm_mlp[l][None], w_ff1[l].astype(BF16),
                     w_ff2[l].astype(BF16), norm_final[None], n_lat_tiles, rows_out, last)
    return x_all
```

```python
import functools
import math

import jax
import jax.numpy as jnp
import numpy as np
from jax import lax
from jax.experimental import pallas as pl
from jax.experimental.pallas import tpu as pltpu

GRID_W = 64
RET_HEADS = 4
RET_DIM = 64
RET_W = RET_HEADS * RET_DIM
RET_CHUNK = 128
MLA_HEADS = 8
MLA_NOPE = 64
MLA_ROPE = 32
MLA_V = 64
MLA_Q_RANK = 256
MLA_KV_RANK = 128
MLA_W = MLA_HEADS * MLA_V
POOL_GROUPS = 4
POOL_WINDOWS = (2, 4, 8, 16)
POOL_GDIM = 64
POOL_W = POOL_GROUPS * POOL_GDIM
ROPE_BASE = 10000.0
EPS = 1e-6

LANES = 128
LAT_TILE = 512
ATTN_TILE = 256
POOL_TILE = 256
HEAD_PAD = 128
POOL_HALO = 16
FF_CHUNK = 1024
RET_UNROLL = 2
VMEM_LIMIT = 56 * 1024 * 1024

COL_RET = 0
COL_CQ = 4 * RET_W
COL_U = COL_CQ + MLA_Q_RANK
COL_CKV = COL_U + POOL_W
COL_KPE = COL_CKV + MLA_KV_RANK
COL_KPE_ROT = COL_KPE + HEAD_PAD
IN_COLS = COL_KPE_ROT + HEAD_PAD

F32 = jnp.float32
BF16 = jnp.bfloat16


def _params(sem):
    return pltpu.CompilerParams(dimension_semantics=sem, vmem_limit_bytes=VMEM_LIMIT)


def _resident(shape):
    return pl.BlockSpec(shape, lambda *_: (0,) * len(shape), pipeline_mode=pl.Buffered(1))


def _dot(a, b):
    return jnp.dot(a, b, preferred_element_type=F32)


def _dot_nt(a, b):
    return lax.dot_general(a, b, (((1,), (1,)), ((), ())), preferred_element_type=F32)


def _rms(x, g):
    return x * lax.rsqrt(jnp.mean(x * x, axis=-1, keepdims=True) + EPS) * g


def _silu(x):
    return x * (1.0 / (1.0 + jnp.exp(-x)))


def _ada_kernel(c_ref, w_ref, b_ref, o_ref):
    s = _silu(c_ref[...]).astype(BF16)
    o_ref[0] = _dot(s, w_ref[0].astype(BF16)) + b_ref[0]


def _ada(cc, w_ada, b_ada):
    depth, d, n = w_ada.shape
    rows = cc.shape[0]
    tn = n // 4
    return pl.pallas_call(
        _ada_kernel,
        grid=(depth, n // tn),
        in_specs=[
            pl.BlockSpec((rows, d), lambda l, j: (0, 0)),
            pl.BlockSpec((1, d, tn), lambda l, j: (l, 0, j)),
            pl.BlockSpec((1, 1, tn), lambda l, j: (l, 0, j)),
        ],
        out_specs=pl.BlockSpec((1, rows, tn), lambda l, j: (l, 0, j)),
        out_shape=jax.ShapeDtypeStruct((depth, rows, n), F32),
        compiler_params=_params(("parallel", "parallel")),
        name="ada_mod",
    )(cc, w_ada, b_ada.reshape(depth, 1, n))


def _inproj_kernel(*refs, rope):
    if rope:
        (x_ref, mod_ref, g_ref, w_ref, gq_ref, wq_ref, gkv_ref, wkv_ref, c_ref, s_ref,
         ret_ref, u_ref, q_ref, k_ref, v_ref) = refs
    else:
        (x_ref, mod_ref, g_ref, w_ref, gq_ref, wq_ref, gkv_ref, wkv_ref,
         ret_ref, u_ref, q_ref, k_ref, v_ref) = refs
    nh = MLA_HEADS
    h = _rms(x_ref[0], g_ref[...]) * (1.0 + mod_ref[0, 1:2, :]) + mod_ref[0, 0:1, :]
    p = _dot(h.astype(BF16), w_ref[...])
    ret_ref[0] = p[:, COL_RET:COL_CQ].astype(ret_ref.dtype)
    u_ref[0] = p[:, COL_U:COL_CKV].astype(u_ref.dtype)

    q_scale = (MLA_NOPE + MLA_ROPE) ** -0.5 * math.log2(math.e)
    yq = _rms(p[:, COL_CQ:COL_U], gq_ref[...]).astype(BF16)
    if rope:
        cos, sin = c_ref[...], s_ref[...]
        q2 = _dot(yq, wq_ref[...])
    else:
        q2 = _dot(yq, wq_ref[:, 0:nh * HEAD_PAD])
    for hh in range(nh):
        qh = q2[:, hh * HEAD_PAD:(hh + 1) * HEAD_PAD]
        if rope:
            qh = qh * cos + q2[:, (nh + hh) * HEAD_PAD:(nh + hh + 1) * HEAD_PAD] * sin
        q_ref[0, hh] = (qh * q_scale).astype(q_ref.dtype)

    ykv = _rms(p[:, COL_CKV:COL_KPE], gkv_ref[...]).astype(BF16)
    kv = _dot(ykv, wkv_ref[...])
    kpe = p[:, COL_KPE:COL_KPE_ROT]
    if rope:
        kpe = kpe * cos + p[:, COL_KPE_ROT:IN_COLS] * sin
    lane = lax.broadcasted_iota(jnp.int32, (1, HEAD_PAD), 1)
    ones_col = (lane == MLA_V).astype(F32)
    for hh in range(nh):
        k_ref[0, hh] = (kv[:, hh * HEAD_PAD:(hh + 1) * HEAD_PAD] + kpe).astype(k_ref.dtype)
        v_ref[0, hh] = (kv[:, (nh + hh) * HEAD_PAD:(nh + hh + 1) * HEAD_PAD] + ones_col).astype(v_ref.dtype)


def _inproj(x, mod, g, w, gq, wq, gkv, wkv, tabs, tile):
    b, r, d = x.shape
    rope = tabs is not None
    nh = MLA_HEADS
    row = lambda width: pl.BlockSpec((1, tile, width), lambda i, j: (i, j, 0))
    head = pl.BlockSpec((1, nh, tile, HEAD_PAD), lambda i, j: (i, 0, j, 0))
    in_specs = [
        row(d),
        pl.BlockSpec((1, 2, d), lambda i, j: (i, 0, 0)),
        _resident(g.shape), _resident(w.shape), _resident(gq.shape), _resident(wq.shape),
        _resident(gkv.shape), _resident(wkv.shape),
    ]
    args = [x, mod, g, w, gq, wq, gkv, wkv]
    if rope:
        in_specs += [pl.BlockSpec((tile, HEAD_PAD), lambda i, j: (j, 0))] * 2
        args += list(tabs)
    hshape = jax.ShapeDtypeStruct((b, nh, r, HEAD_PAD), BF16)
    return pl.pallas_call(
        functools.partial(_inproj_kernel, rope=rope),
        grid=(b, r // tile),
        in_specs=in_specs,
        out_specs=[row(4 * RET_W), row(POOL_W), head, head, head],
        out_shape=[jax.ShapeDtypeStruct((b, r, 4 * RET_W), BF16),
                   jax.ShapeDtypeStruct((b, r, POOL_W), BF16), hshape, hshape, hshape],
        compiler_params=_params(("parallel", "parallel")),
        name="in_proj",
    )(*args)


def _ret_kernel(*refs, n_lat, n_ctx, need_ctx):
    if need_ctx:
        (lg_ref, pl_ref, pc_ref, c_ref, sa_ref, sb_ref, ol_ref, oc_ref,
         q_s, k_s, uf_s, ub_s, sf_s, sb_s) = refs
    else:
        (lg_ref, pl_ref, pc_ref, c_ref, sa_ref, sb_ref, ol_ref,
         q_s, k_s, uf_s, ub_s, sf_s, sb_s) = refs
        oc_ref = None
    ch = RET_CHUNK
    pair = pl.program_id(1)
    lane = lax.broadcasted_iota(jnp.int32, (1, LANES), 1)
    lo = lane < RET_DIM
    sub_lo = lax.broadcasted_iota(jnp.int32, (LANES, 1), 0) < RET_DIM
    row = lax.broadcasted_iota(jnp.int32, (ch, 1), 0).astype(F32)
    col = lax.broadcasted_iota(jnp.int32, (1, ch), 1).astype(F32)

    lf_a, lf_b = lg_ref[0, 2 * pair], lg_ref[0, 2 * pair + 1]
    lb_a, lb_b = lg_ref[1, 2 * pair], lg_ref[1, 2 * pair + 1]
    lf = jnp.where(lo, lf_a, lf_b)
    lb = jnp.where(lo, lb_a, lb_b)
    lf_t = jnp.where(sub_lo, lf_a, lf_b)
    lb_t = jnp.where(sub_lo, lb_a, lb_b)
    wqf = jnp.exp(lf * (row + 1.0))
    wqb = jnp.exp(lb * (ch - row))
    wkf_t = jnp.exp(lf_t * (ch - 1.0 - col))
    wkb_t = jnp.exp(lb_t * col)
    gf = jnp.exp(lf * ch)
    gb = jnp.exp(lb * ch)
    diff = row - col

    def decay(l_f, l_b):
        return jnp.where(diff >= 0, jnp.exp(l_f * jnp.maximum(diff, 0.0)),
                         jnp.exp(l_b * jnp.maximum(-diff, 0.0)))

    d_a = decay(lf_a, lb_a)
    d_b = decay(lf_b, lb_b)
    head_of_row = lax.broadcasted_iota(jnp.int32, (LANES, 1), 0) // RET_DIM
    blockdiag = (head_of_row == lane // RET_DIM).astype(F32)
    k_scale = RET_DIM ** -0.5
    half = RET_DIM // 2

    def phase1(src_ref, src_rows, n, rows, roped):
        q = src_ref[0, src_rows, 0:LANES].astype(F32)
        k = src_ref[0, src_rows, LANES:2 * LANES].astype(F32)
        if roped:
            c, sa, sb = c_ref[src_rows, :], sa_ref[src_rows, :], sb_ref[src_rows, :]
            q = q * c + pltpu.roll(q, LANES - half, 1) * sa + pltpu.roll(q, half, 1) * sb
            k = k * c + pltpu.roll(k, LANES - half, 1) * sa + pltpu.roll(k, half, 1) * sb
        k = k * k_scale
        v = src_ref[0, src_rows, 2 * LANES:3 * LANES]
        q_s[rows, :] = q
        k_s[rows, :] = k
        kt = k.T
        uf_s[n] = _dot((kt * wkf_t).astype(BF16), v)
        ub_s[n] = _dot((kt * wkb_t).astype(BF16), v)

    def lat_rows(n):
        return pl.ds(pl.multiple_of(n * ch, ch), ch)

    def phase1_lat(n, carry):
        phase1(pl_ref, lat_rows(n), n, lat_rows(n), True)
        return carry

    lax.fori_loop(0, n_lat, phase1_lat, 0, unroll=RET_UNROLL)
    for m in range(n_ctx):
        phase1(pc_ref, slice(m * ch, (m + 1) * ch), n_lat + m,
               slice((n_lat + m) * ch, (n_lat + m + 1) * ch), False)

    n_all = n_lat + n_ctx
    order_f = list(range(n_lat, n_all)) + list(range(n_lat))
    order_b = list(range(n_all - 1, -1, -1))
    s = jnp.zeros((LANES, LANES), F32)
    for n in order_f:
        sf_s[n] = (s * blockdiag).astype(BF16)
        s = gf * s + uf_s[n]
    s = jnp.zeros((LANES, LANES), F32)
    for n in order_b:
        sb_s[n] = (s * blockdiag).astype(BF16)
        s = gb * s + ub_s[n]

    def phase3(src_ref, src_rows, n, rows, dst_ref):
        q = q_s[rows, :]
        kb = k_s[rows, :].astype(BF16)
        vb = src_ref[0, src_rows, 2 * LANES:3 * LANES]
        gate = src_ref[0, src_rows, 3 * LANES:4 * LANES].astype(F32)
        a_a = _dot_nt(jnp.where(lo, q, 0.0).astype(BF16), kb) * d_a
        a_b = _dot_nt(jnp.where(lo, 0.0, q).astype(BF16), kb) * d_b
        o = jnp.where(lo, _dot(a_a.astype(BF16), vb), _dot(a_b.astype(BF16), vb))
        o = o + _dot((q * wqf).astype(BF16), sf_s[n]) + _dot((q * wqb).astype(BF16), sb_s[n])
        inv = 1.0 / RET_DIM
        mu = jnp.where(lo, jnp.sum(jnp.where(lo, o, 0.0), axis=-1, keepdims=True),
                       jnp.sum(jnp.where(lo, 0.0, o), axis=-1, keepdims=True)) * inv
        dlt = o - mu
        sq = dlt * dlt
        var = jnp.where(lo, jnp.sum(jnp.where(lo, sq, 0.0), axis=-1, keepdims=True),
                        jnp.sum(jnp.where(lo, 0.0, sq), axis=-1, keepdims=True)) * inv
        dst_ref[0, src_rows, :] = (_silu(gate) * (dlt * lax.rsqrt(var + EPS))).astype(dst_ref.dtype)

    def phase3_lat(n, carry):
        phase3(pl_ref, lat_rows(n), n, lat_rows(n), ol_ref)
        return carry

    lax.fori_loop(0, n_lat, phase3_lat, 0, unroll=RET_UNROLL)
    if need_ctx:
        for m in range(n_ctx):
            phase3(pc_ref, slice(m * ch, (m + 1) * ch), n_lat + m,
                   slice((n_lat + m) * ch, (n_lat + m + 1) * ch), oc_ref)


def _retention(lg, ret_lat, ret_ctx, tabs, need_ctx):
    b, rl, _ = ret_lat.shape
    rc = ret_ctx.shape[1]
    ch = RET_CHUNK
    n_lat, n_ctx = rl // ch, rc // ch
    n_all = n_lat + n_ctx
    kern = functools.partial(_ret_kernel, n_lat=n_lat, n_ctx=n_ctx, need_ctx=need_ctx)
    tab_spec = pl.BlockSpec((rl, LANES), lambda i, p: (0, 0), pipeline_mode=pl.Buffered(1))
    out_specs = [pl.BlockSpec((1, rl, LANES), lambda i, p: (i, 0, p))]
    out_shape = [jax.ShapeDtypeStruct((b, rl, RET_W), BF16)]
    if need_ctx:
        out_specs.append(pl.BlockSpec((1, rc, LANES), lambda i, p: (i, 0, p)))
        out_shape.append(jax.ShapeDtypeStruct((b, rc, RET_W), BF16))
    return pl.pallas_call(
        kern,
        grid=(b, RET_HEADS // 2),
        in_specs=[
            pl.BlockSpec(memory_space=pltpu.SMEM),
            pl.BlockSpec((1, rl, 4 * LANES), lambda i, p: (i, 0, p)),
            pl.BlockSpec((1, rc, 4 * LANES), lambda i, p: (i, 0, p)),
            tab_spec, tab_spec, tab_spec,
        ],
        out_specs=out_specs,
        out_shape=out_shape,
        scratch_shapes=[
            pltpu.VMEM((rl + rc, LANES), F32), pltpu.VMEM((rl + rc, LANES), F32),
            pltpu.VMEM((n_all, LANES, LANES), F32), pltpu.VMEM((n_all, LANES, LANES), F32),
            pltpu.VMEM((n_all, LANES, LANES), BF16), pltpu.VMEM((n_all, LANES, LANES), BF16),
        ],
        compiler_params=_params(("parallel", "parallel")),
        name="retention",
    )(lg, ret_lat, ret_ctx, *tabs)


def _attn_kernel(*refs, use_lat):
    if use_lat:
        q_ref, kl_ref, vl_ref, kc_ref, vc_ref, o_ref = refs
    else:
        q_ref, kc_ref, vc_ref, o_ref = refs
    lo = lax.broadcasted_iota(jnp.int32, (1, HEAD_PAD), 1) < MLA_V

    def attend(h):
        q = q_ref[0, h]
        sc = _dot_nt(q, kc_ref[0, h])
        m = jnp.max(sc, axis=-1, keepdims=True)
        if use_lat:
            sl = _dot_nt(q, kl_ref[0, h])
            m = jnp.maximum(m, jnp.max(sl, axis=-1, keepdims=True))
            o = _dot(jnp.exp2(sl - m).astype(BF16), vl_ref[0, h])
            o = o + _dot(jnp.exp2(sc - m).astype(BF16), vc_ref[0, h])
        else:
            o = _dot(jnp.exp2(sc - m).astype(BF16), vc_ref[0, h])
        return o * (1.0 / o[:, MLA_V:MLA_V + 1])

    for hp in range(MLA_HEADS // 2):
        o_ref[0, :, hp * HEAD_PAD:(hp + 1) * HEAD_PAD] = jnp.where(
            lo, attend(2 * hp), pltpu.roll(attend(2 * hp + 1), MLA_V, 1)).astype(o_ref.dtype)


def _attention(q, k_ctx, v_ctx, k_lat=None, v_lat=None):
    b, nh, rows, _ = q.shape
    use_lat = k_lat is not None
    tile = min(ATTN_TILE, rows)

    def whole(a):
        return pl.BlockSpec((1, nh, a.shape[2], HEAD_PAD), lambda i, j: (i, 0, 0, 0))

    in_specs = [pl.BlockSpec((1, nh, tile, HEAD_PAD), lambda i, j: (i, 0, j, 0))]
    args = [q]
    if use_lat:
        in_specs += [whole(k_lat), whole(v_lat)]
        args += [k_lat, v_lat]
    in_specs += [whole(k_ctx), whole(v_ctx)]
    args += [k_ctx, v_ctx]
    return pl.pallas_call(
        functools.partial(_attn_kernel, use_lat=use_lat),
        grid=(b, rows // tile),
        in_specs=in_specs,
        out_specs=pl.BlockSpec((1, tile, MLA_W), lambda i, j: (i, j, 0)),
        out_shape=jax.ShapeDtypeStruct((b, rows, MLA_W), BF16),
        compiler_params=_params(("parallel", "arbitrary")),
        name="mla_attention",
    )(*args)


def _pool_kernel(*refs, lengths):
    n_seg = len(lengths)
    u_refs, (w_ref, s_ref), o_refs, pad_s = refs[:n_seg], refs[n_seg:n_seg + 2], refs[n_seg + 2:-1], refs[-1]
    lo = lax.broadcasted_iota(jnp.int32, (1, LANES), 1) < POOL_GDIM
    halo = POOL_HALO
    for u_ref, o_ref, length in zip(u_refs, o_refs, lengths):
        pad_s[0:halo, :] = jnp.zeros((halo, POOL_W), F32)
        pad_s[halo:halo + length, :] = u_ref[0].astype(F32)
        pad_s[halo + length:2 * halo + length, :] = jnp.zeros((halo, POOL_W), F32)
        for r0 in range(0, length, POOL_TILE):
            t = (lax.broadcasted_iota(jnp.int32, (POOL_TILE, 1), 0) + r0).astype(F32)
            cols = []
            for hf in range(POOL_W // LANES):
                cs = slice(hf * LANES, (hf + 1) * LANES)

                def window(w):
                    base = halo + r0 - w // 2
                    tot = pad_s[base:base + POOL_TILE, cs]
                    for d in range(1, w):
                        tot = tot + pad_s[base + d:base + d + POOL_TILE, cs]
                    cnt = (jnp.clip(t - w // 2 + w, 0.0, float(length))
                           - jnp.clip(t - w // 2, 0.0, float(length)))
                    return tot / cnt

                pooled = jnp.where(lo, window(POOL_WINDOWS[2 * hf]), window(POOL_WINDOWS[2 * hf + 1]))
                cols.append(pooled - pad_s[halo + r0:halo + r0 + POOL_TILE, cs])
            pooled = jnp.concatenate(cols, axis=-1).astype(BF16)
            y = _dot(pooled, w_ref[...]) * s_ref[...]
            o_ref[0, r0:r0 + POOL_TILE, :] = y.astype(o_ref.dtype)


def _pool(us, w_bd, scale):
    b = us[0].shape[0]
    lengths = tuple(u.shape[1] for u in us)
    spec = lambda n: pl.BlockSpec((1, n, POOL_W), lambda i: (i, 0, 0))
    return pl.pallas_call(
        functools.partial(_pool_kernel, lengths=lengths),
        grid=(b,),
        in_specs=[spec(n) for n in lengths] + [_resident(w_bd.shape), _resident(scale.shape)],
        out_specs=[spec(n) for n in lengths],
        out_shape=[jax.ShapeDtypeStruct((b, n, POOL_W), BF16) for n in lengths],
        scratch_shapes=[pltpu.VMEM((max(lengths) + 2 * POOL_HALO, POOL_W), F32)],
        compiler_params=_params(("parallel",)),
        name="pool",
    )(*us, w_bd, scale)


def _mix_mlp_kernel(x_ref, r_ref, m_ref, p_ref, wo_ref, mod_ref, g_ref, w1_ref, w2_ref, gf_ref, o_ref,
                    *, final_norm):
    y = (_dot(r_ref[0], wo_ref[0:RET_W, :])
         + _dot(m_ref[0], wo_ref[RET_W:RET_W + MLA_W, :])
         + _dot(p_ref[0], wo_ref[RET_W + MLA_W:, :]))
    x = x_ref[0] + mod_ref[0, 0:1, :] * y
    h = (_rms(x, g_ref[...]) * (1.0 + mod_ref[0, 2:3, :]) + mod_ref[0, 1:2, :]).astype(BF16)
    acc = None
    for c0 in range(0, w1_ref.shape[1], FF_CHUNK):
        a = jnp.maximum(_dot(h, w1_ref[:, c0:c0 + FF_CHUNK]), 0.0)
        part = _dot((a * a).astype(BF16), w2_ref[c0:c0 + FF_CHUNK, :])
        acc = part if acc is None else acc + part
    y = x + mod_ref[0, 3:4, :] * acc
    if final_norm:
        y = _rms(y, gf_ref[...])
    o_ref[0] = y


def _mix_mlp(x, ret, mla, pool, w_out, mod, g, w1, w2, g_final, tile, final_norm):
    b, r, d = x.shape
    row = lambda width: pl.BlockSpec((1, tile, width), lambda i, j: (i, j, 0))
    return pl.pallas_call(
        functools.partial(_mix_mlp_kernel, final_norm=final_norm),
        grid=(b, r // tile),
        in_specs=[
            row(d), row(RET_W), row(MLA_W), row(POOL_W),
            _resident(w_out.shape),
            pl.BlockSpec((1, 4, d), lambda i, j: (i, 0, 0)),
            _resident(g.shape), _resident(w1.shape), _resident(w2.shape), _resident(g_final.shape),
        ],
        out_specs=row(d),
        out_shape=jax.ShapeDtypeStruct((b, r, d), F32),
        compiler_params=_params(("parallel", "parallel")),
        name="mix_mlp",
    )(x, ret, mla, pool, w_out, mod, g, w1, w2, g_final)


def _in_proj_columns():
    zero = 4 * RET_W + MLA_Q_RANK + MLA_KV_RANK + MLA_ROPE + POOL_W
    src = np.full((IN_COLS,), zero, np.int32)
    sign = np.ones((IN_COLS,), np.float32)
    for p in range(RET_HEADS // 2):
        for part in range(4):
            for hh in range(2):
                h = 2 * p + hh
                dst = COL_RET + p * 4 * LANES + part * LANES + hh * RET_DIM
                src[dst:dst + RET_DIM] = part * RET_W + h * RET_DIM + np.arange(RET_DIM)
    base = 4 * RET_W
    src[COL_CQ:COL_CQ + MLA_Q_RANK] = base + np.arange(MLA_Q_RANK)
    src[COL_CKV:COL_CKV + MLA_KV_RANK] = base + MLA_Q_RANK + np.arange(MLA_KV_RANK)
    kpe0 = base + MLA_Q_RANK + MLA_KV_RANK
    half = MLA_ROPE // 2
    src[COL_KPE + MLA_NOPE:COL_KPE + MLA_NOPE + MLA_ROPE] = kpe0 + np.arange(MLA_ROPE)
    src[COL_KPE_ROT + MLA_NOPE:COL_KPE_ROT + MLA_NOPE + half] = kpe0 + half + np.arange(half)
    sign[COL_KPE_ROT + MLA_NOPE:COL_KPE_ROT + MLA_NOPE + half] = -1.0
    src[COL_KPE_ROT + MLA_NOPE + half:COL_KPE_ROT + MLA_NOPE + MLA_ROPE] = kpe0 + np.arange(half)
    src[COL_U:COL_U + POOL_W] = kpe0 + MLA_ROPE + np.arange(POOL_W)
    return src, sign


def _uq_columns():
    per = MLA_NOPE + MLA_ROPE
    nh = MLA_HEADS
    half = MLA_ROPE // 2
    src = np.full((2 * nh * HEAD_PAD,), nh * per, np.int32)
    sign = np.ones((2 * nh * HEAD_PAD,), np.float32)
    for h in range(nh):
        src[h * HEAD_PAD:h * HEAD_PAD + per] = h * per + np.arange(per)
        r0 = (nh + h) * HEAD_PAD + MLA_NOPE
        src[r0:r0 + half] = h * per + MLA_NOPE + half + np.arange(half)
        sign[r0:r0 + half] = -1.0
        src[r0 + half:r0 + MLA_ROPE] = h * per + MLA_NOPE + np.arange(half)
    return src, sign


def _ukv_columns():
    per = MLA_NOPE + MLA_V
    src = np.full((2 * MLA_HEADS * HEAD_PAD,), MLA_HEADS * per, np.int32)
    for h in range(MLA_HEADS):
        src[h * HEAD_PAD:h * HEAD_PAD + MLA_NOPE] = h * per + np.arange(MLA_NOPE)
        v0 = (MLA_HEADS + h) * HEAD_PAD
        src[v0:v0 + MLA_V] = h * per + MLA_NOPE + np.arange(MLA_V)
    return src, np.ones(src.shape, np.float32)


def _take_cols(w, cols):
    src, sign = cols
    w = jnp.concatenate([w, jnp.zeros((w.shape[0], 1), w.dtype)], axis=1)
    return (jnp.take(w, jnp.asarray(src), axis=1) * jnp.asarray(sign)).astype(BF16)


def _rope_angles(n_rows, dim):
    pos = np.arange(n_rows)
    n_freq = dim // 4
    inv = np.float32(ROPE_BASE) ** (-np.arange(n_freq, dtype=np.float32) / np.float32(n_freq))
    return np.concatenate([(pos // GRID_W).astype(np.float32)[:, None] * inv,
                           (pos % GRID_W).astype(np.float32)[:, None] * inv], axis=-1)


def _ret_rope_tables(n_rows):
    ang = _rope_angles(n_rows, RET_DIM)
    cos, sin = np.cos(ang), np.sin(ang)
    zero = np.zeros_like(sin)
    reps = LANES // RET_DIM
    c = np.tile(np.concatenate([cos, cos], axis=-1), (1, reps))
    sa = np.tile(np.concatenate([-sin, zero], axis=-1), (1, reps))
    sb = np.tile(np.concatenate([zero, sin], axis=-1), (1, reps))
    return tuple(jnp.asarray(a, F32) for a in (c, sa, sb))


def _mla_rope_tables(n_rows):
    ang = _rope_angles(n_rows, MLA_ROPE)
    cos, sin = np.cos(ang), np.sin(ang)
    c = np.ones((n_rows, HEAD_PAD), np.float32)
    s = np.zeros((n_rows, HEAD_PAD), np.float32)
    c[:, MLA_NOPE:MLA_NOPE + MLA_ROPE] = np.concatenate([cos, cos], axis=-1)
    s[:, MLA_NOPE:MLA_NOPE + MLA_ROPE] = np.concatenate([sin, sin], axis=-1)
    return jnp.asarray(c, F32), jnp.asarray(s, F32)


def _block_diag(w_pool):
    g, c, _ = w_pool.shape
    out = jnp.zeros((g * c, g * c), w_pool.dtype)
    for i in range(g):
        out = out.at[i * c:(i + 1) * c, i * c:(i + 1) * c].set(w_pool[i])
    return out


def kernel(x, c, ctx, c_ctx, w_ada, b_ada, norm_mix, w_in, q_norm, w_uq, kv_norm, w_ukv,
           ret_decay_logit, w_pool, pool_scale, w_out, norm_mlp, w_ff1, w_ff2, norm_final):
    b, n_lat, d = x.shape
    n_ctx = ctx.shape[1]
    depth = w_ada.shape[0]
    lat_tile = min(LAT_TILE, n_lat)
    assert n_lat % lat_tile == 0 and n_lat % ATTN_TILE == 0 and n_lat % GRID_W == 0
    assert n_ctx % RET_CHUNK == 0 and n_ctx % POOL_TILE == 0 and n_lat % POOL_TILE == 0
    assert d == 4 * RET_W and w_ff1.shape[2] % FF_CHUNK == 0

    in_cols, uq_cols, ukv_cols = _in_proj_columns(), _uq_columns(), _ukv_columns()
    ret_tabs = _ret_rope_tables(n_lat)
    mla_tabs = _mla_rope_tables(n_lat)

    rows = -(-(b + 1) // 8) * 8
    cc = jnp.zeros((rows, d), F32).at[:b].set(c).at[b].set(c_ctx)
    mod = _ada(cc, w_ada, b_ada).reshape(depth, rows, 6, d)
    mod_lat = mod[:, :b]
    mod_ctx = jnp.broadcast_to(mod[:, b:b + 1], mod_lat.shape)
    lg = jax.nn.log_sigmoid(ret_decay_logit.astype(F32))

    h_ctx = ctx
    for l in range(depth):
        last = l == depth - 1
        w_in_l, w_uq_l, w_ukv_l = _take_cols(w_in[l], in_cols), _take_cols(w_uq[l], uq_cols), _take_cols(w_ukv[l], ukv_cols)
        w_out_l, w1_l, w2_l = w_out[l].astype(BF16), w_ff1[l].astype(BF16), w_ff2[l].astype(BF16)
        norms = (norm_mix[l][None], w_in_l, q_norm[l][None], w_uq_l, kv_norm[l][None], w_ukv_l)

        ret_l, u_l, q_l, k_l, v_l = _inproj(x, mod_lat[l, :, 0:2], *norms, mla_tabs, lat_tile)
        ret_c, u_c, q_c, k_c, v_c = _inproj(h_ctx, mod_ctx[l, :, 0:2], *norms, None, n_ctx)

        ret_o = _retention(lg[l], ret_l, ret_c, ret_tabs, not last)
        mla_l = _attention(q_l, k_c, v_c, k_l, v_l)
        pool_w = (_block_diag(w_pool[l]).astype(BF16), pool_scale[l][None])
        mlp_w = (norm_mlp[l][None], w1_l, w2_l, norm_final[None])
        if last:
            (pool_l,) = _pool((u_l,), *pool_w)
        else:
            pool_l, pool_c = _pool((u_l, u_c), *pool_w)
            mla_c = _attention(q_c, k_c, v_c)
            h_ctx = _mix_mlp(h_ctx, ret_o[1], mla_c, pool_c, w_out_l, mod_ctx[l, :, 2:6], *mlp_w, n_ctx, False)
        x = _mix_mlp(x, ret_o[0], mla_l, pool_l, w_out_l, mod_lat[l, :, 2:6], *mlp_w, lat_tile, last)
    return x
```

```python
import functools
import math

import jax
import jax.numpy as jnp
import numpy as np
from jax import lax
from jax.experimental import pallas as pl
from jax.experimental.pallas import tpu as pltpu

GRID_W = 64
RET_HEADS = 4
RET_DIM = 64
RET_W = RET_HEADS * RET_DIM
RET_CHUNK = 128
MLA_HEADS = 8
MLA_NOPE = 64
MLA_ROPE = 32
MLA_V = 64
MLA_Q_RANK = 256
MLA_KV_RANK = 128
MLA_W = MLA_HEADS * MLA_V
POOL_GROUPS = 4
POOL_WINDOWS = (2, 4, 8, 16)
POOL_GDIM = 64
POOL_W = POOL_GROUPS * POOL_GDIM
ROPE_BASE = 10000.0
EPS = 1e-6

LANES = 128
LAT_TILE = 512
ATTN_TILE = 256
KEY_CHUNK = 256
SCORE_LOOKAHEAD = 5
POOL_TILE = 256
HEAD_PAD = 128
VT_ROWS = MLA_V + 16
POOL_HALO = 16
FF_CHUNK = 1024
VMEM_LIMIT = 56 * 1024 * 1024

COL_RET = 0
COL_CQ = 4 * RET_W
COL_U = COL_CQ + MLA_Q_RANK
COL_CKV = COL_U + POOL_W
COL_KPE = COL_CKV + MLA_KV_RANK
COL_KPE_ROT = COL_KPE + HEAD_PAD
IN_COLS = COL_KPE_ROT + HEAD_PAD

F32 = jnp.float32
BF16 = jnp.bfloat16


def _params(sem):
    return pltpu.CompilerParams(dimension_semantics=sem, vmem_limit_bytes=VMEM_LIMIT)


def _resident(shape):
    return pl.BlockSpec(shape, lambda *_: (0,) * len(shape), pipeline_mode=pl.Buffered(1))


def _dot(a, b):
    return jnp.dot(a, b, preferred_element_type=F32)


def _dot_nt(a, b):
    return lax.dot_general(a, b, (((1,), (1,)), ((), ())), preferred_element_type=F32)


def _rms(x, g):
    return x * lax.rsqrt(jnp.mean(x * x, axis=-1, keepdims=True) + EPS) * g


def _silu(x):
    return x * (1.0 / (1.0 + jnp.exp(-x)))


def _ada_kernel(c_ref, w_ref, b_ref, o_ref):
    s = _silu(c_ref[...]).astype(BF16)
    o_ref[0] = _dot(s, w_ref[0].astype(BF16)) + b_ref[0]


def _ada(cc, w_ada, b_ada):
    depth, d, n = w_ada.shape
    rows = cc.shape[0]
    tn = n // 4
    return pl.pallas_call(
        _ada_kernel,
        grid=(depth, n // tn),
        in_specs=[
            pl.BlockSpec((rows, d), lambda l, j: (0, 0)),
            pl.BlockSpec((1, d, tn), lambda l, j: (l, 0, j)),
            pl.BlockSpec((1, 1, tn), lambda l, j: (l, 0, j)),
        ],
        out_specs=pl.BlockSpec((1, rows, tn), lambda l, j: (l, 0, j)),
        out_shape=jax.ShapeDtypeStruct((depth, rows, n), F32),
        compiler_params=_params(("parallel", "parallel")),
        name="ada_mod",
    )(cc, w_ada, b_ada.reshape(depth, 1, n))


def _inproj_kernel(*refs, rope):
    if rope:
        (x_ref, mod_ref, g_ref, w_ref, gq_ref, wq_ref, gkv_ref, wkv_ref, c_ref, s_ref,
         ret_ref, u_ref, q_ref, k_ref, vt_ref) = refs
    else:
        (x_ref, mod_ref, g_ref, w_ref, gq_ref, wq_ref, gkv_ref, wkv_ref,
         ret_ref, u_ref, q_ref, k_ref, vt_ref) = refs
    nh = MLA_HEADS
    h = _rms(x_ref[0], g_ref[...]) * (1.0 + mod_ref[0, 1:2, :]) + mod_ref[0, 0:1, :]
    p = _dot(h.astype(BF16), w_ref[...])
    ret_ref[0] = p[:, COL_RET:COL_CQ].astype(ret_ref.dtype)
    u_ref[0] = p[:, COL_U:COL_CKV].astype(u_ref.dtype)

    q_scale = (MLA_NOPE + MLA_ROPE) ** -0.5 * math.log2(math.e)
    yq = _rms(p[:, COL_CQ:COL_U], gq_ref[...]).astype(BF16)
    if rope:
        cos, sin = c_ref[...], s_ref[...]
        q2 = _dot(yq, wq_ref[...])
    else:
        q2 = _dot(yq, wq_ref[:, 0:nh * HEAD_PAD])
    for hh in range(nh):
        qh = q2[:, hh * HEAD_PAD:(hh + 1) * HEAD_PAD]
        if rope:
            qh = qh * cos + q2[:, (nh + hh) * HEAD_PAD:(nh + hh + 1) * HEAD_PAD] * sin
        q_ref[0, hh] = (qh * q_scale).astype(q_ref.dtype)

    ykv = _rms(p[:, COL_CKV:COL_KPE], gkv_ref[...]).astype(BF16)
    kv = _dot(ykv, wkv_ref[...])
    kpe = p[:, COL_KPE:COL_KPE_ROT]
    if rope:
        kpe = kpe * cos + p[:, COL_KPE_ROT:IN_COLS] * sin
    lane = lax.broadcasted_iota(jnp.int32, (1, HEAD_PAD), 1)
    ones_col = (lane == MLA_V).astype(F32)
    for hh in range(nh):
        k_ref[0, hh] = (kv[:, hh * HEAD_PAD:(hh + 1) * HEAD_PAD] + kpe).astype(k_ref.dtype)
        v = kv[:, (nh + hh) * HEAD_PAD:(nh + hh + 1) * HEAD_PAD] + ones_col
        vt_ref[0, hh] = v.T.astype(vt_ref.dtype)


def _inproj(x, mod, g, w, gq, wq, gkv, wkv, tabs, tile):
    b, r, d = x.shape
    rope = tabs is not None
    nh = MLA_HEADS
    row = lambda width: pl.BlockSpec((1, tile, width), lambda i, j: (i, j, 0))
    head = pl.BlockSpec((1, nh, tile, HEAD_PAD), lambda i, j: (i, 0, j, 0))
    head_t = pl.BlockSpec((1, nh, HEAD_PAD, tile), lambda i, j: (i, 0, 0, j))
    in_specs = [
        row(d),
        pl.BlockSpec((1, 2, d), lambda i, j: (i, 0, 0)),
        _resident(g.shape), _resident(w.shape), _resident(gq.shape), _resident(wq.shape),
        _resident(gkv.shape), _resident(wkv.shape),
    ]
    args = [x, mod, g, w, gq, wq, gkv, wkv]
    if rope:
        in_specs += [pl.BlockSpec((tile, HEAD_PAD), lambda i, j: (j, 0))] * 2
        args += list(tabs)
    hshape = jax.ShapeDtypeStruct((b, nh, r, HEAD_PAD), BF16)
    return pl.pallas_call(
        functools.partial(_inproj_kernel, rope=rope),
        grid=(b, r // tile),
        in_specs=in_specs,
        out_specs=[row(4 * RET_W), row(POOL_W), head, head, head_t],
        out_shape=[jax.ShapeDtypeStruct((b, r, 4 * RET_W), BF16),
                   jax.ShapeDtypeStruct((b, r, POOL_W), BF16), hshape, hshape,
                   jax.ShapeDtypeStruct((b, nh, HEAD_PAD, r), BF16)],
        compiler_params=_params(("parallel", "parallel")),
        name="in_proj",
    )(*args)


def _ret_kernel(*refs, n_lat, n_ctx, need_ctx):
    if need_ctx:
        (lg_ref, pl_ref, pc_ref, c_ref, sa_ref, sb_ref, ol_ref, oc_ref, q_s, k_s, u_s, st_s) = refs
    else:
        (lg_ref, pl_ref, pc_ref, c_ref, sa_ref, sb_ref, ol_ref, q_s, k_s, u_s, st_s) = refs
        oc_ref = None
    ch = RET_CHUNK
    pair = pl.program_id(1)
    lane = lax.broadcasted_iota(jnp.int32, (1, LANES), 1)
    lo = lane < RET_DIM
    sub_lo = lax.broadcasted_iota(jnp.int32, (LANES, 1), 0) < RET_DIM
    row = lax.broadcasted_iota(jnp.int32, (ch, 1), 0).astype(F32)
    col = lax.broadcasted_iota(jnp.int32, (1, ch), 1).astype(F32)

    lf_a, lf_b = lg_ref[0, 2 * pair], lg_ref[0, 2 * pair + 1]
    lb_a, lb_b = lg_ref[1, 2 * pair], lg_ref[1, 2 * pair + 1]
    lf = jnp.where(lo, lf_a, lf_b)
    lb = jnp.where(lo, lb_a, lb_b)
    lf_t = jnp.where(sub_lo, lf_a, lf_b)
    lb_t = jnp.where(sub_lo, lb_a, lb_b)
    wqf = jnp.exp(lf * (row + 1.0))
    wqb = jnp.exp(lb * (ch - row))
    wkf_t = jnp.exp(lf_t * (ch - 1.0 - col))
    wkb_t = jnp.exp(lb_t * col)
    gf = jnp.exp(lf * ch)
    gb = jnp.exp(lb * ch)
    diff = row - col

    def decay(l_f, l_b):
        return jnp.where(diff >= 0, jnp.exp(l_f * jnp.maximum(diff, 0.0)),
                         jnp.exp(l_b * jnp.maximum(-diff, 0.0)))

    d_ab = jnp.concatenate([decay(lf_a, lb_a), decay(lf_b, lb_b)], axis=0)
    wq_fb = jnp.concatenate([wqf, wqb], axis=1)
    wk_fb_t = jnp.concatenate([wkf_t, wkb_t], axis=0)
    head_of_row = lax.broadcasted_iota(jnp.int32, (LANES, 1), 0) // RET_DIM
    blockdiag = (head_of_row == lane // RET_DIM).astype(F32)
    k_scale = RET_DIM ** -0.5
    half = RET_DIM // 2
    n_all = n_lat + n_ctx

    def source(n):
        if n < n_lat:
            return pl_ref, slice(n * ch, (n + 1) * ch), True
        return pc_ref, slice((n - n_lat) * ch, (n - n_lat + 1) * ch), False

    for n in range(n_all):
        src_ref, src_rows, roped = source(n)
        rows = slice(n * ch, (n + 1) * ch)
        q = src_ref[0, src_rows, 0:LANES].astype(F32)
        k = src_ref[0, src_rows, LANES:2 * LANES].astype(F32)
        if roped:
            c, sa, sb = c_ref[src_rows, :], sa_ref[src_rows, :], sb_ref[src_rows, :]
            q = q * c + pltpu.roll(q, LANES - half, 1) * sa + pltpu.roll(q, half, 1) * sb
            k = k * c + pltpu.roll(k, LANES - half, 1) * sa + pltpu.roll(k, half, 1) * sb
        k = k * k_scale
        q_s[rows, :] = q
        k_s[rows, :] = k
        kt = k.T
        kt2 = jnp.concatenate([kt, kt], axis=0) * wk_fb_t
        u_s[n] = _dot(kt2.astype(BF16), src_ref[0, src_rows, 2 * LANES:3 * LANES])

    order_f = list(range(n_lat, n_all)) + list(range(n_lat))
    order_b = list(range(n_all - 1, -1, -1))
    s = jnp.zeros((LANES, LANES), F32)
    for n in order_f:
        st_s[n, 0:LANES, :] = (s * blockdiag).astype(BF16)
        s = gf * s + u_s[n, 0:LANES, :]
    s = jnp.zeros((LANES, LANES), F32)
    for n in order_b:
        st_s[n, LANES:2 * LANES, :] = (s * blockdiag).astype(BF16)
        s = gb * s + u_s[n, LANES:2 * LANES, :]

    for n in range(n_all if need_ctx else n_lat):
        src_ref, src_rows, _ = source(n)
        dst_ref = ol_ref if n < n_lat else oc_ref
        rows = slice(n * ch, (n + 1) * ch)
        q = q_s[rows, :]
        kb = k_s[rows, :].astype(BF16)
        vb = src_ref[0, src_rows, 2 * LANES:3 * LANES]
        gate = src_ref[0, src_rows, 3 * LANES:4 * LANES].astype(F32)
        q_ab = jnp.concatenate([jnp.where(lo, q, 0.0), jnp.where(lo, 0.0, q)], axis=0).astype(BF16)
        o_ab = _dot((_dot_nt(q_ab, kb) * d_ab).astype(BF16), vb)
        o = jnp.where(lo, o_ab[0:ch], o_ab[ch:2 * ch])
        o = o + _dot((jnp.concatenate([q, q], axis=1) * wq_fb).astype(BF16), st_s[n])
        inv = 1.0 / RET_DIM
        mu = jnp.where(lo, jnp.sum(jnp.where(lo, o, 0.0), axis=-1, keepdims=True),
                       jnp.sum(jnp.where(lo, 0.0, o), axis=-1, keepdims=True)) * inv
        dlt = o - mu
        sq = dlt * dlt
        var = jnp.where(lo, jnp.sum(jnp.where(lo, sq, 0.0), axis=-1, keepdims=True),
                        jnp.sum(jnp.where(lo, 0.0, sq), axis=-1, keepdims=True)) * inv
        dst_ref[0, src_rows, :] = (_silu(gate) * (dlt * lax.rsqrt(var + EPS))).astype(dst_ref.dtype)


def _retention(lg, ret_lat, ret_ctx, tabs, need_ctx):
    b, rl, _ = ret_lat.shape
    rc = ret_ctx.shape[1]
    ch = RET_CHUNK
    n_lat, n_ctx = rl // ch, rc // ch
    n_all = n_lat + n_ctx
    kern = functools.partial(_ret_kernel, n_lat=n_lat, n_ctx=n_ctx, need_ctx=need_ctx)
    tab_spec = pl.BlockSpec((rl, LANES), lambda i, p: (0, 0), pipeline_mode=pl.Buffered(1))
    out_specs = [pl.BlockSpec((1, rl, LANES), lambda i, p: (i, 0, p))]
    out_shape = [jax.ShapeDtypeStruct((b, rl, RET_W), BF16)]
    if need_ctx:
        out_specs.append(pl.BlockSpec((1, rc, LANES), lambda i, p: (i, 0, p)))
        out_shape.append(jax.ShapeDtypeStruct((b, rc, RET_W), BF16))
    return pl.pallas_call(
        kern,
        grid=(b, RET_HEADS // 2),
        in_specs=[
            pl.BlockSpec(memory_space=pltpu.SMEM),
            pl.BlockSpec((1, rl, 4 * LANES), lambda i, p: (i, 0, p)),
            pl.BlockSpec((1, rc, 4 * LANES), lambda i, p: (i, 0, p)),
            tab_spec, tab_spec, tab_spec,
        ],
        out_specs=out_specs,
        out_shape=out_shape,
        scratch_shapes=[
            pltpu.VMEM((rl + rc, LANES), F32), pltpu.VMEM((rl + rc, LANES), F32),
            pltpu.VMEM((n_all, 2 * LANES, LANES), F32),
            pltpu.VMEM((n_all, 2 * LANES, LANES), BF16),
        ],
        compiler_params=_params(("parallel", "parallel")),
        name="retention",
    )(lg, ret_lat, ret_ctx, *tabs)


def _attn_kernel(*refs, use_lat):
    if use_lat:
        q_ref, kl_ref, vtl_ref, kc_ref, vtc_ref, o_ref = refs
    else:
        q_ref, kc_ref, vtc_ref, o_ref = refs

    sources = [(kc_ref, vtc_ref)]
    if use_lat:
        sources.append((kl_ref, vtl_ref))

    items = [(h, k_ref, vt_ref, c0) for h in range(MLA_HEADS) for k_ref, vt_ref in sources
             for c0 in range(0, k_ref.shape[2], KEY_CHUNK)]
    last_item = {h: i for i, (h, _, _, _) in enumerate(items)}

    scores, m_run, acc, done = {}, {}, {}, {}
    for t in range(len(items) + SCORE_LOOKAHEAD):
        if t < len(items):
            h, k_ref, _, c0 = items[t]
            scores[t] = _dot_nt(k_ref[0, h, c0:c0 + KEY_CHUNK, :], q_ref[0, h])
        i = t - SCORE_LOOKAHEAD
        if i < 0:
            continue
        h, _, vt_ref, c0 = items[i]
        s = scores.pop(i)
        m_new = jnp.max(s, axis=0, keepdims=True)
        if h in m_run:
            m_new = jnp.maximum(m_run[h], m_new)
        pv = _dot(vt_ref[0, h, 0:VT_ROWS, c0:c0 + KEY_CHUNK], jnp.exp2(s - m_new).astype(BF16))
        acc[h] = pv if h not in acc else acc[h] * jnp.exp2(m_run[h] - m_new) + pv
        m_run[h] = m_new
        if i == last_item[h]:
            a = acc.pop(h)
            done[h] = a[0:MLA_V, :] * (1.0 / a[MLA_V:MLA_V + 1, :])
            if h % 2 == 1:
                pair_t = jnp.concatenate([done.pop(h - 1), done.pop(h)], axis=0)
                o_ref[0, :, (h // 2) * HEAD_PAD:(h // 2 + 1) * HEAD_PAD] = pair_t.T.astype(o_ref.dtype)


def _attention(q, k_ctx, vt_ctx, k_lat=None, vt_lat=None):
    b, nh, rows, _ = q.shape
    use_lat = k_lat is not None
    tile = min(ATTN_TILE, rows)

    def whole(a):
        return pl.BlockSpec((1,) + a.shape[1:], lambda i, j: (i, 0, 0, 0))

    in_specs = [pl.BlockSpec((1, nh, tile, HEAD_PAD), lambda i, j: (i, 0, j, 0))]
    args = [q]
    if use_lat:
        in_specs += [whole(k_lat), whole(vt_lat)]
        args += [k_lat, vt_lat]
    in_specs += [whole(k_ctx), whole(vt_ctx)]
    args += [k_ctx, vt_ctx]
    return pl.pallas_call(
        functools.partial(_attn_kernel, use_lat=use_lat),
        grid=(b, rows // tile),
        in_specs=in_specs,
        out_specs=pl.BlockSpec((1, tile, MLA_W), lambda i, j: (i, j, 0)),
        out_shape=jax.ShapeDtypeStruct((b, rows, MLA_W), BF16),
        compiler_params=_params(("parallel", "arbitrary")),
        name="mla_attention",
    )(*args)


def _pool_kernel(*refs, lengths):
    n_seg = len(lengths)
    u_refs, (w_ref, s_ref), o_refs, pad_s = refs[:n_seg], refs[n_seg:n_seg + 2], refs[n_seg + 2:-1], refs[-1]
    lo = lax.broadcasted_iota(jnp.int32, (1, LANES), 1) < POOL_GDIM
    halo = POOL_HALO
    for u_ref, o_ref, length in zip(u_refs, o_refs, lengths):
        pad_s[0:halo, :] = jnp.zeros((halo, POOL_W), F32)
        pad_s[halo:halo + length, :] = u_ref[0].astype(F32)
        pad_s[halo + length:2 * halo + length, :] = jnp.zeros((halo, POOL_W), F32)
        for r0 in range(0, length, POOL_TILE):
            t = (lax.broadcasted_iota(jnp.int32, (POOL_TILE, 1), 0) + r0).astype(F32)
            cols = []
            for hf in range(POOL_W // LANES):
                cs = slice(hf * LANES, (hf + 1) * LANES)

                def window(w):
                    base = halo + r0 - w // 2
                    tot = pad_s[base:base + POOL_TILE, cs]
                    for d in range(1, w):
                        tot = tot + pad_s[base + d:base + d + POOL_TILE, cs]
                    cnt = (jnp.clip(t - w // 2 + w, 0.0, float(length))
                           - jnp.clip(t - w // 2, 0.0, float(length)))
                    return tot / cnt

                pooled = jnp.where(lo, window(POOL_WINDOWS[2 * hf]), window(POOL_WINDOWS[2 * hf + 1]))
                cols.append(pooled - pad_s[halo + r0:halo + r0 + POOL_TILE, cs])
            pooled = jnp.concatenate(cols, axis=-1).astype(BF16)
            y = _dot(pooled, w_ref[...]) * s_ref[...]
            o_ref[0, r0:r0 + POOL_TILE, :] = y.astype(o_ref.dtype)


def _pool(us, w_bd, scale):
    b = us[0].shape[0]
    lengths = tuple(u.shape[1] for u in us)
    spec = lambda n: pl.BlockSpec((1, n, POOL_W), lambda i: (i, 0, 0))
    return pl.pallas_call(
        functools.partial(_pool_kernel, lengths=lengths),
        grid=(b,),
        in_specs=[spec(n) for n in lengths] + [_resident(w_bd.shape), _resident(scale.shape)],
        out_specs=[spec(n) for n in lengths],
        out_shape=[jax.ShapeDtypeStruct((b, n, POOL_W), BF16) for n in lengths],
        scratch_shapes=[pltpu.VMEM((max(lengths) + 2 * POOL_HALO, POOL_W), F32)],
        compiler_params=_params(("parallel",)),
        name="pool",
    )(*us, w_bd, scale)


def _mix_mlp_kernel(x_ref, r_ref, m_ref, p_ref, wo_ref, mod_ref, g_ref, w1_ref, w2_ref, gf_ref, o_ref,
                    *, final_norm):
    y = (_dot(r_ref[0], wo_ref[0:RET_W, :])
         + _dot(m_ref[0], wo_ref[RET_W:RET_W + MLA_W, :])
         + _dot(p_ref[0], wo_ref[RET_W + MLA_W:, :]))
    x = x_ref[0] + mod_ref[0, 0:1, :] * y
    h = (_rms(x, g_ref[...]) * (1.0 + mod_ref[0, 2:3, :]) + mod_ref[0, 1:2, :]).astype(BF16)
    acc = None
    for c0 in range(0, w1_ref.shape[1], FF_CHUNK):
        a = jnp.maximum(_dot(h, w1_ref[:, c0:c0 + FF_CHUNK]), 0.0)
        part = _dot((a * a).astype(BF16), w2_ref[c0:c0 + FF_CHUNK, :])
        acc = part if acc is None else acc + part
    y = x + mod_ref[0, 3:4, :] * acc
    if final_norm:
        y = _rms(y, gf_ref[...])
    o_ref[0] = y


def _mix_mlp(x, ret, mla, pool, w_out, mod, g, w1, w2, g_final, tile, final_norm):
    b, r, d = x.shape
    row = lambda width: pl.BlockSpec((1, tile, width), lambda i, j: (i, j, 0))
    return pl.pallas_call(
        functools.partial(_mix_mlp_kernel, final_norm=final_norm),
        grid=(b, r // tile),
        in_specs=[
            row(d), row(RET_W), row(MLA_W), row(POOL_W),
            _resident(w_out.shape),
            pl.BlockSpec((1, 4, d), lambda i, j: (i, 0, 0)),
            _resident(g.shape), _resident(w1.shape), _resident(w2.shape), _resident(g_final.shape),
        ],
        out_specs=row(d),
        out_shape=jax.ShapeDtypeStruct((b, r, d), F32),
        compiler_params=_params(("parallel", "parallel")),
        name="mix_mlp",
    )(x, ret, mla, pool, w_out, mod, g, w1, w2, g_final)


def _in_proj_columns():
    zero = 4 * RET_W + MLA_Q_RANK + MLA_KV_RANK + MLA_ROPE + POOL_W
    src = np.full((IN_COLS,), zero, np.int32)
    sign = np.ones((IN_COLS,), np.float32)
    for p in range(RET_HEADS // 2):
        for part in range(4):
            for hh in range(2):
                h = 2 * p + hh
                dst = COL_RET + p * 4 * LANES + part * LANES + hh * RET_DIM
                src[dst:dst + RET_DIM] = part * RET_W + h * RET_DIM + np.arange(RET_DIM)
    base = 4 * RET_W
    src[COL_CQ:COL_CQ + MLA_Q_RANK] = base + np.arange(MLA_Q_RANK)
    src[COL_CKV:COL_CKV + MLA_KV_RANK] = base + MLA_Q_RANK + np.arange(MLA_KV_RANK)
    kpe0 = base + MLA_Q_RANK + MLA_KV_RANK
    half = MLA_ROPE // 2
    src[COL_KPE + MLA_NOPE:COL_KPE + MLA_NOPE + MLA_ROPE] = kpe0 + np.arange(MLA_ROPE)
    src[COL_KPE_ROT + MLA_NOPE:COL_KPE_ROT + MLA_NOPE + half] = kpe0 + half + np.arange(half)
    sign[COL_KPE_ROT + MLA_NOPE:COL_KPE_ROT + MLA_NOPE + half] = -1.0
    src[COL_KPE_ROT + MLA_NOPE + half:COL_KPE_ROT + MLA_NOPE + MLA_ROPE] = kpe0 + np.arange(half)
    src[COL_U:COL_U + POOL_W] = kpe0 + MLA_ROPE + np.arange(POOL_W)
    return src, sign


def _uq_columns():
    per = MLA_NOPE + MLA_ROPE
    nh = MLA_HEADS
    half = MLA_ROPE // 2
    src = np.full((2 * nh * HEAD_PAD,), nh * per, np.int32)
    sign = np.ones((2 * nh * HEAD_PAD,), np.float32)
    for h in range(nh):
        src[h * HEAD_PAD:h * HEAD_PAD + per] = h * per + np.arange(per)
        r0 = (nh + h) * HEAD_PAD + MLA_NOPE
        src[r0:r0 + half] = h * per + MLA_NOPE + half + np.arange(half)
        sign[r0:r0 + half] = -1.0
        src[r0 + half:r0 + MLA_ROPE] = h * per + MLA_NOPE + np.arange(half)
    return src, sign


def _ukv_columns():
    per = MLA_NOPE + MLA_V
    src = np.full((2 * MLA_HEADS * HEAD_PAD,), MLA_HEADS * per, np.int32)
    for h in range(MLA_HEADS):
        src[h * HEAD_PAD:h * HEAD_PAD + MLA_NOPE] = h * per + np.arange(MLA_NOPE)
        v0 = (MLA_HEADS + h) * HEAD_PAD
        src[v0:v0 + MLA_V] = h * per + MLA_NOPE + np.arange(MLA_V)
    return src, np.ones(src.shape, np.float32)


def _take_cols(w, cols):
    src, sign = cols
    zero, n = w.shape[1], len(src)
    wb = w.astype(BF16)
    pieces, i = [], 0
    while i < n:
        j = i + 1
        if src[i] == zero:
            while j < n and src[j] == zero:
                j += 1
            pieces.append(jnp.zeros((w.shape[0], j - i), BF16))
        else:
            while j < n and src[j] != zero and src[j] == src[j - 1] + 1 and sign[j] == sign[i]:
                j += 1
            piece = wb[:, src[i]:src[i] + j - i]
            pieces.append(-piece if sign[i] < 0 else piece)
        i = j
    return jnp.concatenate(pieces, axis=1)


def _rope_angles(n_rows, dim):
    pos = np.arange(n_rows)
    n_freq = dim // 4
    inv = np.float32(ROPE_BASE) ** (-np.arange(n_freq, dtype=np.float32) / np.float32(n_freq))
    return np.concatenate([(pos // GRID_W).astype(np.float32)[:, None] * inv,
                           (pos % GRID_W).astype(np.float32)[:, None] * inv], axis=-1)


def _ret_rope_tables(n_rows):
    ang = _rope_angles(n_rows, RET_DIM)
    cos, sin = np.cos(ang), np.sin(ang)
    zero = np.zeros_like(sin)
    reps = LANES // RET_DIM
    c = np.tile(np.concatenate([cos, cos], axis=-1), (1, reps))
    sa = np.tile(np.concatenate([-sin, zero], axis=-1), (1, reps))
    sb = np.tile(np.concatenate([zero, sin], axis=-1), (1, reps))
    return tuple(jnp.asarray(a, F32) for a in (c, sa, sb))


def _mla_rope_tables(n_rows):
    ang = _rope_angles(n_rows, MLA_ROPE)
    cos, sin = np.cos(ang), np.sin(ang)
    c = np.ones((n_rows, HEAD_PAD), np.float32)
    s = np.zeros((n_rows, HEAD_PAD), np.float32)
    c[:, MLA_NOPE:MLA_NOPE + MLA_ROPE] = np.concatenate([cos, cos], axis=-1)
    s[:, MLA_NOPE:MLA_NOPE + MLA_ROPE] = np.concatenate([sin, sin], axis=-1)
    return jnp.asarray(c, F32), jnp.asarray(s, F32)


def _block_diag(w_pool):
    g, c, _ = w_pool.shape
    out = jnp.zeros((g * c, g * c), w_pool.dtype)
    for i in range(g):
        out = out.at[i * c:(i + 1) * c, i * c:(i + 1) * c].set(w_pool[i])
    return out


def kernel(x, c, ctx, c_ctx, w_ada, b_ada, norm_mix, w_in, q_norm, w_uq, kv_norm, w_ukv,
           ret_decay_logit, w_pool, pool_scale, w_out, norm_mlp, w_ff1, w_ff2, norm_final):
    b, n_lat, d = x.shape
    n_ctx = ctx.shape[1]
    depth = w_ada.shape[0]
    lat_tile = min(LAT_TILE, n_lat)
    assert n_lat % lat_tile == 0 and n_lat % ATTN_TILE == 0 and n_lat % GRID_W == 0
    assert n_ctx % RET_CHUNK == 0 and n_ctx % POOL_TILE == 0 and n_lat % POOL_TILE == 0
    assert d == 4 * RET_W and w_ff1.shape[2] % FF_CHUNK == 0

    in_cols, uq_cols, ukv_cols = _in_proj_columns(), _uq_columns(), _ukv_columns()
    ret_tabs = _ret_rope_tables(n_lat)
    mla_tabs = _mla_rope_tables(n_lat)

    rows = -(-(b + 1) // 8) * 8
    cc = jnp.zeros((rows, d), F32).at[:b].set(c).at[b].set(c_ctx)
    mod = _ada(cc, w_ada, b_ada).reshape(depth, rows, 6, d)
    mod_lat = mod[:, :b]
    mod_ctx = jnp.broadcast_to(mod[:, b:b + 1], mod_lat.shape)
    lg = jax.nn.log_sigmoid(ret_decay_logit.astype(F32))

    h_ctx = ctx
    for l in range(depth):
        last = l == depth - 1
        w_in_l, w_uq_l, w_ukv_l = _take_cols(w_in[l], in_cols), _take_cols(w_uq[l], uq_cols), _take_cols(w_ukv[l], ukv_cols)
        w_out_l, w1_l, w2_l = w_out[l].astype(BF16), w_ff1[l].astype(BF16), w_ff2[l].astype(BF16)
        norms = (norm_mix[l][None], w_in_l, q_norm[l][None], w_uq_l, kv_norm[l][None], w_ukv_l)

        ret_l, u_l, q_l, k_l, v_l = _inproj(x, mod_lat[l, :, 0:2], *norms, mla_tabs, lat_tile)
        ret_c, u_c, q_c, k_c, v_c = _inproj(h_ctx, mod_ctx[l, :, 0:2], *norms, None, n_ctx)

        ret_o = _retention(lg[l], ret_l, ret_c, ret_tabs, not last)
        mla_l = _attention(q_l, k_c, v_c, k_l, v_l)
        pool_w = (_block_diag(w_pool[l]).astype(BF16), pool_scale[l][None])
        mlp_w = (norm_mlp[l][None], w1_l, w2_l, norm_final[None])
        if last:
            (pool_l,) = _pool((u_l,), *pool_w)
        else:
            pool_l, pool_c = _pool((u_l, u_c), *pool_w)
            mla_c = _attention(q_c, k_c, v_c)
            h_ctx = _mix_mlp(h_ctx, ret_o[1], mla_c, pool_c, w_out_l, mod_ctx[l, :, 2:6], *mlp_w, n_ctx, False)
        x = _mix_mlp(x, ret_o[0], mla_l, pool_l, w_out_l, mod_lat[l, :, 2:6], *mlp_w, lat_tile, last)
    return x
```

```python
import functools
import math

import jax
import jax.numpy as jnp
import numpy as np
from jax import lax
from jax.experimental import pallas as pl
from jax.experimental.pallas import tpu as pltpu

GRID_W = 64
RET_HEADS = 4
RET_DIM = 64
RET_W = RET_HEADS * RET_DIM
RET_CHUNK = 128
MLA_HEADS = 8
MLA_NOPE = 64
MLA_ROPE = 32
MLA_V = 64
MLA_Q_RANK = 256
MLA_KV_RANK = 128
MLA_W = MLA_HEADS * MLA_V
POOL_GROUPS = 4
POOL_WINDOWS = (2, 4, 8, 16)
POOL_GDIM = 64
POOL_W = POOL_GROUPS * POOL_GDIM
ROPE_BASE = 10000.0
EPS = 1e-6

LANES = 128
LAT_TILE = 512
ATTN_TILE = 256
KEY_CHUNK = 256
SCORE_LOOKAHEAD = 5
POOL_TILE = 256
HEAD_PAD = 128
VT_ROWS = MLA_V + 16
POOL_HALO = 16
FF_CHUNK = 1024
VMEM_LIMIT = 56 * 1024 * 1024

COL_RET = 0
COL_CQ = 4 * RET_W
COL_U = COL_CQ + MLA_Q_RANK
COL_CKV = COL_U + POOL_W
COL_KPE = COL_CKV + MLA_KV_RANK
COL_KPE_ROT = COL_KPE + HEAD_PAD
IN_COLS = COL_KPE_ROT + HEAD_PAD

F32 = jnp.float32
BF16 = jnp.bfloat16


def _params(sem):
    return pltpu.CompilerParams(dimension_semantics=sem, vmem_limit_bytes=VMEM_LIMIT)


def _layer_spec(a, layer):
    tail = a.shape[1:]
    return pl.BlockSpec((1,) + tail, lambda *_: (layer,) + (0,) * len(tail), pipeline_mode=pl.Buffered(1))


def _dot(a, b):
    return jnp.dot(a, b, preferred_element_type=F32)


def _dot_nt(a, b):
    return lax.dot_general(a, b, (((1,), (1,)), ((), ())), preferred_element_type=F32)


def _rms(x, g):
    return x * lax.rsqrt(jnp.mean(x * x, axis=-1, keepdims=True) + EPS) * g


def _silu(x):
    return x * (1.0 / (1.0 + jnp.exp(-x)))


def _ada_kernel(c_ref, w_ref, b_ref, o_ref):
    s = _silu(c_ref[...]).astype(BF16)
    o_ref[0] = _dot(s, w_ref[0].astype(BF16)) + b_ref[0]


def _ada(cc, w_ada, b_ada):
    depth, d, n = w_ada.shape
    rows = cc.shape[0]
    tn = n // 4
    return pl.pallas_call(
        _ada_kernel,
        grid=(depth, n // tn),
        in_specs=[
            pl.BlockSpec((rows, d), lambda l, j: (0, 0)),
            pl.BlockSpec((1, d, tn), lambda l, j: (l, 0, j)),
            pl.BlockSpec((1, 1, tn), lambda l, j: (l, 0, j)),
        ],
        out_specs=pl.BlockSpec((1, rows, tn), lambda l, j: (l, 0, j)),
        out_shape=jax.ShapeDtypeStruct((depth, rows, n), F32),
        compiler_params=_params(("parallel", "parallel")),
        name="ada_mod",
    )(cc, w_ada, b_ada.reshape(depth, 1, n))


def _inproj_kernel(*refs, rope):
    if rope:
        (x_ref, mod_ref, g_ref, w_ref, gq_ref, wq_ref, gkv_ref, wkv_ref, c_ref, s_ref,
         ret_ref, u_ref, q_ref, k_ref, vt_ref) = refs
    else:
        (x_ref, mod_ref, g_ref, w_ref, gq_ref, wq_ref, gkv_ref, wkv_ref,
         ret_ref, u_ref, q_ref, k_ref, vt_ref) = refs
    nh = MLA_HEADS
    h = _rms(x_ref[0], g_ref[0]) * (1.0 + mod_ref[0, 0, 1:2, :]) + mod_ref[0, 0, 0:1, :]
    p = _dot(h.astype(BF16), w_ref[0])
    ret_ref[0] = p[:, COL_RET:COL_CQ].astype(ret_ref.dtype)
    u_ref[0] = p[:, COL_U:COL_CKV].astype(u_ref.dtype)

    q_scale = (MLA_NOPE + MLA_ROPE) ** -0.5 * math.log2(math.e)
    yq = _rms(p[:, COL_CQ:COL_U], gq_ref[0]).astype(BF16)
    if rope:
        cos, sin = c_ref[...], s_ref[...]
        q2 = _dot(yq, wq_ref[0])
    else:
        q2 = _dot(yq, wq_ref[0, :, 0:nh * HEAD_PAD])
    for hh in range(nh):
        qh = q2[:, hh * HEAD_PAD:(hh + 1) * HEAD_PAD]
        if rope:
            qh = qh * cos + q2[:, (nh + hh) * HEAD_PAD:(nh + hh + 1) * HEAD_PAD] * sin
        q_ref[0, hh] = (qh * q_scale).astype(q_ref.dtype)

    ykv = _rms(p[:, COL_CKV:COL_KPE], gkv_ref[0]).astype(BF16)
    kv = _dot(ykv, wkv_ref[0])
    kpe = p[:, COL_KPE:COL_KPE_ROT]
    if rope:
        kpe = kpe * cos + p[:, COL_KPE_ROT:IN_COLS] * sin
    lane = lax.broadcasted_iota(jnp.int32, (1, HEAD_PAD), 1)
    ones_col = (lane == MLA_V).astype(F32)
    for hh in range(nh):
        k_ref[0, hh] = (kv[:, hh * HEAD_PAD:(hh + 1) * HEAD_PAD] + kpe).astype(k_ref.dtype)
        v = kv[:, (nh + hh) * HEAD_PAD:(nh + hh + 1) * HEAD_PAD] + ones_col
        vt_ref[0, hh] = v.T.astype(vt_ref.dtype)


def _inproj(x, mod, mod_row, layer, g, w, gq, wq, gkv, wkv, tabs, tile):
    b, r, d = x.shape
    rope = tabs is not None
    nh = MLA_HEADS
    row = lambda width: pl.BlockSpec((1, tile, width), lambda i, j: (i, j, 0))
    head = pl.BlockSpec((1, nh, tile, HEAD_PAD), lambda i, j: (i, 0, j, 0))
    head_t = pl.BlockSpec((1, nh, HEAD_PAD, tile), lambda i, j: (i, 0, 0, j))
    in_specs = [
        row(d),
        pl.BlockSpec((1, 1) + mod.shape[2:], lambda i, j: (layer, mod_row(i), 0, 0)),
    ] + [_layer_spec(a, layer) for a in (g, w, gq, wq, gkv, wkv)]
    args = [x, mod, g, w, gq, wq, gkv, wkv]
    if rope:
        in_specs += [pl.BlockSpec((tile, HEAD_PAD), lambda i, j: (j, 0))] * 2
        args += list(tabs)
    hshape = jax.ShapeDtypeStruct((b, nh, r, HEAD_PAD), BF16)
    return pl.pallas_call(
        functools.partial(_inproj_kernel, rope=rope),
        grid=(b, r // tile),
        in_specs=in_specs,
        out_specs=[row(4 * RET_W), row(POOL_W), head, head, head_t],
        out_shape=[jax.ShapeDtypeStruct((b, r, 4 * RET_W), BF16),
                   jax.ShapeDtypeStruct((b, r, POOL_W), BF16), hshape, hshape,
                   jax.ShapeDtypeStruct((b, nh, HEAD_PAD, r), BF16)],
        compiler_params=_params(("parallel", "parallel")),
        name="in_proj",
    )(*args)


def _ret_kernel(*refs, layer, n_lat, n_ctx, need_ctx):
    if need_ctx:
        (lg_ref, pl_ref, pc_ref, c_ref, sa_ref, sb_ref, ol_ref, oc_ref, q_s, k_s, u_s, st_s) = refs
    else:
        (lg_ref, pl_ref, pc_ref, c_ref, sa_ref, sb_ref, ol_ref, q_s, k_s, u_s, st_s) = refs
        oc_ref = None
    ch = RET_CHUNK
    pair = pl.program_id(1)
    lane = lax.broadcasted_iota(jnp.int32, (1, LANES), 1)
    lo = lane < RET_DIM
    sub_lo = lax.broadcasted_iota(jnp.int32, (LANES, 1), 0) < RET_DIM
    row = lax.broadcasted_iota(jnp.int32, (ch, 1), 0).astype(F32)
    col = lax.broadcasted_iota(jnp.int32, (1, ch), 1).astype(F32)

    lf_a, lf_b = lg_ref[layer, 0, 2 * pair], lg_ref[layer, 0, 2 * pair + 1]
    lb_a, lb_b = lg_ref[layer, 1, 2 * pair], lg_ref[layer, 1, 2 * pair + 1]
    lf = jnp.where(lo, lf_a, lf_b)
    lb = jnp.where(lo, lb_a, lb_b)
    lf_t = jnp.where(sub_lo, lf_a, lf_b)
    lb_t = jnp.where(sub_lo, lb_a, lb_b)
    wqf = jnp.exp(lf * (row + 1.0))
    wqb = jnp.exp(lb * (ch - row))
    wkf_t = jnp.exp(lf_t * (ch - 1.0 - col))
    wkb_t = jnp.exp(lb_t * col)
    gf = jnp.exp(lf * ch)
    gb = jnp.exp(lb * ch)
    diff = row - col

    def decay(l_f, l_b):
        return jnp.where(diff >= 0, jnp.exp(l_f * jnp.maximum(diff, 0.0)),
                         jnp.exp(l_b * jnp.maximum(-diff, 0.0)))

    k_scale = RET_DIM ** -0.5
    d_ab = jnp.concatenate([decay(lf_a, lb_a), decay(lf_b, lb_b)], axis=0) * k_scale
    wq_fb = jnp.concatenate([wqf, wqb], axis=1)
    wk_fb_t = jnp.concatenate([wkf_t, wkb_t], axis=0) * k_scale
    head_of_row = lax.broadcasted_iota(jnp.int32, (LANES, 1), 0) // RET_DIM
    blockdiag = (head_of_row == lane // RET_DIM).astype(F32)
    half = RET_DIM // 2
    n_all = n_lat + n_ctx

    def source(n):
        if n < n_lat:
            return pl_ref, slice(n * ch, (n + 1) * ch), True
        return pc_ref, slice((n - n_lat) * ch, (n - n_lat + 1) * ch), False

    for n in range(n_all):
        src_ref, src_rows, roped = source(n)
        rows = slice(n * ch, (n + 1) * ch)
        q = src_ref[0, src_rows, 0:LANES].astype(F32)
        k = src_ref[0, src_rows, LANES:2 * LANES].astype(F32)
        if roped:
            c, sa, sb = c_ref[src_rows, :], sa_ref[src_rows, :], sb_ref[src_rows, :]
            q = q * c + pltpu.roll(q, LANES - half, 1) * sa + pltpu.roll(q, half, 1) * sb
            k = k * c + pltpu.roll(k, LANES - half, 1) * sa + pltpu.roll(k, half, 1) * sb
        q_s[rows, :] = q
        k_s[rows, :] = k
        kt = k.T
        kt2 = jnp.concatenate([kt, kt], axis=0) * wk_fb_t
        u_s[n] = _dot(kt2.astype(BF16), src_ref[0, src_rows, 2 * LANES:3 * LANES])

    order_f = list(range(n_lat, n_all)) + list(range(n_lat))
    order_b = list(range(n_all - 1, -1, -1))
    s = jnp.zeros((LANES, LANES), F32)
    for n in order_f:
        st_s[n, 0:LANES, :] = (s * blockdiag).astype(BF16)
        s = gf * s + u_s[n, 0:LANES, :]
    s = jnp.zeros((LANES, LANES), F32)
    for n in order_b:
        st_s[n, LANES:2 * LANES, :] = (s * blockdiag).astype(BF16)
        s = gb * s + u_s[n, LANES:2 * LANES, :]

    for n in range(n_all if need_ctx else n_lat):
        src_ref, src_rows, _ = source(n)
        dst_ref = ol_ref if n < n_lat else oc_ref
        rows = slice(n * ch, (n + 1) * ch)
        q = q_s[rows, :]
        kb = k_s[rows, :].astype(BF16)
        vb = src_ref[0, src_rows, 2 * LANES:3 * LANES]
        gate = src_ref[0, src_rows, 3 * LANES:4 * LANES].astype(F32)
        q_ab = jnp.concatenate([jnp.where(lo, q, 0.0), jnp.where(lo, 0.0, q)], axis=0).astype(BF16)
        o_ab = _dot((_dot_nt(q_ab, kb) * d_ab).astype(BF16), vb)
        o = jnp.where(lo, o_ab[0:ch], o_ab[ch:2 * ch])
        o = o + _dot((jnp.concatenate([q, q], axis=1) * wq_fb).astype(BF16), st_s[n])
        inv = 1.0 / RET_DIM
        mu = jnp.where(lo, jnp.sum(jnp.where(lo, o, 0.0), axis=-1, keepdims=True),
                       jnp.sum(jnp.where(lo, 0.0, o), axis=-1, keepdims=True)) * inv
        dlt = o - mu
        sq = dlt * dlt
        var = jnp.where(lo, jnp.sum(jnp.where(lo, sq, 0.0), axis=-1, keepdims=True),
                        jnp.sum(jnp.where(lo, 0.0, sq), axis=-1, keepdims=True)) * inv
        dst_ref[0, src_rows, :] = (_silu(gate) * (dlt * lax.rsqrt(var + EPS))).astype(dst_ref.dtype)


def _retention(lg, layer, ret_lat, ret_ctx, tabs, need_ctx):
    b, rl, _ = ret_lat.shape
    rc = ret_ctx.shape[1]
    ch = RET_CHUNK
    n_lat, n_ctx = rl // ch, rc // ch
    n_all = n_lat + n_ctx
    kern = functools.partial(_ret_kernel, layer=layer, n_lat=n_lat, n_ctx=n_ctx, need_ctx=need_ctx)
    tab_spec = pl.BlockSpec((rl, LANES), lambda i, p: (0, 0), pipeline_mode=pl.Buffered(1))
    out_specs = [pl.BlockSpec((1, rl, LANES), lambda i, p: (i, 0, p))]
    out_shape = [jax.ShapeDtypeStruct((b, rl, RET_W), BF16)]
    if need_ctx:
        out_specs.append(pl.BlockSpec((1, rc, LANES), lambda i, p: (i, 0, p)))
        out_shape.append(jax.ShapeDtypeStruct((b, rc, RET_W), BF16))
    return pl.pallas_call(
        kern,
        grid=(b, RET_HEADS // 2),
        in_specs=[
            pl.BlockSpec(memory_space=pltpu.SMEM),
            pl.BlockSpec((1, rl, 4 * LANES), lambda i, p: (i, 0, p)),
            pl.BlockSpec((1, rc, 4 * LANES), lambda i, p: (i, 0, p)),
            tab_spec, tab_spec, tab_spec,
        ],
        out_specs=out_specs,
        out_shape=out_shape,
        scratch_shapes=[
            pltpu.VMEM((rl + rc, LANES), F32), pltpu.VMEM((rl + rc, LANES), F32),
            pltpu.VMEM((n_all, 2 * LANES, LANES), F32),
            pltpu.VMEM((n_all, 2 * LANES, LANES), BF16),
        ],
        compiler_params=_params(("parallel", "parallel")),
        name="retention",
    )(lg, ret_lat, ret_ctx, *tabs)


def _attn_kernel(*refs, use_lat):
    if use_lat:
        q_ref, kl_ref, vtl_ref, kc_ref, vtc_ref, o_ref = refs
    else:
        q_ref, kc_ref, vtc_ref, o_ref = refs

    sources = [(kc_ref, vtc_ref)]
    if use_lat:
        sources.append((kl_ref, vtl_ref))

    items = [(h, k_ref, vt_ref, c0) for h in range(MLA_HEADS) for k_ref, vt_ref in sources
             for c0 in range(0, k_ref.shape[2], KEY_CHUNK)]
    last_item = {h: i for i, (h, _, _, _) in enumerate(items)}

    scores, m_run, acc, done = {}, {}, {}, {}
    for t in range(len(items) + SCORE_LOOKAHEAD):
        if t < len(items):
            h, k_ref, _, c0 = items[t]
            scores[t] = _dot_nt(k_ref[0, h, c0:c0 + KEY_CHUNK, :], q_ref[0, h])
        i = t - SCORE_LOOKAHEAD
        if i < 0:
            continue
        h, _, vt_ref, c0 = items[i]
        s = scores.pop(i)
        m_new = jnp.max(s, axis=0, keepdims=True)
        if h in m_run:
            m_new = jnp.maximum(m_run[h], m_new)
        pv = _dot(vt_ref[0, h, 0:VT_ROWS, c0:c0 + KEY_CHUNK], jnp.exp2(s - m_new).astype(BF16))
        acc[h] = pv if h not in acc else acc[h] * jnp.exp2(m_run[h] - m_new) + pv
        m_run[h] = m_new
        if i == last_item[h]:
            a = acc.pop(h)
            done[h] = a[0:MLA_V, :] * (1.0 / a[MLA_V:MLA_V + 1, :])
            if h % 2 == 1:
                pair_t = jnp.concatenate([done.pop(h - 1), done.pop(h)], axis=0)
                o_ref[0, :, (h // 2) * HEAD_PAD:(h // 2 + 1) * HEAD_PAD] = pair_t.T.astype(o_ref.dtype)


def _attention(q, k_ctx, vt_ctx, k_lat=None, vt_lat=None):
    b, nh, rows, _ = q.shape
    use_lat = k_lat is not None
    tile = min(ATTN_TILE, rows)

    def whole(a):
        return pl.BlockSpec((1,) + a.shape[1:], lambda i, j: (i, 0, 0, 0))

    in_specs = [pl.BlockSpec((1, nh, tile, HEAD_PAD), lambda i, j: (i, 0, j, 0))]
    args = [q]
    if use_lat:
        in_specs += [whole(k_lat), whole(vt_lat)]
        args += [k_lat, vt_lat]
    in_specs += [whole(k_ctx), whole(vt_ctx)]
    args += [k_ctx, vt_ctx]
    return pl.pallas_call(
        functools.partial(_attn_kernel, use_lat=use_lat),
        grid=(b, rows // tile),
        in_specs=in_specs,
        out_specs=pl.BlockSpec((1, tile, MLA_W), lambda i, j: (i, j, 0)),
        out_shape=jax.ShapeDtypeStruct((b, rows, MLA_W), BF16),
        compiler_params=_params(("parallel", "arbitrary")),
        name="mla_attention",
    )(*args)


def _pool_kernel(*refs, lengths):
    n_seg = len(lengths)
    u_refs, (w_ref, s_ref), o_refs, pad_s = refs[:n_seg], refs[n_seg:n_seg + 2], refs[n_seg + 2:-1], refs[-1]
    lo = lax.broadcasted_iota(jnp.int32, (1, LANES), 1) < POOL_GDIM
    halo = POOL_HALO
    for u_ref, o_ref, length in zip(u_refs, o_refs, lengths):
        pad_s[0:halo, :] = jnp.zeros((halo, POOL_W), F32)
        pad_s[halo:halo + length, :] = u_ref[0].astype(F32)
        pad_s[halo + length:2 * halo + length, :] = jnp.zeros((halo, POOL_W), F32)
        for r0 in range(0, length, POOL_TILE):
            t = (lax.broadcasted_iota(jnp.int32, (POOL_TILE, 1), 0) + r0).astype(F32)
            cols = []
            for hf in range(POOL_W // LANES):
                cs = slice(hf * LANES, (hf + 1) * LANES)

                def rows_sum(first, count):
                    base = halo + r0 + first
                    tot = pad_s[base:base + POOL_TILE, cs]
                    for d in range(1, count):
                        tot = tot + pad_s[base + d:base + d + POOL_TILE, cs]
                    return tot

                def inv_count(w):
                    if r0 - w // 2 >= 0 and r0 + POOL_TILE - w // 2 + w <= length:
                        return 1.0 / w
                    cnt = (jnp.clip(t - w // 2 + w, 0.0, float(length))
                           - jnp.clip(t - w // 2, 0.0, float(length)))
                    return 1.0 / cnt

                w_n, w_w = POOL_WINDOWS[2 * hf], POOL_WINDOWS[2 * hf + 1]
                inner = rows_sum(-(w_n // 2), w_n)
                outer = (inner + rows_sum(-(w_w // 2), (w_w - w_n) // 2)
                         + rows_sum(w_n // 2, (w_w - w_n) // 2))
                pooled = jnp.where(lo, inner * inv_count(w_n), outer * inv_count(w_w))
                cols.append(pooled - pad_s[halo + r0:halo + r0 + POOL_TILE, cs])
            pooled = jnp.concatenate(cols, axis=-1).astype(BF16)
            y = _dot(pooled, w_ref[0]) * s_ref[0]
            o_ref[0, r0:r0 + POOL_TILE, :] = y.astype(o_ref.dtype)


def _pool(us, layer, w_bd, scale):
    b = us[0].shape[0]
    lengths = tuple(u.shape[1] for u in us)
    spec = lambda n: pl.BlockSpec((1, n, POOL_W), lambda i: (i, 0, 0))
    return pl.pallas_call(
        functools.partial(_pool_kernel, lengths=lengths),
        grid=(b,),
        in_specs=[spec(n) for n in lengths] + [_layer_spec(w_bd, layer), _layer_spec(scale, layer)],
        out_specs=[spec(n) for n in lengths],
        out_shape=[jax.ShapeDtypeStruct((b, n, POOL_W), BF16) for n in lengths],
        scratch_shapes=[pltpu.VMEM((max(lengths) + 2 * POOL_HALO, POOL_W), F32)],
        compiler_params=_params(("parallel",)),
        name="pool",
    )(*us, w_bd, scale)


def _mix_mlp_kernel(x_ref, r_ref, m_ref, p_ref, wo_ref, mod_ref, g_ref, w1_ref, w2_ref, gf_ref, o_ref,
                    *, final_norm):
    y = (_dot(r_ref[0], wo_ref[0, 0:RET_W, :])
         + _dot(m_ref[0], wo_ref[0, RET_W:RET_W + MLA_W, :])
         + _dot(p_ref[0], wo_ref[0, RET_W + MLA_W:, :]))
    x = x_ref[0] + mod_ref[0, 0, 2:3, :] * y
    h = (_rms(x, g_ref[0]) * (1.0 + mod_ref[0, 0, 4:5, :]) + mod_ref[0, 0, 3:4, :]).astype(BF16)
    acc = None
    for c0 in range(0, w1_ref.shape[2], FF_CHUNK):
        a = jnp.maximum(_dot(h, w1_ref[0, :, c0:c0 + FF_CHUNK]), 0.0)
        part = _dot((a * a).astype(BF16), w2_ref[0, c0:c0 + FF_CHUNK, :])
        acc = part if acc is None else acc + part
    y = x + mod_ref[0, 0, 5:6, :] * acc
    if final_norm:
        y = _rms(y, gf_ref[...])
    o_ref[0] = y


def _mix_mlp(x, ret, mla, pool, mod, mod_row, layer, w_out, g, w1, w2, g_final, tile, final_norm):
    b, r, d = x.shape
    row = lambda width: pl.BlockSpec((1, tile, width), lambda i, j: (i, j, 0))
    return pl.pallas_call(
        functools.partial(_mix_mlp_kernel, final_norm=final_norm),
        grid=(b, r // tile),
        in_specs=[
            row(d), row(RET_W), row(MLA_W), row(POOL_W),
            _layer_spec(w_out, layer),
            pl.BlockSpec((1, 1) + mod.shape[2:], lambda i, j: (layer, mod_row(i), 0, 0)),
            _layer_spec(g, layer), _layer_spec(w1, layer), _layer_spec(w2, layer),
            pl.BlockSpec(g_final.shape, lambda i, j: (0, 0)),
        ],
        out_specs=row(d),
        out_shape=jax.ShapeDtypeStruct((b, r, d), F32),
        compiler_params=_params(("parallel", "parallel")),
        name="mix_mlp",
    )(x, ret, mla, pool, w_out, mod, g, w1, w2, g_final)


def _in_proj_columns():
    zero = 4 * RET_W + MLA_Q_RANK + MLA_KV_RANK + MLA_ROPE + POOL_W
    src = np.full((IN_COLS,), zero, np.int32)
    sign = np.ones((IN_COLS,), np.float32)
    for p in range(RET_HEADS // 2):
        for part in range(4):
            for hh in range(2):
                h = 2 * p + hh
                dst = COL_RET + p * 4 * LANES + part * LANES + hh * RET_DIM
                src[dst:dst + RET_DIM] = part * RET_W + h * RET_DIM + np.arange(RET_DIM)
    base = 4 * RET_W
    src[COL_CQ:COL_CQ + MLA_Q_RANK] = base + np.arange(MLA_Q_RANK)
    src[COL_CKV:COL_CKV + MLA_KV_RANK] = base + MLA_Q_RANK + np.arange(MLA_KV_RANK)
    kpe0 = base + MLA_Q_RANK + MLA_KV_RANK
    half = MLA_ROPE // 2
    src[COL_KPE + MLA_NOPE:COL_KPE + MLA_NOPE + MLA_ROPE] = kpe0 + np.arange(MLA_ROPE)
    src[COL_KPE_ROT + MLA_NOPE:COL_KPE_ROT + MLA_NOPE + half] = kpe0 + half + np.arange(half)
    sign[COL_KPE_ROT + MLA_NOPE:COL_KPE_ROT + MLA_NOPE + half] = -1.0
    src[COL_KPE_ROT + MLA_NOPE + half:COL_KPE_ROT + MLA_NOPE + MLA_ROPE] = kpe0 + np.arange(half)
    src[COL_U:COL_U + POOL_W] = kpe0 + MLA_ROPE + np.arange(POOL_W)
    return src, sign


def _uq_columns():
    per = MLA_NOPE + MLA_ROPE
    nh = MLA_HEADS
    half = MLA_ROPE // 2
    src = np.full((2 * nh * HEAD_PAD,), nh * per, np.int32)
    sign = np.ones((2 * nh * HEAD_PAD,), np.float32)
    for h in range(nh):
        src[h * HEAD_PAD:h * HEAD_PAD + per] = h * per + np.arange(per)
        r0 = (nh + h) * HEAD_PAD + MLA_NOPE
        src[r0:r0 + half] = h * per + MLA_NOPE + half + np.arange(half)
        sign[r0:r0 + half] = -1.0
        src[r0 + half:r0 + MLA_ROPE] = h * per + MLA_NOPE + np.arange(half)
    return src, sign


def _ukv_columns():
    per = MLA_NOPE + MLA_V
    src = np.full((2 * MLA_HEADS * HEAD_PAD,), MLA_HEADS * per, np.int32)
    for h in range(MLA_HEADS):
        src[h * HEAD_PAD:h * HEAD_PAD + MLA_NOPE] = h * per + np.arange(MLA_NOPE)
        v0 = (MLA_HEADS + h) * HEAD_PAD
        src[v0:v0 + MLA_V] = h * per + MLA_NOPE + np.arange(MLA_V)
    return src, np.ones(src.shape, np.float32)


def _take_cols(w, cols):
    src, sign = cols
    zero, n = w.shape[-1], len(src)
    wb = w.astype(BF16)
    pieces, i = [], 0
    while i < n:
        j = i + 1
        if src[i] == zero:
            while j < n and src[j] == zero:
                j += 1
            pieces.append(jnp.zeros(w.shape[:-1] + (j - i,), BF16))
        else:
            while j < n and src[j] != zero and src[j] == src[j - 1] + 1 and sign[j] == sign[i]:
                j += 1
            piece = wb[..., src[i]:src[i] + j - i]
            pieces.append(-piece if sign[i] < 0 else piece)
        i = j
    return jnp.concatenate(pieces, axis=-1)


def _rope_angles(n_rows, dim):
    pos = np.arange(n_rows)
    n_freq = dim // 4
    inv = np.float32(ROPE_BASE) ** (-np.arange(n_freq, dtype=np.float32) / np.float32(n_freq))
    return np.concatenate([(pos // GRID_W).astype(np.float32)[:, None] * inv,
                           (pos % GRID_W).astype(np.float32)[:, None] * inv], axis=-1)


def _ret_rope_tables(n_rows):
    ang = _rope_angles(n_rows, RET_DIM)
    cos, sin = np.cos(ang), np.sin(ang)
    zero = np.zeros_like(sin)
    reps = LANES // RET_DIM
    c = np.tile(np.concatenate([cos, cos], axis=-1), (1, reps))
    sa = np.tile(np.concatenate([-sin, zero], axis=-1), (1, reps))
    sb = np.tile(np.concatenate([zero, sin], axis=-1), (1, reps))
    return tuple(jnp.asarray(a, F32) for a in (c, sa, sb))


def _mla_rope_tables(n_rows):
    ang = _rope_angles(n_rows, MLA_ROPE)
    cos, sin = np.cos(ang), np.sin(ang)
    c = np.ones((n_rows, HEAD_PAD), np.float32)
    s = np.zeros((n_rows, HEAD_PAD), np.float32)
    c[:, MLA_NOPE:MLA_NOPE + MLA_ROPE] = np.concatenate([cos, cos], axis=-1)
    s[:, MLA_NOPE:MLA_NOPE + MLA_ROPE] = np.concatenate([sin, sin], axis=-1)
    return jnp.asarray(c, F32), jnp.asarray(s, F32)


def _block_diag(w_pool):
    depth, g, c, _ = w_pool.shape
    rows = []
    for i in range(g):
        blocks = [w_pool[:, i] if j == i else jnp.zeros((depth, c, c), w_pool.dtype) for j in range(g)]
        rows.append(jnp.concatenate(blocks, axis=2))
    return jnp.concatenate(rows, axis=1)


def kernel(x, c, ctx, c_ctx, w_ada, b_ada, norm_mix, w_in, q_norm, w_uq, kv_norm, w_ukv,
           ret_decay_logit, w_pool, pool_scale, w_out, norm_mlp, w_ff1, w_ff2, norm_final):
    b, n_lat, d = x.shape
    n_ctx = ctx.shape[1]
    depth = w_ada.shape[0]
    lat_tile = min(LAT_TILE, n_lat)
    assert n_lat % lat_tile == 0 and n_lat % ATTN_TILE == 0 and n_lat % GRID_W == 0
    assert n_ctx % RET_CHUNK == 0 and n_ctx % POOL_TILE == 0 and n_lat % POOL_TILE == 0
    assert d == 4 * RET_W and w_ff1.shape[2] % FF_CHUNK == 0

    in_cols, uq_cols, ukv_cols = _in_proj_columns(), _uq_columns(), _ukv_columns()
    ret_tabs = _ret_rope_tables(n_lat)
    mla_tabs = _mla_rope_tables(n_lat)

    rows = -(-(b + 1) // 8) * 8
    cc = jnp.zeros((rows, d), F32).at[:b].set(c).at[b].set(c_ctx)
    mod = _ada(cc, w_ada, b_ada).reshape(depth, rows, 6, d)
    lat_row, ctx_row = (lambda i: i), (lambda i: b)
    lg = jax.nn.log_sigmoid(ret_decay_logit.astype(F32))

    def rows3(a):
        return a.reshape(depth, 1, a.shape[-1])

    proj_w = (rows3(norm_mix), _take_cols(w_in, in_cols), rows3(q_norm), _take_cols(w_uq, uq_cols),
              rows3(kv_norm), _take_cols(w_ukv, ukv_cols))
    pool_w = (_block_diag(w_pool).astype(BF16), rows3(pool_scale))
    mlp_w = (w_out.astype(BF16), rows3(norm_mlp), w_ff1.astype(BF16), w_ff2.astype(BF16), norm_final[None])

    h_ctx = ctx
    for l in range(depth):
        last = l == depth - 1
        ret_l, u_l, q_l, k_l, vt_l = _inproj(x, mod, lat_row, l, *proj_w, mla_tabs, lat_tile)
        ret_c, u_c, q_c, k_c, vt_c = _inproj(h_ctx, mod, ctx_row, l, *proj_w, None, n_ctx)
        ret_o = _retention(lg, l, ret_l, ret_c, ret_tabs, not last)
        mla_l = _attention(q_l, k_c, vt_c, k_l, vt_l)
        if last:
            (pool_l,) = _pool((u_l,), l, *pool_w)
        else:
            pool_l, pool_c = _pool((u_l, u_c), l, *pool_w)
            mla_c = _attention(q_c, k_c, vt_c)
            h_ctx = _mix_mlp(h_ctx, ret_o[1], mla_c, pool_c, mod, ctx_row, l, *mlp_w, n_ctx, False)
        x = _mix_mlp(x, ret_o[0], mla_l, pool_l, mod, lat_row, l, *mlp_w, lat_tile, last)
    return x
```

```python
import functools
import math

import jax
import jax.numpy as jnp
import numpy as np
from jax import lax
from jax.experimental import pallas as pl
from jax.experimental.pallas import tpu as pltpu

GRID_W = 64
RET_HEADS = 4
RET_DIM = 64
RET_W = RET_HEADS * RET_DIM
RET_CHUNK = 128
MLA_HEADS = 8
MLA_NOPE = 64
MLA_ROPE = 32
MLA_V = 64
MLA_Q_RANK = 256
MLA_KV_RANK = 128
MLA_W = MLA_HEADS * MLA_V
POOL_GROUPS = 4
POOL_WINDOWS = (2, 4, 8, 16)
POOL_GDIM = 64
POOL_W = POOL_GROUPS * POOL_GDIM
ROPE_BASE = 10000.0
EPS = 1e-6

LANES = 128
LAT_TILE = 512
IN_SUB = 256
ATTN_TILE = 512
Q_SUB = 256
KEY_CHUNK = 256
SCORE_LOOKAHEAD = 5
POOL_TILE = 256
HEAD_PAD = 128
VT_ROWS = MLA_V + 16
POOL_HALO = 16
FF_CHUNK = 1024
VMEM_LIMIT = 56 * 1024 * 1024

COL_RET = 0
COL_CQ = 4 * RET_W
COL_U = COL_CQ + MLA_Q_RANK
COL_CKV = COL_U + POOL_W
COL_KPE = COL_CKV + MLA_KV_RANK
COL_KPE_ROT = COL_KPE + HEAD_PAD
IN_COLS = COL_KPE_ROT + HEAD_PAD

F32 = jnp.float32
BF16 = jnp.bfloat16


def _params(sem):
    return pltpu.CompilerParams(dimension_semantics=sem, vmem_limit_bytes=VMEM_LIMIT)


def _layer_spec(a, layer):
    tail = a.shape[1:]
    return pl.BlockSpec((1,) + tail, lambda *_: (layer,) + (0,) * len(tail), pipeline_mode=pl.Buffered(1))


def _dot(a, b):
    return jnp.dot(a, b, preferred_element_type=F32)


def _dot_nt(a, b):
    return lax.dot_general(a, b, (((1,), (1,)), ((), ())), preferred_element_type=F32)


def _rms(x, g):
    return x * lax.rsqrt(jnp.mean(x * x, axis=-1, keepdims=True) + EPS) * g


def _silu(x):
    return x * (1.0 / (1.0 + jnp.exp(-x)))


def _ada_kernel(c_ref, w_ref, b_ref, o_ref):
    s = _silu(c_ref[...]).astype(BF16)
    o_ref[0] = _dot(s, w_ref[0].astype(BF16)) + b_ref[0]


def _ada(cc, w_ada, b_ada):
    depth, d, n = w_ada.shape
    rows = cc.shape[0]
    tn = n // 4
    return pl.pallas_call(
        _ada_kernel,
        grid=(depth, n // tn),
        in_specs=[
            pl.BlockSpec((rows, d), lambda l, j: (0, 0)),
            pl.BlockSpec((1, d, tn), lambda l, j: (l, 0, j)),
            pl.BlockSpec((1, 1, tn), lambda l, j: (l, 0, j)),
        ],
        out_specs=pl.BlockSpec((1, rows, tn), lambda l, j: (l, 0, j)),
        out_shape=jax.ShapeDtypeStruct((depth, rows, n), F32),
        compiler_params=_params(("parallel", "parallel")),
        name="ada_mod",
    )(cc, w_ada, b_ada.reshape(depth, 1, n))


def _inproj_kernel(*refs, rope):
    if rope:
        (x_ref, mod_ref, g_ref, w_ref, gq_ref, wq_ref, gkv_ref, wkv_ref, c_ref, s_ref,
         ret_ref, u_ref, q_ref, k_ref, vt_ref) = refs
    else:
        (x_ref, mod_ref, g_ref, w_ref, gq_ref, wq_ref, gkv_ref, wkv_ref,
         ret_ref, u_ref, q_ref, k_ref, vt_ref) = refs
    nh = MLA_HEADS
    lane = lax.broadcasted_iota(jnp.int32, (1, HEAD_PAD), 1)
    ones_col = (lane == MLA_V).astype(F32)
    q_scale = (MLA_NOPE + MLA_ROPE) ** -0.5 * math.log2(math.e)

    def project(rows):
        h = _rms(x_ref[0, rows, :], g_ref[0]) * (1.0 + mod_ref[0, 0, 1:2, :]) + mod_ref[0, 0, 0:1, :]
        p = _dot(h.astype(BF16), w_ref[0])
        ret_ref[0, rows, :] = p[:, COL_RET:COL_CQ].astype(ret_ref.dtype)
        u_ref[0, rows, :] = p[:, COL_U:COL_CKV].astype(u_ref.dtype)
        return p

    def up_project(p, rows):
        yq = _rms(p[:, COL_CQ:COL_U], gq_ref[0]).astype(BF16)
        if rope:
            cos, sin = c_ref[rows, :], s_ref[rows, :]
            q2 = _dot(yq, wq_ref[0])
        else:
            q2 = _dot(yq, wq_ref[0, :, 0:nh * HEAD_PAD])
        ykv = _rms(p[:, COL_CKV:COL_KPE], gkv_ref[0]).astype(BF16)
        kv = _dot(ykv, wkv_ref[0])
        kpe = p[:, COL_KPE:COL_KPE_ROT]
        if rope:
            kpe = kpe * cos + p[:, COL_KPE_ROT:IN_COLS] * sin
        for hh in range(nh):
            qh = q2[:, hh * HEAD_PAD:(hh + 1) * HEAD_PAD]
            if rope:
                qh = qh * cos + q2[:, (nh + hh) * HEAD_PAD:(nh + hh + 1) * HEAD_PAD] * sin
            q_ref[0, hh, rows, :] = (qh * q_scale).astype(q_ref.dtype)
            k_ref[0, hh, rows, :] = (kv[:, hh * HEAD_PAD:(hh + 1) * HEAD_PAD] + kpe).astype(k_ref.dtype)
            v = kv[:, (nh + hh) * HEAD_PAD:(nh + hh + 1) * HEAD_PAD] + ones_col
            vt_ref[0, hh, :, rows] = v.T.astype(vt_ref.dtype)

    subs = [slice(r0, r0 + IN_SUB) for r0 in range(0, x_ref.shape[1], IN_SUB)]
    p_next = project(subs[0])
    for i, rows in enumerate(subs):
        p = p_next
        if i + 1 < len(subs):
            p_next = project(subs[i + 1])
        up_project(p, rows)


def _inproj(x, mod, mod_row, layer, g, w, gq, wq, gkv, wkv, tabs, tile):
    b, r, d = x.shape
    rope = tabs is not None
    nh = MLA_HEADS
    row = lambda width: pl.BlockSpec((1, tile, width), lambda i, j: (i, j, 0))
    head = pl.BlockSpec((1, nh, tile, HEAD_PAD), lambda i, j: (i, 0, j, 0))
    head_t = pl.BlockSpec((1, nh, HEAD_PAD, tile), lambda i, j: (i, 0, 0, j))
    in_specs = [
        row(d),
        pl.BlockSpec((1, 1) + mod.shape[2:], lambda i, j: (layer, mod_row(i), 0, 0)),
    ] + [_layer_spec(a, layer) for a in (g, w, gq, wq, gkv, wkv)]
    args = [x, mod, g, w, gq, wq, gkv, wkv]
    if rope:
        in_specs += [pl.BlockSpec((tile, HEAD_PAD), lambda i, j: (j, 0))] * 2
        args += list(tabs)
    hshape = jax.ShapeDtypeStruct((b, nh, r, HEAD_PAD), BF16)
    return pl.pallas_call(
        functools.partial(_inproj_kernel, rope=rope),
        grid=(b, r // tile),
        in_specs=in_specs,
        out_specs=[row(4 * RET_W), row(POOL_W), head, head, head_t],
        out_shape=[jax.ShapeDtypeStruct((b, r, 4 * RET_W), BF16),
                   jax.ShapeDtypeStruct((b, r, POOL_W), BF16), hshape, hshape,
                   jax.ShapeDtypeStruct((b, nh, HEAD_PAD, r), BF16)],
        compiler_params=_params(("parallel", "parallel")),
        name="in_proj",
    )(*args)


def _ret_kernel(*refs, layer, n_lat, n_ctx, need_ctx):
    if need_ctx:
        (lg_ref, pl_ref, pc_ref, c_ref, sa_ref, sb_ref, ol_ref, oc_ref, q_s, k_s, u_s, st_s) = refs
    else:
        (lg_ref, pl_ref, pc_ref, c_ref, sa_ref, sb_ref, ol_ref, q_s, k_s, u_s, st_s) = refs
        oc_ref = None
    ch = RET_CHUNK
    pair = pl.program_id(1)
    lane = lax.broadcasted_iota(jnp.int32, (1, LANES), 1)
    lo = lane < RET_DIM
    sub_lo = lax.broadcasted_iota(jnp.int32, (LANES, 1), 0) < RET_DIM
    row = lax.broadcasted_iota(jnp.int32, (ch, 1), 0).astype(F32)
    col = lax.broadcasted_iota(jnp.int32, (1, ch), 1).astype(F32)

    lf_a, lf_b = lg_ref[layer, 0, 2 * pair], lg_ref[layer, 0, 2 * pair + 1]
    lb_a, lb_b = lg_ref[layer, 1, 2 * pair], lg_ref[layer, 1, 2 * pair + 1]
    lf = jnp.where(lo, lf_a, lf_b)
    lb = jnp.where(lo, lb_a, lb_b)
    lf_t = jnp.where(sub_lo, lf_a, lf_b)
    lb_t = jnp.where(sub_lo, lb_a, lb_b)
    wqf = jnp.exp(lf * (row + 1.0))
    wqb = jnp.exp(lb * (ch - row))
    wkf_t = jnp.exp(lf_t * (ch - 1.0 - col))
    wkb_t = jnp.exp(lb_t * col)
    gf = jnp.exp(lf * ch)
    gb = jnp.exp(lb * ch)
    diff = row - col

    def decay(l_f, l_b):
        return jnp.where(diff >= 0, jnp.exp(l_f * jnp.maximum(diff, 0.0)),
                         jnp.exp(l_b * jnp.maximum(-diff, 0.0)))

    k_scale = RET_DIM ** -0.5
    d_ab = jnp.concatenate([decay(lf_a, lb_a), decay(lf_b, lb_b)], axis=0) * k_scale
    wq_fb = jnp.concatenate([wqf, wqb], axis=1)
    wk_fb_t = jnp.concatenate([wkf_t, wkb_t], axis=0) * k_scale
    head_of_row = lax.broadcasted_iota(jnp.int32, (LANES, 1), 0) // RET_DIM
    blockdiag = (head_of_row == lane // RET_DIM).astype(F32)
    half = RET_DIM // 2
    n_all = n_lat + n_ctx

    def source(n):
        if n < n_lat:
            return pl_ref, slice(n * ch, (n + 1) * ch), True
        return pc_ref, slice((n - n_lat) * ch, (n - n_lat + 1) * ch), False

    for n in range(n_all):
        src_ref, src_rows, roped = source(n)
        rows = slice(n * ch, (n + 1) * ch)
        q = src_ref[0, src_rows, 0:LANES].astype(F32)
        k = src_ref[0, src_rows, LANES:2 * LANES].astype(F32)
        if roped:
            c, sa, sb = c_ref[src_rows, :], sa_ref[src_rows, :], sb_ref[src_rows, :]
            q = q * c + pltpu.roll(q, LANES - half, 1) * sa + pltpu.roll(q, half, 1) * sb
            k = k * c + pltpu.roll(k, LANES - half, 1) * sa + pltpu.roll(k, half, 1) * sb
        q_s[rows, :] = q
        k_s[rows, :] = k
        kt = k.T
        kt2 = jnp.concatenate([kt, kt], axis=0) * wk_fb_t
        u_s[n] = _dot(kt2.astype(BF16), src_ref[0, src_rows, 2 * LANES:3 * LANES])

    order_f = list(range(n_lat, n_all)) + list(range(n_lat))
    order_b = list(range(n_all - 1, -1, -1))
    s = jnp.zeros((LANES, LANES), F32)
    for n in order_f:
        st_s[n, 0:LANES, :] = (s * blockdiag).astype(BF16)
        s = gf * s + u_s[n, 0:LANES, :]
    s = jnp.zeros((LANES, LANES), F32)
    for n in order_b:
        st_s[n, LANES:2 * LANES, :] = (s * blockdiag).astype(BF16)
        s = gb * s + u_s[n, LANES:2 * LANES, :]

    for n in range(n_all if need_ctx else n_lat):
        src_ref, src_rows, _ = source(n)
        dst_ref = ol_ref if n < n_lat else oc_ref
        rows = slice(n * ch, (n + 1) * ch)
        q = q_s[rows, :]
        kb = k_s[rows, :].astype(BF16)
        vb = src_ref[0, src_rows, 2 * LANES:3 * LANES]
        gate = src_ref[0, src_rows, 3 * LANES:4 * LANES].astype(F32)
        q_ab = jnp.concatenate([jnp.where(lo, q, 0.0), jnp.where(lo, 0.0, q)], axis=0).astype(BF16)
        o_ab = _dot((_dot_nt(q_ab, kb) * d_ab).astype(BF16), vb)
        o = jnp.where(lo, o_ab[0:ch], o_ab[ch:2 * ch])
        o = o + _dot((jnp.concatenate([q, q], axis=1) * wq_fb).astype(BF16), st_s[n])
        inv = 1.0 / RET_DIM
        mu = jnp.where(lo, jnp.sum(jnp.where(lo, o, 0.0), axis=-1, keepdims=True),
                       jnp.sum(jnp.where(lo, 0.0, o), axis=-1, keepdims=True)) * inv
        dlt = o - mu
        sq = dlt * dlt
        var = jnp.where(lo, jnp.sum(jnp.where(lo, sq, 0.0), axis=-1, keepdims=True),
                        jnp.sum(jnp.where(lo, 0.0, sq), axis=-1, keepdims=True)) * inv
        dst_ref[0, src_rows, :] = (_silu(gate) * (dlt * lax.rsqrt(var + EPS))).astype(dst_ref.dtype)


def _retention(lg, layer, ret_lat, ret_ctx, tabs, need_ctx):
    b, rl, _ = ret_lat.shape
    rc = ret_ctx.shape[1]
    ch = RET_CHUNK
    n_lat, n_ctx = rl // ch, rc // ch
    n_all = n_lat + n_ctx
    kern = functools.partial(_ret_kernel, layer=layer, n_lat=n_lat, n_ctx=n_ctx, need_ctx=need_ctx)
    tab_spec = pl.BlockSpec((rl, LANES), lambda i, p: (0, 0), pipeline_mode=pl.Buffered(1))
    out_specs = [pl.BlockSpec((1, rl, LANES), lambda i, p: (i, 0, p))]
    out_shape = [jax.ShapeDtypeStruct((b, rl, RET_W), BF16)]
    if need_ctx:
        out_specs.append(pl.BlockSpec((1, rc, LANES), lambda i, p: (i, 0, p)))
        out_shape.append(jax.ShapeDtypeStruct((b, rc, RET_W), BF16))
    return pl.pallas_call(
        kern,
        grid=(b, RET_HEADS // 2),
        in_specs=[
            pl.BlockSpec(memory_space=pltpu.SMEM),
            pl.BlockSpec((1, rl, 4 * LANES), lambda i, p: (i, 0, p)),
            pl.BlockSpec((1, rc, 4 * LANES), lambda i, p: (i, 0, p)),
            tab_spec, tab_spec, tab_spec,
        ],
        out_specs=out_specs,
        out_shape=out_shape,
        scratch_shapes=[
            pltpu.VMEM((rl + rc, LANES), F32), pltpu.VMEM((rl + rc, LANES), F32),
            pltpu.VMEM((n_all, 2 * LANES, LANES), F32),
            pltpu.VMEM((n_all, 2 * LANES, LANES), BF16),
        ],
        compiler_params=_params(("parallel", "parallel")),
        name="retention",
    )(lg, ret_lat, ret_ctx, *tabs)


def _attn_kernel(*refs, use_lat):
    if use_lat:
        q_ref, kl_ref, vtl_ref, kc_ref, vtc_ref, o_ref = refs
    else:
        q_ref, kc_ref, vtc_ref, o_ref = refs

    sources = [(kc_ref, vtc_ref)]
    if use_lat:
        sources.append((kl_ref, vtl_ref))

    n_sub = q_ref.shape[2] // Q_SUB
    items = [((qs, h), k_ref, vt_ref, c0) for qs in range(n_sub) for h in range(MLA_HEADS)
             for k_ref, vt_ref in sources for c0 in range(0, k_ref.shape[2], KEY_CHUNK)]
    last_item = {g: i for i, (g, _, _, _) in enumerate(items)}

    scores, m_run, acc, done = {}, {}, {}, {}
    for t in range(len(items) + SCORE_LOOKAHEAD):
        if t < len(items):
            (qs, h), k_ref, _, c0 = items[t]
            q = q_ref[0, h, qs * Q_SUB:(qs + 1) * Q_SUB, :]
            scores[t] = _dot_nt(k_ref[0, h, c0:c0 + KEY_CHUNK, :], q)
        i = t - SCORE_LOOKAHEAD
        if i < 0:
            continue
        g, _, vt_ref, c0 = items[i]
        qs, h = g
        s = scores.pop(i)
        m_new = jnp.max(s, axis=0, keepdims=True)
        if g in m_run:
            m_new = jnp.maximum(m_run[g], m_new)
        pv = _dot(vt_ref[0, h, 0:VT_ROWS, c0:c0 + KEY_CHUNK], jnp.exp2(s - m_new).astype(BF16))
        acc[g] = pv if g not in acc else acc[g] * jnp.exp2(m_run[g] - m_new) + pv
        m_run[g] = m_new
        if i == last_item[g]:
            a = acc.pop(g)
            done[g] = a[0:MLA_V, :] * (1.0 / a[MLA_V:MLA_V + 1, :])
            if h % 2 == 1:
                pair_t = jnp.concatenate([done.pop((qs, h - 1)), done.pop(g)], axis=0)
                o_ref[0, qs * Q_SUB:(qs + 1) * Q_SUB, (h // 2) * HEAD_PAD:(h // 2 + 1) * HEAD_PAD] = (
                    pair_t.T.astype(o_ref.dtype))


def _attention(q, k_ctx, vt_ctx, k_lat=None, vt_lat=None):
    b, nh, rows, _ = q.shape
    use_lat = k_lat is not None
    tile = min(ATTN_TILE, rows)

    def whole(a):
        return pl.BlockSpec((1,) + a.shape[1:], lambda i, j: (i, 0, 0, 0))

    in_specs = [pl.BlockSpec((1, nh, tile, HEAD_PAD), lambda i, j: (i, 0, j, 0))]
    args = [q]
    if use_lat:
        in_specs += [whole(k_lat), whole(vt_lat)]
        args += [k_lat, vt_lat]
    in_specs += [whole(k_ctx), whole(vt_ctx)]
    args += [k_ctx, vt_ctx]
    return pl.pallas_call(
        functools.partial(_attn_kernel, use_lat=use_lat),
        grid=(b, rows // tile),
        in_specs=in_specs,
        out_specs=pl.BlockSpec((1, tile, MLA_W), lambda i, j: (i, j, 0)),
        out_shape=jax.ShapeDtypeStruct((b, rows, MLA_W), BF16),
        compiler_params=_params(("parallel", "arbitrary")),
        name="mla_attention",
    )(*args)


def _pool_kernel(*refs, lengths):
    n_seg = len(lengths)
    u_refs, (w_ref, s_ref), o_refs, pad_s = refs[:n_seg], refs[n_seg:n_seg + 2], refs[n_seg + 2:-1], refs[-1]
    lo = lax.broadcasted_iota(jnp.int32, (1, LANES), 1) < POOL_GDIM
    halo = POOL_HALO
    for u_ref, o_ref, length in zip(u_refs, o_refs, lengths):
        pad_s[0:halo, :] = jnp.zeros((halo, POOL_W), F32)
        pad_s[halo:halo + length, :] = u_ref[0].astype(F32)
        pad_s[halo + length:2 * halo + length, :] = jnp.zeros((halo, POOL_W), F32)
        for r0 in range(0, length, POOL_TILE):
            t = (lax.broadcasted_iota(jnp.int32, (POOL_TILE, 1), 0) + r0).astype(F32)
            cols = []
            for hf in range(POOL_W // LANES):
                cs = slice(hf * LANES, (hf + 1) * LANES)

                def rows_sum(first, count):
                    base = halo + r0 + first
                    tot = pad_s[base:base + POOL_TILE, cs]
                    for d in range(1, count):
                        tot = tot + pad_s[base + d:base + d + POOL_TILE, cs]
                    return tot

                def inv_count(w):
                    if r0 - w // 2 >= 0 and r0 + POOL_TILE - w // 2 + w <= length:
                        return 1.0 / w
                    cnt = (jnp.clip(t - w // 2 + w, 0.0, float(length))
                           - jnp.clip(t - w // 2, 0.0, float(length)))
                    return 1.0 / cnt

                w_n, w_w = POOL_WINDOWS[2 * hf], POOL_WINDOWS[2 * hf + 1]
                inner = rows_sum(-(w_n // 2), w_n)
                outer = (inner + rows_sum(-(w_w // 2), (w_w - w_n) // 2)
                         + rows_sum(w_n // 2, (w_w - w_n) // 2))
                pooled = jnp.where(lo, inner * inv_count(w_n), outer * inv_count(w_w))
                cols.append(pooled - pad_s[halo + r0:halo + r0 + POOL_TILE, cs])
            pooled = jnp.concatenate(cols, axis=-1).astype(BF16)
            y = _dot(pooled, w_ref[0]) * s_ref[0]
            o_ref[0, r0:r0 + POOL_TILE, :] = y.astype(o_ref.dtype)


def _pool(us, layer, w_bd, scale):
    b = us[0].shape[0]
    lengths = tuple(u.shape[1] for u in us)
    spec = lambda n: pl.BlockSpec((1, n, POOL_W), lambda i: (i, 0, 0))
    return pl.pallas_call(
        functools.partial(_pool_kernel, lengths=lengths),
        grid=(b,),
        in_specs=[spec(n) for n in lengths] + [_layer_spec(w_bd, layer), _layer_spec(scale, layer)],
        out_specs=[spec(n) for n in lengths],
        out_shape=[jax.ShapeDtypeStruct((b, n, POOL_W), BF16) for n in lengths],
        scratch_shapes=[pltpu.VMEM((max(lengths) + 2 * POOL_HALO, POOL_W), F32)],
        compiler_params=_params(("parallel",)),
        name="pool",
    )(*us, w_bd, scale)


def _mix_mlp_kernel(x_ref, r_ref, m_ref, p_ref, wo_ref, mod_ref, g_ref, w1_ref, w2_ref, gf_ref, o_ref,
                    *, final_norm):
    y = (_dot(r_ref[0], wo_ref[0, 0:RET_W, :])
         + _dot(m_ref[0], wo_ref[0, RET_W:RET_W + MLA_W, :])
         + _dot(p_ref[0], wo_ref[0, RET_W + MLA_W:, :]))
    x = x_ref[0] + mod_ref[0, 0, 2:3, :] * y
    h = (_rms(x, g_ref[0]) * (1.0 + mod_ref[0, 0, 4:5, :]) + mod_ref[0, 0, 3:4, :]).astype(BF16)
    acc = None
    for c0 in range(0, w1_ref.shape[2], FF_CHUNK):
        a = jnp.maximum(_dot(h, w1_ref[0, :, c0:c0 + FF_CHUNK]), 0.0)
        part = _dot((a * a).astype(BF16), w2_ref[0, c0:c0 + FF_CHUNK, :])
        acc = part if acc is None else acc + part
    y = x + mod_ref[0, 0, 5:6, :] * acc
    if final_norm:
        y = _rms(y, gf_ref[...])
    o_ref[0] = y


def _mix_mlp(x, ret, mla, pool, mod, mod_row, layer, w_out, g, w1, w2, g_final, tile, final_norm):
    b, r, d = x.shape
    row = lambda width: pl.BlockSpec((1, tile, width), lambda i, j: (i, j, 0))
    return pl.pallas_call(
        functools.partial(_mix_mlp_kernel, final_norm=final_norm),
        grid=(b, r // tile),
        in_specs=[
            row(d), row(RET_W), row(MLA_W), row(POOL_W),
            _layer_spec(w_out, layer),
            pl.BlockSpec((1, 1) + mod.shape[2:], lambda i, j: (layer, mod_row(i), 0, 0)),
            _layer_spec(g, layer), _layer_spec(w1, layer), _layer_spec(w2, layer),
            pl.BlockSpec(g_final.shape, lambda i, j: (0, 0)),
        ],
        out_specs=row(d),
        out_shape=jax.ShapeDtypeStruct((b, r, d), F32),
        compiler_params=_params(("parallel", "parallel")),
        name="mix_mlp",
    )(x, ret, mla, pool, w_out, mod, g, w1, w2, g_final)


def _in_proj_columns():
    zero = 4 * RET_W + MLA_Q_RANK + MLA_KV_RANK + MLA_ROPE + POOL_W
    src = np.full((IN_COLS,), zero, np.int32)
    sign = np.ones((IN_COLS,), np.float32)
    for p in range(RET_HEADS // 2):
        for part in range(4):
            for hh in range(2):
                h = 2 * p + hh
                dst = COL_RET + p * 4 * LANES + part * LANES + hh * RET_DIM
                src[dst:dst + RET_DIM] = part * RET_W + h * RET_DIM + np.arange(RET_DIM)
    base = 4 * RET_W
    src[COL_CQ:COL_CQ + MLA_Q_RANK] = base + np.arange(MLA_Q_RANK)
    src[COL_CKV:COL_CKV + MLA_KV_RANK] = base + MLA_Q_RANK + np.arange(MLA_KV_RANK)
    kpe0 = base + MLA_Q_RANK + MLA_KV_RANK
    half = MLA_ROPE // 2
    src[COL_KPE + MLA_NOPE:COL_KPE + MLA_NOPE + MLA_ROPE] = kpe0 + np.arange(MLA_ROPE)
    src[COL_KPE_ROT + MLA_NOPE:COL_KPE_ROT + MLA_NOPE + half] = kpe0 + half + np.arange(half)
    sign[COL_KPE_ROT + MLA_NOPE:COL_KPE_ROT + MLA_NOPE + half] = -1.0
    src[COL_KPE_ROT + MLA_NOPE + half:COL_KPE_ROT + MLA_NOPE + MLA_ROPE] = kpe0 + np.arange(half)
    src[COL_U:COL_U + POOL_W] = kpe0 + MLA_ROPE + np.arange(POOL_W)
    return src, sign


def _uq_columns():
    per = MLA_NOPE + MLA_ROPE
    nh = MLA_HEADS
    half = MLA_ROPE // 2
    src = np.full((2 * nh * HEAD_PAD,), nh * per, np.int32)
    sign = np.ones((2 * nh * HEAD_PAD,), np.float32)
    for h in range(nh):
        src[h * HEAD_PAD:h * HEAD_PAD + per] = h * per + np.arange(per)
        r0 = (nh + h) * HEAD_PAD + MLA_NOPE
        src[r0:r0 + half] = h * per + MLA_NOPE + half + np.arange(half)
        sign[r0:r0 + half] = -1.0
        src[r0 + half:r0 + MLA_ROPE] = h * per + MLA_NOPE + np.arange(half)
    return src, sign


def _ukv_columns():
    per = MLA_NOPE + MLA_V
    src = np.full((2 * MLA_HEADS * HEAD_PAD,), MLA_HEADS * per, np.int32)
    for h in range(MLA_HEADS):
        src[h * HEAD_PAD:h * HEAD_PAD + MLA_NOPE] = h * per + np.arange(MLA_NOPE)
        v0 = (MLA_HEADS + h) * HEAD_PAD
        src[v0:v0 + MLA_V] = h * per + MLA_NOPE + np.arange(MLA_V)
    return src, np.ones(src.shape, np.float32)


def _take_cols(w, cols):
    src, sign = cols
    zero, n = w.shape[-1], len(src)
    wb = w.astype(BF16)
    pieces, i = [], 0
    while i < n:
        j = i + 1
        if src[i] == zero:
            while j < n and src[j] == zero:
                j += 1
            pieces.append(jnp.zeros(w.shape[:-1] + (j - i,), BF16))
        else:
            while j < n and src[j] != zero and src[j] == src[j - 1] + 1 and sign[j] == sign[i]:
                j += 1
            piece = wb[..., src[i]:src[i] + j - i]
            pieces.append(-piece if sign[i] < 0 else piece)
        i = j
    return jnp.concatenate(pieces, axis=-1)


def _rope_angles(n_rows, dim):
    pos = np.arange(n_rows)
    n_freq = dim // 4
    inv = np.float32(ROPE_BASE) ** (-np.arange(n_freq, dtype=np.float32) / np.float32(n_freq))
    return np.concatenate([(pos // GRID_W).astype(np.float32)[:, None] * inv,
                           (pos % GRID_W).astype(np.float32)[:, None] * inv], axis=-1)


def _ret_rope_tables(n_rows):
    ang = _rope_angles(n_rows, RET_DIM)
    cos, sin = np.cos(ang), np.sin(ang)
    zero = np.zeros_like(sin)
    reps = LANES // RET_DIM
    c = np.tile(np.concatenate([cos, cos], axis=-1), (1, reps))
    sa = np.tile(np.concatenate([-sin, zero], axis=-1), (1, reps))
    sb = np.tile(np.concatenate([zero, sin], axis=-1), (1, reps))
    return tuple(jnp.asarray(a, F32) for a in (c, sa, sb))


def _mla_rope_tables(n_rows):
    ang = _rope_angles(n_rows, MLA_ROPE)
    cos, sin = np.cos(ang), np.sin(ang)
    c = np.ones((n_rows, HEAD_PAD), np.float32)
    s = np.zeros((n_rows, HEAD_PAD), np.float32)
    c[:, MLA_NOPE:MLA_NOPE + MLA_ROPE] = np.concatenate([cos, cos], axis=-1)
    s[:, MLA_NOPE:MLA_NOPE + MLA_ROPE] = np.concatenate([sin, sin], axis=-1)
    return jnp.asarray(c, F32), jnp.asarray(s, F32)


def _block_diag(w_pool):
    depth, g, c, _ = w_pool.shape
    rows = []
    for i in range(g):
        blocks = [w_pool[:, i] if j == i else jnp.zeros((depth, c, c), w_pool.dtype) for j in range(g)]
        rows.append(jnp.concatenate(blocks, axis=2))
    return jnp.concatenate(rows, axis=1)


def kernel(x, c, ctx, c_ctx, w_ada, b_ada, norm_mix, w_in, q_norm, w_uq, kv_norm, w_ukv,
           ret_decay_logit, w_pool, pool_scale, w_out, norm_mlp, w_ff1, w_ff2, norm_final):
    b, n_lat, d = x.shape
    n_ctx = ctx.shape[1]
    depth = w_ada.shape[0]
    lat_tile = min(LAT_TILE, n_lat)
    assert n_lat % lat_tile == 0 and n_lat % ATTN_TILE == 0 and n_lat % GRID_W == 0
    assert n_ctx % RET_CHUNK == 0 and n_ctx % POOL_TILE == 0 and n_lat % POOL_TILE == 0
    assert d == 4 * RET_W and w_ff1.shape[2] % FF_CHUNK == 0

    in_cols, uq_cols, ukv_cols = _in_proj_columns(), _uq_columns(), _ukv_columns()
    ret_tabs = _ret_rope_tables(n_lat)
    mla_tabs = _mla_rope_tables(n_lat)

    rows = -(-(b + 1) // 8) * 8
    cc = jnp.zeros((rows, d), F32).at[:b].set(c).at[b].set(c_ctx)
    mod = _ada(cc, w_ada, b_ada).reshape(depth, rows, 6, d)
    lat_row, ctx_row = (lambda i: i), (lambda i: b)
    lg = jax.nn.log_sigmoid(ret_decay_logit.astype(F32))

    def rows3(a):
        return a.reshape(depth, 1, a.shape[-1])

    proj_w = (rows3(norm_mix), _take_cols(w_in, in_cols), rows3(q_norm), _take_cols(w_uq, uq_cols),
              rows3(kv_norm), _take_cols(w_ukv, ukv_cols))
    pool_w = (_block_diag(w_pool).astype(BF16), rows3(pool_scale))
    mlp_w = (w_out.astype(BF16), rows3(norm_mlp), w_ff1.astype(BF16), w_ff2.astype(BF16), norm_final[None])

    h_ctx = ctx
    for l in range(depth):
        last = l == depth - 1
        ret_l, u_l, q_l, k_l, vt_l = _inproj(x, mod, lat_row, l, *proj_w, mla_tabs, lat_tile)
        ret_c, u_c, q_c, k_c, vt_c = _inproj(h_ctx, mod, ctx_row, l, *proj_w, None, n_ctx)
        ret_o = _retention(lg, l, ret_l, ret_c, ret_tabs, not last)
        mla_l = _attention(q_l, k_c, vt_c, k_l, vt_l)
        if last:
            (pool_l,) = _pool((u_l,), l, *pool_w)
        else:
            pool_l, pool_c = _pool((u_l, u_c), l, *pool_w)
            mla_c = _attention(q_c, k_c, vt_c)
            h_ctx = _mix_mlp(h_ctx, ret_o[1], mla_c, pool_c, mod, ctx_row, l, *mlp_w, n_ctx, False)
        x = _mix_mlp(x, ret_o[0], mla_l, pool_l, mod, lat_row, l, *mlp_w, lat_tile, last)
    return x
```

```python
import functools
import math

import jax
import jax.numpy as jnp
import numpy as np
from jax import lax
from jax.experimental import pallas as pl
from jax.experimental.pallas import tpu as pltpu

GRID_W = 64
RET_HEADS = 4
RET_DIM = 64
RET_W = RET_HEADS * RET_DIM
RET_CHUNK = 128
MLA_HEADS = 8
MLA_NOPE = 64
MLA_ROPE = 32
MLA_V = 64
MLA_Q_RANK = 256
MLA_KV_RANK = 128
MLA_W = MLA_HEADS * MLA_V
POOL_GROUPS = 4
POOL_WINDOWS = (2, 4, 8, 16)
POOL_GDIM = 64
POOL_W = POOL_GROUPS * POOL_GDIM
ROPE_BASE = 10000.0
EPS = 1e-6

LANES = 128
LAT_TILE = 512
IN_SUB = 256
ATTN_TILE = 512
Q_SUB = 256
KEY_CHUNK = 256
SCORE_LOOKAHEAD = 5
POOL_TILE = 256
HEAD_PAD = 128
VT_ROWS = MLA_V + 16
POOL_HALO = 16
FF_CHUNK = 1024
VMEM_LIMIT = 56 * 1024 * 1024

COL_RET = 0
COL_CQ = 4 * RET_W
COL_CKV = COL_CQ + MLA_Q_RANK
COL_U = COL_CKV + MLA_KV_RANK
COL_KPE = COL_U + POOL_W
IN_COLS = COL_KPE + HEAD_PAD
UQ_ROT = MLA_HEADS * HEAD_PAD
UQ_COLS = UQ_ROT + MLA_HEADS * MLA_ROPE
UKV_V = MLA_HEADS * MLA_NOPE
UKV_COLS = UKV_V + MLA_HEADS * MLA_V

F32 = jnp.float32
BF16 = jnp.bfloat16


def _params(sem):
    return pltpu.CompilerParams(dimension_semantics=sem, vmem_limit_bytes=VMEM_LIMIT)


def _layer_spec(a, layer):
    tail = a.shape[1:]
    return pl.BlockSpec((1,) + tail, lambda *_: (layer,) + (0,) * len(tail), pipeline_mode=pl.Buffered(1))


def _dot(a, b):
    return jnp.dot(a, b, preferred_element_type=F32)


def _dot_nt(a, b):
    return lax.dot_general(a, b, (((1,), (1,)), ((), ())), preferred_element_type=F32)


def _rms(x, g):
    return x * lax.rsqrt(jnp.mean(x * x, axis=-1, keepdims=True) + EPS) * g


def _silu(x):
    return x * (1.0 / (1.0 + jnp.exp(-x)))


def _ada_kernel(c_ref, w_ref, b_ref, o_ref):
    s = _silu(c_ref[...]).astype(BF16)
    o_ref[0] = _dot(s, w_ref[0].astype(BF16)) + b_ref[0]


def _ada(cc, w_ada, b_ada):
    depth, d, n = w_ada.shape
    rows = cc.shape[0]
    tn = n // 4
    return pl.pallas_call(
        _ada_kernel,
        grid=(depth, n // tn),
        in_specs=[
            pl.BlockSpec((rows, d), lambda l, j: (0, 0)),
            pl.BlockSpec((1, d, tn), lambda l, j: (l, 0, j)),
            pl.BlockSpec((1, 1, tn), lambda l, j: (l, 0, j)),
        ],
        out_specs=pl.BlockSpec((1, rows, tn), lambda l, j: (l, 0, j)),
        out_shape=jax.ShapeDtypeStruct((depth, rows, n), F32),
        compiler_params=_params(("parallel", "parallel")),
        name="ada_mod",
    )(cc, w_ada, b_ada.reshape(depth, 1, n))


def _inproj_kernel(*refs, rope):
    if rope:
        (x_ref, mod_ref, g_ref, w_ref, gq_ref, wq_ref, gkv_ref, wkv_ref, c_ref, s_ref,
         ret_ref, u_ref, q_ref, k_ref, vt_ref) = refs
    else:
        (x_ref, mod_ref, g_ref, w_ref, gq_ref, wq_ref, gkv_ref, wkv_ref,
         ret_ref, u_ref, q_ref, k_ref, vt_ref) = refs
    nh = MLA_HEADS
    lane = lax.broadcasted_iota(jnp.int32, (1, HEAD_PAD), 1)
    nope_lanes = lane < MLA_NOPE
    rope_lanes = jnp.logical_and(lane >= MLA_NOPE, lane < MLA_NOPE + MLA_ROPE)
    sub = min(IN_SUB, x_ref.shape[1])
    den_rows = (lax.broadcasted_iota(jnp.int32, (VT_ROWS - MLA_V, sub), 0) == 0).astype(vt_ref.dtype)
    q_scale = (MLA_NOPE + MLA_ROPE) ** -0.5 * math.log2(math.e)

    def project(rows):
        h = _rms(x_ref[0, rows, :], g_ref[0]) * (1.0 + mod_ref[0, 0, 1:2, :]) + mod_ref[0, 0, 0:1, :]
        p = _dot(h.astype(BF16), w_ref[0])
        ret_ref[0, rows, :] = p[:, COL_RET:COL_CQ].astype(ret_ref.dtype)
        u_ref[0, rows, :] = p[:, COL_U:COL_KPE].astype(u_ref.dtype)
        return p

    def up_project(p, rows):
        yq = _rms(p[:, COL_CQ:COL_CKV], gq_ref[0]).astype(BF16)
        if rope:
            cos, sin = c_ref[rows, :], s_ref[rows, :]
            q2 = _dot(yq, wq_ref[0])
        else:
            q2 = _dot(yq, wq_ref[0, :, 0:UQ_ROT])
        ykv = _rms(p[:, COL_CKV:COL_U], gkv_ref[0]).astype(BF16)
        kv = _dot(ykv, wkv_ref[0])
        kx = p[:, COL_KPE:IN_COLS]
        kpe = pltpu.roll(kx, MLA_NOPE, 1)
        if rope:
            kpe = kpe * cos + pltpu.roll(kx, MLA_NOPE - MLA_ROPE, 1) * sin
        kpe = jnp.where(rope_lanes, kpe, 0.0)
        for hh in range(nh):
            qh = q2[:, hh * HEAD_PAD:(hh + 1) * HEAD_PAD]
            if rope:
                blk, j = divmod(hh * MLA_ROPE, HEAD_PAD)
                rot = q2[:, UQ_ROT + blk * HEAD_PAD:UQ_ROT + (blk + 1) * HEAD_PAD]
                shift = (MLA_NOPE - j) % HEAD_PAD
                qh = qh * cos + (pltpu.roll(rot, shift, 1) if shift else rot) * sin
            q_ref[0, hh, rows, :] = (qh * q_scale).astype(q_ref.dtype)
            blk, j = divmod(hh * MLA_NOPE, HEAD_PAD)
            kn = kv[:, blk * HEAD_PAD:(blk + 1) * HEAD_PAD]
            if j:
                kn = pltpu.roll(kn, HEAD_PAD - j, 1)
            k_ref[0, hh, rows, :] = jnp.where(nope_lanes, kn, kpe).astype(k_ref.dtype)
        for blk in range(nh * MLA_V // HEAD_PAD):
            vt = kv[:, UKV_V + blk * HEAD_PAD:UKV_V + (blk + 1) * HEAD_PAD].T.astype(vt_ref.dtype)
            for j in range(HEAD_PAD // MLA_V):
                hh = blk * (HEAD_PAD // MLA_V) + j
                vt_ref[0, hh, 0:MLA_V, rows] = vt[j * MLA_V:(j + 1) * MLA_V, :]
                vt_ref[0, hh, MLA_V:VT_ROWS, rows] = den_rows

    subs = [slice(r0, r0 + sub) for r0 in range(0, x_ref.shape[1], sub)]
    p_next = project(subs[0])
    for i, rows in enumerate(subs):
        p = p_next
        if i + 1 < len(subs):
            p_next = project(subs[i + 1])
        up_project(p, rows)


def _inproj(x, mod, mod_row, layer, g, w, gq, wq, gkv, wkv, tabs, tile):
    b, r, d = x.shape
    rope = tabs is not None
    nh = MLA_HEADS
    row = lambda width: pl.BlockSpec((1, tile, width), lambda i, j: (i, j, 0))
    head = pl.BlockSpec((1, nh, tile, HEAD_PAD), lambda i, j: (i, 0, j, 0))
    head_t = pl.BlockSpec((1, nh, VT_ROWS, tile), lambda i, j: (i, 0, 0, j))
    assert tile % min(IN_SUB, tile) == 0
    in_specs = [
        row(d),
        pl.BlockSpec((1, 1) + mod.shape[2:], lambda i, j: (layer, mod_row(i), 0, 0)),
    ] + [_layer_spec(a, layer) for a in (g, w, gq, wq, gkv, wkv)]
    args = [x, mod, g, w, gq, wq, gkv, wkv]
    if rope:
        in_specs += [pl.BlockSpec((tile, HEAD_PAD), lambda i, j: (j, 0))] * 2
        args += list(tabs)
    hshape = jax.ShapeDtypeStruct((b, nh, r, HEAD_PAD), BF16)
    return pl.pallas_call(
        functools.partial(_inproj_kernel, rope=rope),
        grid=(b, r // tile),
        in_specs=in_specs,
        out_specs=[row(4 * RET_W), row(POOL_W), head, head, head_t],
        out_shape=[jax.ShapeDtypeStruct((b, r, 4 * RET_W), BF16),
                   jax.ShapeDtypeStruct((b, r, POOL_W), BF16), hshape, hshape,
                   jax.ShapeDtypeStruct((b, nh, VT_ROWS, r), BF16)],
        compiler_params=_params(("parallel", "parallel")),
        name="in_proj",
    )(*args)


def _ret_kernel(*refs, layer, n_lat, n_ctx, need_ctx):
    if need_ctx:
        (lg_ref, *qkvg, c_ref, sa_ref, sb_ref, ol_ref, oc_ref, q_s, k_s, u_s, st_s) = refs
    else:
        (lg_ref, *qkvg, c_ref, sa_ref, sb_ref, ol_ref, q_s, k_s, u_s, st_s) = refs
        oc_ref = None
    lat_refs, ctx_refs = qkvg[0:4], qkvg[4:8]
    ch = RET_CHUNK
    pair = pl.program_id(1)
    lane = lax.broadcasted_iota(jnp.int32, (1, LANES), 1)
    lo = lane < RET_DIM
    sub_lo = lax.broadcasted_iota(jnp.int32, (LANES, 1), 0) < RET_DIM
    row = lax.broadcasted_iota(jnp.int32, (ch, 1), 0).astype(F32)
    col = lax.broadcasted_iota(jnp.int32, (1, ch), 1).astype(F32)

    lf_a, lf_b = lg_ref[layer, 0, 2 * pair], lg_ref[layer, 0, 2 * pair + 1]
    lb_a, lb_b = lg_ref[layer, 1, 2 * pair], lg_ref[layer, 1, 2 * pair + 1]
    lf = jnp.where(lo, lf_a, lf_b)
    lb = jnp.where(lo, lb_a, lb_b)
    lf_t = jnp.where(sub_lo, lf_a, lf_b)
    lb_t = jnp.where(sub_lo, lb_a, lb_b)
    wqf = jnp.exp(lf * (row + 1.0))
    wqb = jnp.exp(lb * (ch - row))
    wkf_t = jnp.exp(lf_t * (ch - 1.0 - col))
    wkb_t = jnp.exp(lb_t * col)
    gf = jnp.exp(lf * ch)
    gb = jnp.exp(lb * ch)
    diff = row - col

    def decay(l_f, l_b):
        return jnp.where(diff >= 0, jnp.exp(l_f * jnp.maximum(diff, 0.0)),
                         jnp.exp(l_b * jnp.maximum(-diff, 0.0)))

    k_scale = RET_DIM ** -0.5
    d_ab = jnp.concatenate([decay(lf_a, lb_a), decay(lf_b, lb_b)], axis=0) * k_scale
    wq_fb = jnp.concatenate([wqf, wqb], axis=1)
    wk_fb_t = jnp.concatenate([wkf_t, wkb_t], axis=0) * k_scale
    head_of_row = lax.broadcasted_iota(jnp.int32, (LANES, 1), 0) // RET_DIM
    blockdiag = (head_of_row == lane // RET_DIM).astype(F32)
    half = RET_DIM // 2
    n_all = n_lat + n_ctx

    def source(n):
        if n < n_lat:
            return lat_refs, slice(n * ch, (n + 1) * ch), True
        return ctx_refs, slice((n - n_lat) * ch, (n - n_lat + 1) * ch), False

    for n in range(n_all):
        (q_ref, k_ref, v_ref, _), src_rows, roped = source(n)
        rows = slice(n * ch, (n + 1) * ch)
        q = q_ref[0, src_rows, :].astype(F32)
        k = k_ref[0, src_rows, :].astype(F32)
        if roped:
            c, sa, sb = c_ref[src_rows, :], sa_ref[src_rows, :], sb_ref[src_rows, :]
            q = q * c + pltpu.roll(q, LANES - half, 1) * sa + pltpu.roll(q, half, 1) * sb
            k = k * c + pltpu.roll(k, LANES - half, 1) * sa + pltpu.roll(k, half, 1) * sb
        q_s[rows, :] = q
        k_s[rows, :] = k
        kt = k.T
        kt2 = jnp.concatenate([kt, kt], axis=0) * wk_fb_t
        u_s[n] = _dot(kt2.astype(BF16), v_ref[0, src_rows, :])

    order_f = list(range(n_lat, n_all)) + list(range(n_lat))
    order_b = list(range(n_all - 1, -1, -1))
    s = jnp.zeros((LANES, LANES), F32)
    for n in order_f:
        st_s[n, 0:LANES, :] = (s * blockdiag).astype(BF16)
        s = gf * s + u_s[n, 0:LANES, :]
    s = jnp.zeros((LANES, LANES), F32)
    for n in order_b:
        st_s[n, LANES:2 * LANES, :] = (s * blockdiag).astype(BF16)
        s = gb * s + u_s[n, LANES:2 * LANES, :]

    for n in range(n_all if need_ctx else n_lat):
        (_, _, v_ref, g_ref), src_rows, _ = source(n)
        dst_ref = ol_ref if n < n_lat else oc_ref
        rows = slice(n * ch, (n + 1) * ch)
        q = q_s[rows, :]
        kb = k_s[rows, :].astype(BF16)
        vb = v_ref[0, src_rows, :]
        gate = g_ref[0, src_rows, :].astype(F32)
        q_ab = jnp.concatenate([jnp.where(lo, q, 0.0), jnp.where(lo, 0.0, q)], axis=0).astype(BF16)
        o_ab = _dot((_dot_nt(q_ab, kb) * d_ab).astype(BF16), vb)
        o = jnp.where(lo, o_ab[0:ch], o_ab[ch:2 * ch])
        o = o + _dot((jnp.concatenate([q, q], axis=1) * wq_fb).astype(BF16), st_s[n])
        inv = 1.0 / RET_DIM
        mu = jnp.where(lo, jnp.sum(jnp.where(lo, o, 0.0), axis=-1, keepdims=True),
                       jnp.sum(jnp.where(lo, 0.0, o), axis=-1, keepdims=True)) * inv
        dlt = o - mu
        sq = dlt * dlt
        var = jnp.where(lo, jnp.sum(jnp.where(lo, sq, 0.0), axis=-1, keepdims=True),
                        jnp.sum(jnp.where(lo, 0.0, sq), axis=-1, keepdims=True)) * inv
        dst_ref[0, src_rows, :] = (_silu(gate) * (dlt * lax.rsqrt(var + EPS))).astype(dst_ref.dtype)


def _retention(lg, layer, ret_lat, ret_ctx, tabs, need_ctx):
    b, rl, _ = ret_lat.shape
    rc = ret_ctx.shape[1]
    ch = RET_CHUNK
    n_lat, n_ctx = rl // ch, rc // ch
    n_all = n_lat + n_ctx
    kern = functools.partial(_ret_kernel, layer=layer, n_lat=n_lat, n_ctx=n_ctx, need_ctx=need_ctx)
    tab_spec = pl.BlockSpec((rl, LANES), lambda i, p: (0, 0), pipeline_mode=pl.Buffered(1))
    pairs = RET_HEADS // 2

    def part_spec(rows, part):
        return pl.BlockSpec((1, rows, LANES), lambda i, p: (i, 0, part * pairs + p))
    out_specs = [pl.BlockSpec((1, rl, LANES), lambda i, p: (i, 0, p))]
    out_shape = [jax.ShapeDtypeStruct((b, rl, RET_W), BF16)]
    if need_ctx:
        out_specs.append(pl.BlockSpec((1, rc, LANES), lambda i, p: (i, 0, p)))
        out_shape.append(jax.ShapeDtypeStruct((b, rc, RET_W), BF16))
    return pl.pallas_call(
        kern,
        grid=(b, RET_HEADS // 2),
        in_specs=[
            pl.BlockSpec(memory_space=pltpu.SMEM),
            *[part_spec(rl, part) for part in range(4)], *[part_spec(rc, part) for part in range(4)],
            tab_spec, tab_spec, tab_spec,
        ],
        out_specs=out_specs,
        out_shape=out_shape,
        scratch_shapes=[
            pltpu.VMEM((rl + rc, LANES), F32), pltpu.VMEM((rl + rc, LANES), F32),
            pltpu.VMEM((n_all, 2 * LANES, LANES), F32),
            pltpu.VMEM((n_all, 2 * LANES, LANES), BF16),
        ],
        compiler_params=_params(("parallel", "parallel")),
        name="retention",
    )(lg, *[ret_lat] * 4, *[ret_ctx] * 4, *tabs)


def _attn_kernel(*refs, use_lat):
    if use_lat:
        q_ref, kl_ref, vtl_ref, kc_ref, vtc_ref, o_ref = refs
    else:
        q_ref, kc_ref, vtc_ref, o_ref = refs

    sources = [(kc_ref, vtc_ref)]
    if use_lat:
        sources.append((kl_ref, vtl_ref))

    n_sub = q_ref.shape[2] // Q_SUB
    items = [((qs, h), k_ref, vt_ref, c0) for qs in range(n_sub) for h in range(MLA_HEADS)
             for k_ref, vt_ref in sources for c0 in range(0, k_ref.shape[2], KEY_CHUNK)]
    last_item = {g: i for i, (g, _, _, _) in enumerate(items)}

    scores, m_run, acc, done = {}, {}, {}, {}
    for t in range(len(items) + SCORE_LOOKAHEAD):
        if t < len(items):
            (qs, h), k_ref, _, c0 = items[t]
            q = q_ref[0, h, qs * Q_SUB:(qs + 1) * Q_SUB, :]
            scores[t] = _dot_nt(k_ref[0, h, c0:c0 + KEY_CHUNK, :], q)
        i = t - SCORE_LOOKAHEAD
        if i < 0:
            continue
        g, _, vt_ref, c0 = items[i]
        qs, h = g
        s = scores.pop(i)
        m_new = jnp.max(s, axis=0, keepdims=True)
        if g in m_run:
            m_new = jnp.maximum(m_run[g], m_new)
        pv = _dot(vt_ref[0, h, :, c0:c0 + KEY_CHUNK], jnp.exp2(s - m_new).astype(BF16))
        acc[g] = pv if g not in acc else acc[g] * jnp.exp2(m_run[g] - m_new) + pv
        m_run[g] = m_new
        if i == last_item[g]:
            a = acc.pop(g)
            done[g] = a[0:MLA_V, :] * (1.0 / a[MLA_V:MLA_V + 1, :])
            if h % 2 == 1:
                pair_t = jnp.concatenate([done.pop((qs, h - 1)), done.pop(g)], axis=0)
                o_ref[0, qs * Q_SUB:(qs + 1) * Q_SUB, (h // 2) * HEAD_PAD:(h // 2 + 1) * HEAD_PAD] = (
                    pair_t.T.astype(o_ref.dtype))


def _attention(q, k_ctx, vt_ctx, k_lat=None, vt_lat=None):
    b, nh, rows, _ = q.shape
    use_lat = k_lat is not None
    tile = min(ATTN_TILE, rows)

    def whole(a):
        return pl.BlockSpec((1,) + a.shape[1:], lambda i, j: (i, 0, 0, 0))

    in_specs = [pl.BlockSpec((1, nh, tile, HEAD_PAD), lambda i, j: (i, 0, j, 0))]
    args = [q]
    if use_lat:
        in_specs += [whole(k_lat), whole(vt_lat)]
        args += [k_lat, vt_lat]
    in_specs += [whole(k_ctx), whole(vt_ctx)]
    args += [k_ctx, vt_ctx]
    return pl.pallas_call(
        functools.partial(_attn_kernel, use_lat=use_lat),
        grid=(b, rows // tile),
        in_specs=in_specs,
        out_specs=pl.BlockSpec((1, tile, MLA_W), lambda i, j: (i, j, 0)),
        out_shape=jax.ShapeDtypeStruct((b, rows, MLA_W), BF16),
        compiler_params=_params(("parallel", "arbitrary")),
        name="mla_attention",
    )(*args)


def _pool_kernel(*refs, lengths):
    n_seg = len(lengths)
    u_refs, (w_ref, s_ref), o_refs, pad_s = refs[:n_seg], refs[n_seg:n_seg + 2], refs[n_seg + 2:-1], refs[-1]
    lo = lax.broadcasted_iota(jnp.int32, (1, LANES), 1) < POOL_GDIM
    halo = POOL_HALO
    for u_ref, o_ref, length in zip(u_refs, o_refs, lengths):
        pad_s[0:halo, :] = jnp.zeros((halo, POOL_W), F32)
        pad_s[halo:halo + length, :] = u_ref[0].astype(F32)
        pad_s[halo + length:2 * halo + length, :] = jnp.zeros((halo, POOL_W), F32)
        for r0 in range(0, length, POOL_TILE):
            t = (lax.broadcasted_iota(jnp.int32, (POOL_TILE, 1), 0) + r0).astype(F32)
            cols = []
            for hf in range(POOL_W // LANES):
                cs = slice(hf * LANES, (hf + 1) * LANES)

                def rows_sum(first, count):
                    base = halo + r0 + first
                    tot = pad_s[base:base + POOL_TILE, cs]
                    for d in range(1, count):
                        tot = tot + pad_s[base + d:base + d + POOL_TILE, cs]
                    return tot

                def inv_count(w):
                    if r0 - w // 2 >= 0 and r0 + POOL_TILE - w // 2 + w <= length:
                        return 1.0 / w
                    cnt = (jnp.clip(t - w // 2 + w, 0.0, float(length))
                           - jnp.clip(t - w // 2, 0.0, float(length)))
                    return 1.0 / cnt

                w_n, w_w = POOL_WINDOWS[2 * hf], POOL_WINDOWS[2 * hf + 1]
                inner = rows_sum(-(w_n // 2), w_n)
                outer = (inner + rows_sum(-(w_w // 2), (w_w - w_n) // 2)
                         + rows_sum(w_n // 2, (w_w - w_n) // 2))
                pooled = jnp.where(lo, inner * inv_count(w_n), outer * inv_count(w_w))
                cols.append(pooled - pad_s[halo + r0:halo + r0 + POOL_TILE, cs])
            pooled = jnp.concatenate(cols, axis=-1).astype(BF16)
            y = _dot(pooled, w_ref[0]) * s_ref[0]
            o_ref[0, r0:r0 + POOL_TILE, :] = y.astype(o_ref.dtype)


def _pool(us, layer, w_bd, scale):
    b = us[0].shape[0]
    lengths = tuple(u.shape[1] for u in us)
    spec = lambda n: pl.BlockSpec((1, n, POOL_W), lambda i: (i, 0, 0))
    return pl.pallas_call(
        functools.partial(_pool_kernel, lengths=lengths),
        grid=(b,),
        in_specs=[spec(n) for n in lengths] + [_layer_spec(w_bd, layer), _layer_spec(scale, layer)],
        out_specs=[spec(n) for n in lengths],
        out_shape=[jax.ShapeDtypeStruct((b, n, POOL_W), BF16) for n in lengths],
        scratch_shapes=[pltpu.VMEM((max(lengths) + 2 * POOL_HALO, POOL_W), F32)],
        compiler_params=_params(("parallel",)),
        name="pool",
    )(*us, w_bd, scale)


def _mix_mlp_kernel(x_ref, r_ref, m_ref, p_ref, wo_ref, mod_ref, g_ref, w1_ref, w2_ref, gf_ref, o_ref,
                    *, final_norm):
    y = (_dot(r_ref[0], wo_ref[0, 0:RET_W, :])
         + _dot(m_ref[0], wo_ref[0, RET_W:RET_W + MLA_W, :])
         + _dot(p_ref[0], wo_ref[0, RET_W + MLA_W:, :]))
    x = x_ref[0] + mod_ref[0, 0, 2:3, :] * y
    h = (_rms(x, g_ref[0]) * (1.0 + mod_ref[0, 0, 4:5, :]) + mod_ref[0, 0, 3:4, :]).astype(BF16)
    acc = None
    for c0 in range(0, w1_ref.shape[2], FF_CHUNK):
        a = jnp.maximum(_dot(h, w1_ref[0, :, c0:c0 + FF_CHUNK]), 0.0)
        part = _dot((a * a).astype(BF16), w2_ref[0, c0:c0 + FF_CHUNK, :])
        acc = part if acc is None else acc + part
    y = x + mod_ref[0, 0, 5:6, :] * acc
    if final_norm:
        y = _rms(y, gf_ref[...])
    o_ref[0] = y


def _mix_mlp(x, ret, mla, pool, mod, mod_row, layer, w_out, g, w1, w2, g_final, tile, final_norm):
    b, r, d = x.shape
    row = lambda width: pl.BlockSpec((1, tile, width), lambda i, j: (i, j, 0))
    return pl.pallas_call(
        functools.partial(_mix_mlp_kernel, final_norm=final_norm),
        grid=(b, r // tile),
        in_specs=[
            row(d), row(RET_W), row(MLA_W), row(POOL_W),
            _layer_spec(w_out, layer),
            pl.BlockSpec((1, 1) + mod.shape[2:], lambda i, j: (layer, mod_row(i), 0, 0)),
            _layer_spec(g, layer), _layer_spec(w1, layer), _layer_spec(w2, layer),
            pl.BlockSpec(g_final.shape, lambda i, j: (0, 0)),
        ],
        out_specs=row(d),
        out_shape=jax.ShapeDtypeStruct((b, r, d), F32),
        compiler_params=_params(("parallel", "parallel")),
        name="mix_mlp",
    )(x, ret, mla, pool, w_out, mod, g, w1, w2, g_final)


def _in_proj_columns():
    zero = 4 * RET_W + MLA_Q_RANK + MLA_KV_RANK + MLA_ROPE + POOL_W
    src = np.full((IN_COLS,), zero, np.int32)
    sign = np.ones((IN_COLS,), np.float32)
    kpe0 = COL_U
    src[0:kpe0] = np.arange(kpe0)
    half = MLA_ROPE // 2
    src[COL_KPE:COL_KPE + MLA_ROPE] = kpe0 + np.arange(MLA_ROPE)
    rot0 = COL_KPE + MLA_ROPE
    src[rot0:rot0 + half] = kpe0 + half + np.arange(half)
    sign[rot0:rot0 + half] = -1.0
    src[rot0 + half:rot0 + MLA_ROPE] = kpe0 + np.arange(half)
    src[COL_U:COL_U + POOL_W] = kpe0 + MLA_ROPE + np.arange(POOL_W)
    return src, sign


def _uq_columns():
    per = MLA_NOPE + MLA_ROPE
    half = MLA_ROPE // 2
    src = np.full((UQ_COLS,), MLA_HEADS * per, np.int32)
    sign = np.ones((UQ_COLS,), np.float32)
    for h in range(MLA_HEADS):
        src[h * HEAD_PAD:h * HEAD_PAD + per] = h * per + np.arange(per)
        r0 = UQ_ROT + h * MLA_ROPE
        src[r0:r0 + half] = h * per + MLA_NOPE + half + np.arange(half)
        sign[r0:r0 + half] = -1.0
        src[r0 + half:r0 + MLA_ROPE] = h * per + MLA_NOPE + np.arange(half)
    return src, sign


def _ukv_columns():
    per = MLA_NOPE + MLA_V
    src = np.zeros((UKV_COLS,), np.int32)
    for h in range(MLA_HEADS):
        src[h * MLA_NOPE:(h + 1) * MLA_NOPE] = h * per + np.arange(MLA_NOPE)
        src[UKV_V + h * MLA_V:UKV_V + (h + 1) * MLA_V] = h * per + MLA_NOPE + np.arange(MLA_V)
    return src, np.ones(src.shape, np.float32)


def _take_cols(w, cols):
    src, sign = cols
    zero, n = w.shape[-1], len(src)
    wb = w.astype(BF16)
    pieces, i = [], 0
    while i < n:
        j = i + 1
        if src[i] == zero:
            while j < n and src[j] == zero:
                j += 1
            pieces.append(jnp.zeros(w.shape[:-1] + (j - i,), BF16))
        else:
            while j < n and src[j] != zero and src[j] == src[j - 1] + 1 and sign[j] == sign[i]:
                j += 1
            piece = wb[..., src[i]:src[i] + j - i]
            pieces.append(-piece if sign[i] < 0 else piece)
        i = j
    return jnp.concatenate(pieces, axis=-1)


def _rope_angles(n_rows, dim):
    pos = np.arange(n_rows)
    n_freq = dim // 4
    inv = np.float32(ROPE_BASE) ** (-np.arange(n_freq, dtype=np.float32) / np.float32(n_freq))
    return np.concatenate([(pos // GRID_W).astype(np.float32)[:, None] * inv,
                           (pos % GRID_W).astype(np.float32)[:, None] * inv], axis=-1)


def _ret_rope_tables(n_rows):
    ang = _rope_angles(n_rows, RET_DIM)
    cos, sin = np.cos(ang), np.sin(ang)
    zero = np.zeros_like(sin)
    reps = LANES // RET_DIM
    c = np.tile(np.concatenate([cos, cos], axis=-1), (1, reps))
    sa = np.tile(np.concatenate([-sin, zero], axis=-1), (1, reps))
    sb = np.tile(np.concatenate([zero, sin], axis=-1), (1, reps))
    return tuple(jnp.asarray(a, F32) for a in (c, sa, sb))


def _mla_rope_tables(n_rows):
    ang = _rope_angles(n_rows, MLA_ROPE)
    cos, sin = np.cos(ang), np.sin(ang)
    c = np.ones((n_rows, HEAD_PAD), np.float32)
    s = np.zeros((n_rows, HEAD_PAD), np.float32)
    c[:, MLA_NOPE:MLA_NOPE + MLA_ROPE] = np.concatenate([cos, cos], axis=-1)
    s[:, MLA_NOPE:MLA_NOPE + MLA_ROPE] = np.concatenate([sin, sin], axis=-1)
    return jnp.asarray(c, F32), jnp.asarray(s, F32)


def _block_diag(w_pool):
    depth, g, c, _ = w_pool.shape
    rows = []
    for i in range(g):
        blocks = [w_pool[:, i] if j == i else jnp.zeros((depth, c, c), w_pool.dtype) for j in range(g)]
        rows.append(jnp.concatenate(blocks, axis=2))
    return jnp.concatenate(rows, axis=1)


def kernel(x, c, ctx, c_ctx, w_ada, b_ada, norm_mix, w_in, q_norm, w_uq, kv_norm, w_ukv,
           ret_decay_logit, w_pool, pool_scale, w_out, norm_mlp, w_ff1, w_ff2, norm_final):
    b, n_lat, d = x.shape
    n_ctx = ctx.shape[1]
    depth = w_ada.shape[0]
    lat_tile = min(LAT_TILE, n_lat)
    assert n_lat % lat_tile == 0 and n_lat % ATTN_TILE == 0 and n_lat % GRID_W == 0
    assert n_ctx % RET_CHUNK == 0 and n_ctx % POOL_TILE == 0 and n_lat % POOL_TILE == 0
    assert d == 4 * RET_W and w_ff1.shape[2] % FF_CHUNK == 0

    in_cols, uq_cols, ukv_cols = _in_proj_columns(), _uq_columns(), _ukv_columns()
    ret_tabs = _ret_rope_tables(n_lat)
    mla_tabs = _mla_rope_tables(n_lat)

    rows = -(-(b + 1) // 8) * 8
    cc = jnp.zeros((rows, d), F32).at[:b].set(c).at[b].set(c_ctx)
    mod = _ada(cc, w_ada, b_ada).reshape(depth, rows, 6, d)
    lat_row, ctx_row = (lambda i: i), (lambda i: b)
    lg = jax.nn.log_sigmoid(ret_decay_logit.astype(F32))

    def rows3(a):
        return a.reshape(depth, 1, a.shape[-1])

    proj_w = (rows3(norm_mix), _take_cols(w_in, in_cols), rows3(q_norm), _take_cols(w_uq, uq_cols),
              rows3(kv_norm), _take_cols(w_ukv, ukv_cols))
    pool_w = (_block_diag(w_pool).astype(BF16), rows3(pool_scale))
    mlp_w = (w_out.astype(BF16), rows3(norm_mlp), w_ff1.astype(BF16), w_ff2.astype(BF16), norm_final[None])

    h_ctx = ctx
    for l in range(depth):
        last = l == depth - 1
        ret_l, u_l, q_l, k_l, vt_l = _inproj(x, mod, lat_row, l, *proj_w, mla_tabs, lat_tile)
        ret_c, u_c, q_c, k_c, vt_c = _inproj(h_ctx, mod, ctx_row, l, *proj_w, None, n_ctx)
        ret_o = _retention(lg, l, ret_l, ret_c, ret_tabs, not last)
        mla_l = _attention(q_l, k_c, vt_c, k_l, vt_l)
        if last:
            (pool_l,) = _pool((u_l,), l, *pool_w)
        else:
            pool_l, pool_c = _pool((u_l, u_c), l, *pool_w)
            mla_c = _attention(q_c, k_c, vt_c)
            h_ctx = _mix_mlp(h_ctx, ret_o[1], mla_c, pool_c, mod, ctx_row, l, *mlp_w, n_ctx, False)
        x = _mix_mlp(x, ret_o[0], mla_l, pool_l, mod, lat_row, l, *mlp_w, lat_tile, last)
    return x
```

```python
import functools
import math

import jax
import jax.numpy as jnp
import numpy as np
from jax import lax
from jax.experimental import pallas as pl
from jax.experimental.pallas import tpu as pltpu

GRID_W = 64
RET_HEADS = 4
RET_DIM = 64
RET_W = RET_HEADS * RET_DIM
RET_CHUNK = 128
MLA_HEADS = 8
MLA_NOPE = 64
MLA_ROPE = 32
MLA_V = 64
MLA_Q_RANK = 256
MLA_KV_RANK = 128
MLA_W = MLA_HEADS * MLA_V
POOL_GROUPS = 4
POOL_WINDOWS = (2, 4, 8, 16)
POOL_GDIM = 64
POOL_W = POOL_GROUPS * POOL_GDIM
ROPE_BASE = 10000.0
EPS = 1e-6

LANES = 128
LAT_TILE = 512
IN_SUB = 256
ATTN_TILE = 512
Q_SUB = 256
KEY_CHUNK = 256
SCORE_LOOKAHEAD = 5
POOL_TILE = 256
HEAD_PAD = 128
VT_ROWS = MLA_V + 16
POOL_HALO = 128
FF_CHUNK = 1024
VMEM_LIMIT = 56 * 1024 * 1024

COL_RET = 0
COL_CQ = 4 * RET_W
COL_CKV = COL_CQ + MLA_Q_RANK
COL_U = COL_CKV + MLA_KV_RANK
COL_KPE = COL_U + POOL_W
IN_COLS = COL_KPE + HEAD_PAD
UQ_ROT = MLA_HEADS * HEAD_PAD
UQ_COLS = UQ_ROT + MLA_HEADS * MLA_ROPE
UKV_V = MLA_HEADS * MLA_NOPE
UKV_COLS = UKV_V + MLA_HEADS * MLA_V

F32 = jnp.float32
BF16 = jnp.bfloat16


def _params(sem):
    return pltpu.CompilerParams(dimension_semantics=sem, vmem_limit_bytes=VMEM_LIMIT)


def _layer_spec(a, layer):
    tail = a.shape[1:]
    return pl.BlockSpec((1,) + tail, lambda *_: (layer,) + (0,) * len(tail), pipeline_mode=pl.Buffered(1))


def _dot(a, b):
    return jnp.dot(a, b, preferred_element_type=F32)


def _dot_nt(a, b):
    return lax.dot_general(a, b, (((1,), (1,)), ((), ())), preferred_element_type=F32)


def _rms(x, g):
    return x * lax.rsqrt(jnp.mean(x * x, axis=-1, keepdims=True) + EPS) * g


def _silu(x):
    return x * (1.0 / (1.0 + jnp.exp(-x)))


def _ada_kernel(c_ref, w_ref, b_ref, o_ref):
    s = _silu(c_ref[...]).astype(BF16)
    o_ref[0] = _dot(s, w_ref[0].astype(BF16)) + b_ref[0]


def _ada(cc, w_ada, b_ada):
    depth, d, n = w_ada.shape
    rows = cc.shape[0]
    tn = n // 4
    return pl.pallas_call(
        _ada_kernel,
        grid=(depth, n // tn),
        in_specs=[
            pl.BlockSpec((rows, d), lambda l, j: (0, 0)),
            pl.BlockSpec((1, d, tn), lambda l, j: (l, 0, j)),
            pl.BlockSpec((1, 1, tn), lambda l, j: (l, 0, j)),
        ],
        out_specs=pl.BlockSpec((1, rows, tn), lambda l, j: (l, 0, j)),
        out_shape=jax.ShapeDtypeStruct((depth, rows, n), F32),
        compiler_params=_params(("parallel", "parallel")),
        name="ada_mod",
    )(cc, w_ada, b_ada.reshape(depth, 1, n))


def _inproj_kernel(*refs, rope):
    if rope:
        (x_ref, mod_ref, g_ref, w_ref, gq_ref, wq_ref, gkv_ref, wkv_ref, c_ref, s_ref,
         ret_ref, u_ref, q_ref, k_ref, vt_ref) = refs
    else:
        (x_ref, mod_ref, g_ref, w_ref, gq_ref, wq_ref, gkv_ref, wkv_ref,
         ret_ref, u_ref, q_ref, k_ref, vt_ref) = refs
    nh = MLA_HEADS
    lane = lax.broadcasted_iota(jnp.int32, (1, HEAD_PAD), 1)
    nope_lanes = lane < MLA_NOPE
    rope_lanes = jnp.logical_and(lane >= MLA_NOPE, lane < MLA_NOPE + MLA_ROPE)
    sub = min(IN_SUB, x_ref.shape[1])
    den_rows = (lax.broadcasted_iota(jnp.int32, (VT_ROWS - MLA_V, sub), 0) == 0).astype(vt_ref.dtype)
    q_scale = (MLA_NOPE + MLA_ROPE) ** -0.5 * math.log2(math.e)

    def project(rows):
        h = _rms(x_ref[0, rows, :], g_ref[0]) * (1.0 + mod_ref[0, 0, 1:2, :]) + mod_ref[0, 0, 0:1, :]
        p = _dot(h.astype(BF16), w_ref[0])
        ret_ref[0, rows, :] = p[:, COL_RET:COL_CQ].astype(ret_ref.dtype)
        u_ref[0, rows, :] = p[:, COL_U:COL_KPE].astype(u_ref.dtype)
        return p

    def up_project(p, rows):
        yq = _rms(p[:, COL_CQ:COL_CKV], gq_ref[0]).astype(BF16)
        if rope:
            cos, sin = c_ref[rows, :], s_ref[rows, :]
            q2 = _dot(yq, wq_ref[0])
        else:
            q2 = _dot(yq, wq_ref[0, :, 0:UQ_ROT])
        ykv = _rms(p[:, COL_CKV:COL_U], gkv_ref[0]).astype(BF16)
        kv = _dot(ykv, wkv_ref[0])
        kx = p[:, COL_KPE:IN_COLS]
        kpe = pltpu.roll(kx, MLA_NOPE, 1)
        if rope:
            kpe = kpe * cos + pltpu.roll(kx, MLA_NOPE - MLA_ROPE, 1) * sin
        kpe = jnp.where(rope_lanes, kpe, 0.0)
        for hh in range(nh):
            qh = q2[:, hh * HEAD_PAD:(hh + 1) * HEAD_PAD]
            if rope:
                blk, j = divmod(hh * MLA_ROPE, HEAD_PAD)
                rot = q2[:, UQ_ROT + blk * HEAD_PAD:UQ_ROT + (blk + 1) * HEAD_PAD]
                shift = (MLA_NOPE - j) % HEAD_PAD
                qh = qh * cos + (pltpu.roll(rot, shift, 1) if shift else rot) * sin
            q_ref[0, hh, rows, :] = (qh * q_scale).astype(q_ref.dtype)
            blk, j = divmod(hh * MLA_NOPE, HEAD_PAD)
            kn = kv[:, blk * HEAD_PAD:(blk + 1) * HEAD_PAD]
            if j:
                kn = pltpu.roll(kn, HEAD_PAD - j, 1)
            k_ref[0, hh, rows, :] = jnp.where(nope_lanes, kn, kpe).astype(k_ref.dtype)
        for blk in range(nh * MLA_V // HEAD_PAD):
            vt = kv[:, UKV_V + blk * HEAD_PAD:UKV_V + (blk + 1) * HEAD_PAD].T.astype(vt_ref.dtype)
            for j in range(HEAD_PAD // MLA_V):
                hh = blk * (HEAD_PAD // MLA_V) + j
                vt_ref[0, hh, 0:MLA_V, rows] = vt[j * MLA_V:(j + 1) * MLA_V, :]
                vt_ref[0, hh, MLA_V:VT_ROWS, rows] = den_rows

    subs = [slice(r0, r0 + sub) for r0 in range(0, x_ref.shape[1], sub)]
    p_next = project(subs[0])
    for i, rows in enumerate(subs):
        p = p_next
        if i + 1 < len(subs):
            p_next = project(subs[i + 1])
        up_project(p, rows)


def _inproj(x, mod, mod_row, layer, g, w, gq, wq, gkv, wkv, tabs, tile):
    b, r, d = x.shape
    rope = tabs is not None
    nh = MLA_HEADS
    row = lambda width: pl.BlockSpec((1, tile, width), lambda i, j: (i, j, 0))
    head = pl.BlockSpec((1, nh, tile, HEAD_PAD), lambda i, j: (i, 0, j, 0))
    head_t = pl.BlockSpec((1, nh, VT_ROWS, tile), lambda i, j: (i, 0, 0, j))
    assert tile % min(IN_SUB, tile) == 0
    in_specs = [
        row(d),
        pl.BlockSpec((1, 1) + mod.shape[2:], lambda i, j: (layer, mod_row(i), 0, 0)),
    ] + [_layer_spec(a, layer) for a in (g, w, gq, wq, gkv, wkv)]
    args = [x, mod, g, w, gq, wq, gkv, wkv]
    if rope:
        in_specs += [pl.BlockSpec((tile, HEAD_PAD), lambda i, j: (j, 0))] * 2
        args += list(tabs)
    hshape = jax.ShapeDtypeStruct((b, nh, r, HEAD_PAD), BF16)
    return pl.pallas_call(
        functools.partial(_inproj_kernel, rope=rope),
        grid=(b, r // tile),
        in_specs=in_specs,
        out_specs=[row(4 * RET_W), row(POOL_W), head, head, head_t],
        out_shape=[jax.ShapeDtypeStruct((b, r, 4 * RET_W), BF16),
                   jax.ShapeDtypeStruct((b, r, POOL_W), BF16), hshape, hshape,
                   jax.ShapeDtypeStruct((b, nh, VT_ROWS, r), BF16)],
        compiler_params=_params(("parallel", "parallel")),
        name="in_proj",
    )(*args)


def _ret_pool_kernel(*refs, layer, n_lat, n_ctx, need_ctx):
    n_seg = 2 if need_ctx else 1
    it = iter(refs)
    lg_ref, rl_ref, rc_ref, c_ref, sa_ref, sb_ref = [next(it) for _ in range(6)]
    u_refs = [next(it) for _ in range(n_seg)]
    band_ref, w_ref, s_ref = next(it), next(it), next(it)
    ro_refs = [next(it) for _ in range(n_seg)]
    po_refs = [next(it) for _ in range(n_seg)]
    q_s, k_s, u_s, st_s, pad_s = it
    ch = RET_CHUNK
    pairs = RET_HEADS // 2
    n_all = n_lat + n_ctx
    lane = lax.broadcasted_iota(jnp.int32, (1, LANES), 1)
    lo = lane < RET_DIM
    sub_lo = lax.broadcasted_iota(jnp.int32, (LANES, 1), 0) < RET_DIM
    row = lax.broadcasted_iota(jnp.int32, (ch, 1), 0).astype(F32)
    col = lax.broadcasted_iota(jnp.int32, (1, ch), 1).astype(F32)
    diff = row - col
    head_of_row = lax.broadcasted_iota(jnp.int32, (LANES, 1), 0) // RET_DIM
    blockdiag = (head_of_row == lane // RET_DIM).astype(F32)
    k_scale = RET_DIM ** -0.5
    half = RET_DIM // 2

    def pair_tables(p):
        lf_a, lf_b = lg_ref[layer, 0, 2 * p], lg_ref[layer, 0, 2 * p + 1]
        lb_a, lb_b = lg_ref[layer, 1, 2 * p], lg_ref[layer, 1, 2 * p + 1]
        lf = jnp.where(lo, lf_a, lf_b)
        lb = jnp.where(lo, lb_a, lb_b)
        lf_t = jnp.where(sub_lo, lf_a, lf_b)
        lb_t = jnp.where(sub_lo, lb_a, lb_b)

        def decay(l_f, l_b):
            return jnp.where(diff >= 0, jnp.exp(l_f * jnp.maximum(diff, 0.0)),
                             jnp.exp(l_b * jnp.maximum(-diff, 0.0)))

        return dict(
            d_ab=jnp.concatenate([decay(lf_a, lb_a), decay(lf_b, lb_b)], axis=0) * k_scale,
            wq_fb=jnp.concatenate([jnp.exp(lf * (row + 1.0)), jnp.exp(lb * (ch - row))], axis=1),
            wk_fb_t=jnp.concatenate([jnp.exp(lf_t * (ch - 1.0 - col)), jnp.exp(lb_t * col)], axis=0) * k_scale,
            gf=jnp.exp(lf * ch), gb=jnp.exp(lb * ch))

    tables = [pair_tables(p) for p in range(pairs)]

    def source(n):
        if n < n_lat:
            return rl_ref, slice(n * ch, (n + 1) * ch), True
        return rc_ref, slice((n - n_lat) * ch, (n - n_lat + 1) * ch), False

    def part(p, which):
        c0 = (which * pairs + p) * LANES
        return slice(c0, c0 + LANES)

    lo_pool = lane < POOL_GDIM
    halo = POOL_HALO

    def pool_fill(u_ref, length):
        pad_s[0:halo, :] = jnp.zeros((halo, POOL_W), BF16)
        pad_s[halo:halo + length, :] = u_ref[0]
        pad_s[halo + length:2 * halo + length, :] = jnp.zeros((halo, POOL_W), BF16)

    def pool_slab(u_ref, o_ref, length, r0):
        t = (lax.broadcasted_iota(jnp.int32, (POOL_TILE, 1), 0) + r0).astype(F32)
        cols = []
        for hf in range(POOL_W // LANES):
            cs = slice(hf * LANES, (hf + 1) * LANES)
            tokens = pad_s[r0:r0 + POOL_TILE + 2 * halo, cs]

            def inv_count(w):
                if r0 - w // 2 >= 0 and r0 + POOL_TILE - w // 2 + w <= length:
                    return 1.0 / w
                cnt = (jnp.clip(t - w // 2 + w, 0.0, float(length))
                       - jnp.clip(t - w // 2, 0.0, float(length)))
                return 1.0 / cnt

            w_n, w_w = POOL_WINDOWS[2 * hf], POOL_WINDOWS[2 * hf + 1]
            pooled = jnp.where(lo_pool, _dot(band_ref[2 * hf], tokens) * inv_count(w_n),
                               _dot(band_ref[2 * hf + 1], tokens) * inv_count(w_w))
            cols.append(pooled - u_ref[0, r0:r0 + POOL_TILE, cs].astype(F32))
        pooled = jnp.concatenate(cols, axis=-1).astype(BF16)
        y = _dot(pooled, w_ref[0]) * s_ref[0]
        o_ref[0, r0:r0 + POOL_TILE, :] = y.astype(o_ref.dtype)

    pool_tasks = []
    for u_ref, o_ref in zip(u_refs, po_refs):
        length = u_ref.shape[1]
        pool_tasks.append(functools.partial(pool_fill, u_ref, length))
        pool_tasks += [functools.partial(pool_slab, u_ref, o_ref, length, r0)
                       for r0 in range(0, length, POOL_TILE)]
    pool_tasks.reverse()

    for n in range(n_all):
        src_ref, src_rows, roped = source(n)
        rows = slice(n * ch, (n + 1) * ch)
        for p in range(pairs):
            q = src_ref[0, src_rows, part(p, 0)].astype(F32)
            k = src_ref[0, src_rows, part(p, 1)].astype(F32)
            if roped:
                c, sa, sb = c_ref[src_rows, :], sa_ref[src_rows, :], sb_ref[src_rows, :]
                q = q * c + pltpu.roll(q, LANES - half, 1) * sa + pltpu.roll(q, half, 1) * sb
                k = k * c + pltpu.roll(k, LANES - half, 1) * sa + pltpu.roll(k, half, 1) * sb
            q_s[p, rows, :] = q
            k_s[p, rows, :] = k
            kt = k.T
            kt2 = jnp.concatenate([kt, kt], axis=0) * tables[p]["wk_fb_t"]
            u_s[p, n] = _dot(kt2.astype(BF16), src_ref[0, src_rows, part(p, 2)])

    order_f = list(range(n_lat, n_all)) + list(range(n_lat))
    order_b = list(range(n_all - 1, -1, -1))
    for p in range(pairs):
        s = jnp.zeros((LANES, LANES), F32)
        for n in order_f:
            st_s[p, n, 0:LANES, :] = (s * blockdiag).astype(BF16)
            s = tables[p]["gf"] * s + u_s[p, n, 0:LANES, :]
        s = jnp.zeros((LANES, LANES), F32)
        for n in order_b:
            st_s[p, n, LANES:2 * LANES, :] = (s * blockdiag).astype(BF16)
            s = tables[p]["gb"] * s + u_s[p, n, LANES:2 * LANES, :]

    n_out = n_all if need_ctx else n_lat

    def scores_and_state(n):
        rows = slice(n * ch, (n + 1) * ch)
        out = []
        for p in range(pairs):
            q = q_s[p, rows, :]
            kb = k_s[p, rows, :].astype(BF16)
            q_ab = jnp.concatenate([jnp.where(lo, q, 0.0), jnp.where(lo, 0.0, q)], axis=0).astype(BF16)
            a_ab = _dot_nt(q_ab, kb)
            carried = _dot((jnp.concatenate([q, q], axis=1) * tables[p]["wq_fb"]).astype(BF16), st_s[p, n])
            out.append((a_ab, carried))
        return out

    def finish(n, staged):
        src_ref, src_rows, _ = source(n)
        dst_ref = ro_refs[0] if n < n_lat else ro_refs[1]
        for p, (a_ab, carried) in enumerate(staged):
            vb = src_ref[0, src_rows, part(p, 2)]
            gate = src_ref[0, src_rows, part(p, 3)].astype(F32)
            o_ab = _dot((a_ab * tables[p]["d_ab"]).astype(BF16), vb)
            o = jnp.where(lo, o_ab[0:ch], o_ab[ch:2 * ch]) + carried
            inv = 1.0 / RET_DIM
            mu = jnp.where(lo, jnp.sum(jnp.where(lo, o, 0.0), axis=-1, keepdims=True),
                           jnp.sum(jnp.where(lo, 0.0, o), axis=-1, keepdims=True)) * inv
            dlt = o - mu
            sq = dlt * dlt
            var = jnp.where(lo, jnp.sum(jnp.where(lo, sq, 0.0), axis=-1, keepdims=True),
                            jnp.sum(jnp.where(lo, 0.0, sq), axis=-1, keepdims=True)) * inv
            dst_ref[0, src_rows, p * LANES:(p + 1) * LANES] = (
                _silu(gate) * (dlt * lax.rsqrt(var + EPS))).astype(dst_ref.dtype)

    staged = scores_and_state(0)
    for n in range(n_out):
        staged_next = scores_and_state(n + 1) if n + 1 < n_out else None
        if pool_tasks:
            pool_tasks.pop()()
        finish(n, staged)
        staged = staged_next
    while pool_tasks:
        pool_tasks.pop()()


def _ret_pool(lg, layer, ret_lat, ret_ctx, tabs, us, band, w_bd, scale, need_ctx):
    b, rl, _ = ret_lat.shape
    rc = ret_ctx.shape[1]
    ch = RET_CHUNK
    n_lat, n_ctx = rl // ch, rc // ch
    n_all = n_lat + n_ctx
    pairs = RET_HEADS // 2
    lengths = tuple(u.shape[1] for u in us)
    assert len(us) == (2 if need_ctx else 1)
    kern = functools.partial(_ret_pool_kernel, layer=layer, n_lat=n_lat, n_ctx=n_ctx, need_ctx=need_ctx)
    tab_spec = pl.BlockSpec((rl, LANES), lambda i: (0, 0), pipeline_mode=pl.Buffered(1))
    rows = lambda n, width: pl.BlockSpec((1, n, width), lambda i: (i, 0, 0))
    out_specs = [rows(n, RET_W) for n in lengths] + [rows(n, POOL_W) for n in lengths]
    out_shape = ([jax.ShapeDtypeStruct((b, n, RET_W), BF16) for n in lengths]
                 + [jax.ShapeDtypeStruct((b, n, POOL_W), BF16) for n in lengths])
    outs = pl.pallas_call(
        kern,
        grid=(b,),
        in_specs=[
            pl.BlockSpec(memory_space=pltpu.SMEM),
            rows(rl, 4 * RET_W), rows(rc, 4 * RET_W),
            tab_spec, tab_spec, tab_spec,
            *[rows(n, POOL_W) for n in lengths],
            pl.BlockSpec(band.shape, lambda i: (0, 0, 0), pipeline_mode=pl.Buffered(1)),
            _layer_spec(w_bd, layer), _layer_spec(scale, layer),
        ],
        out_specs=out_specs,
        out_shape=out_shape,
        scratch_shapes=[
            pltpu.VMEM((pairs, rl + rc, LANES), F32), pltpu.VMEM((pairs, rl + rc, LANES), F32),
            pltpu.VMEM((pairs, n_all, 2 * LANES, LANES), F32),
            pltpu.VMEM((pairs, n_all, 2 * LANES, LANES), BF16),
            pltpu.VMEM((max(lengths) + 2 * POOL_HALO, POOL_W), BF16),
        ],
        compiler_params=_params(("parallel",)),
        name="retention_pool",
    )(lg, ret_lat, ret_ctx, *tabs, *us, band, w_bd, scale)
    n = len(lengths)
    return outs[:n], outs[n:]


def _attn_kernel(*refs, use_lat):
    if use_lat:
        q_ref, kl_ref, vtl_ref, kc_ref, vtc_ref, o_ref = refs
    else:
        q_ref, kc_ref, vtc_ref, o_ref = refs

    sources = [(kc_ref, vtc_ref)]
    if use_lat:
        sources.append((kl_ref, vtl_ref))

    n_sub = q_ref.shape[2] // Q_SUB
    items = [((qs, h), k_ref, vt_ref, c0) for qs in range(n_sub) for h in range(MLA_HEADS)
             for k_ref, vt_ref in sources for c0 in range(0, k_ref.shape[2], KEY_CHUNK)]
    last_item = {g: i for i, (g, _, _, _) in enumerate(items)}

    scores, m_run, acc, done = {}, {}, {}, {}
    for t in range(len(items) + SCORE_LOOKAHEAD):
        if t < len(items):
            (qs, h), k_ref, _, c0 = items[t]
            q = q_ref[0, h, qs * Q_SUB:(qs + 1) * Q_SUB, :]
            scores[t] = _dot_nt(k_ref[0, h, c0:c0 + KEY_CHUNK, :], q)
        i = t - SCORE_LOOKAHEAD
        if i < 0:
            continue
        g, _, vt_ref, c0 = items[i]
        qs, h = g
        s = scores.pop(i)
        m_new = jnp.max(s, axis=0, keepdims=True)
        if g in m_run:
            m_new = jnp.maximum(m_run[g], m_new)
        pv = _dot(vt_ref[0, h, :, c0:c0 + KEY_CHUNK], jnp.exp2(s - m_new).astype(BF16))
        acc[g] = pv if g not in acc else acc[g] * jnp.exp2(m_run[g] - m_new) + pv
        m_run[g] = m_new
        if i == last_item[g]:
            a = acc.pop(g)
            done[g] = a[0:MLA_V, :] * (1.0 / a[MLA_V:MLA_V + 1, :])
            if h % 2 == 1:
                pair_t = jnp.concatenate([done.pop((qs, h - 1)), done.pop(g)], axis=0)
                o_ref[0, qs * Q_SUB:(qs + 1) * Q_SUB, (h // 2) * HEAD_PAD:(h // 2 + 1) * HEAD_PAD] = (
                    pair_t.T.astype(o_ref.dtype))


def _attention(q, k_ctx, vt_ctx, k_lat=None, vt_lat=None):
    b, nh, rows, _ = q.shape
    use_lat = k_lat is not None
    tile = min(ATTN_TILE, rows)

    def whole(a):
        return pl.BlockSpec((1,) + a.shape[1:], lambda i, j: (i, 0, 0, 0))

    in_specs = [pl.BlockSpec((1, nh, tile, HEAD_PAD), lambda i, j: (i, 0, j, 0))]
    args = [q]
    if use_lat:
        in_specs += [whole(k_lat), whole(vt_lat)]
        args += [k_lat, vt_lat]
    in_specs += [whole(k_ctx), whole(vt_ctx)]
    args += [k_ctx, vt_ctx]
    return pl.pallas_call(
        functools.partial(_attn_kernel, use_lat=use_lat),
        grid=(b, rows // tile),
        in_specs=in_specs,
        out_specs=pl.BlockSpec((1, tile, MLA_W), lambda i, j: (i, j, 0)),
        out_shape=jax.ShapeDtypeStruct((b, rows, MLA_W), BF16),
        compiler_params=_params(("parallel", "arbitrary")),
        name="mla_attention",
    )(*args)


def _mix_mlp_kernel(x_ref, r_ref, m_ref, p_ref, wo_ref, mod_ref, g_ref, w1_ref, w2_ref, gf_ref, o_ref,
                    *, final_norm):
    y = (_dot(r_ref[0], wo_ref[0, 0:RET_W, :])
         + _dot(m_ref[0], wo_ref[0, RET_W:RET_W + MLA_W, :])
         + _dot(p_ref[0], wo_ref[0, RET_W + MLA_W:, :]))
    x = x_ref[0] + mod_ref[0, 0, 2:3, :] * y
    h = (_rms(x, g_ref[0]) * (1.0 + mod_ref[0, 0, 4:5, :]) + mod_ref[0, 0, 3:4, :]).astype(BF16)
    acc = None
    for c0 in range(0, w1_ref.shape[2], FF_CHUNK):
        a = jnp.maximum(_dot(h, w1_ref[0, :, c0:c0 + FF_CHUNK]), 0.0)
        part = _dot((a * a).astype(BF16), w2_ref[0, c0:c0 + FF_CHUNK, :])
        acc = part if acc is None else acc + part
    y = x + mod_ref[0, 0, 5:6, :] * acc
    if final_norm:
        y = _rms(y, gf_ref[...])
    o_ref[0] = y


def _mix_mlp(x, ret, mla, pool, mod, mod_row, layer, w_out, g, w1, w2, g_final, tile, final_norm):
    b, r, d = x.shape
    row = lambda width: pl.BlockSpec((1, tile, width), lambda i, j: (i, j, 0))
    return pl.pallas_call(
        functools.partial(_mix_mlp_kernel, final_norm=final_norm),
        grid=(b, r // tile),
        in_specs=[
            row(d), row(RET_W), row(MLA_W), row(POOL_W),
            _layer_spec(w_out, layer),
            pl.BlockSpec((1, 1) + mod.shape[2:], lambda i, j: (layer, mod_row(i), 0, 0)),
            _layer_spec(g, layer), _layer_spec(w1, layer), _layer_spec(w2, layer),
            pl.BlockSpec(g_final.shape, lambda i, j: (0, 0)),
        ],
        out_specs=row(d),
        out_shape=jax.ShapeDtypeStruct((b, r, d), F32),
        compiler_params=_params(("parallel", "parallel")),
        name="mix_mlp",
    )(x, ret, mla, pool, w_out, mod, g, w1, w2, g_final)


def _in_proj_columns():
    zero = 4 * RET_W + MLA_Q_RANK + MLA_KV_RANK + MLA_ROPE + POOL_W
    src = np.full((IN_COLS,), zero, np.int32)
    sign = np.ones((IN_COLS,), np.float32)
    kpe0 = COL_U
    src[0:kpe0] = np.arange(kpe0)
    half = MLA_ROPE // 2
    src[COL_KPE:COL_KPE + MLA_ROPE] = kpe0 + np.arange(MLA_ROPE)
    rot0 = COL_KPE + MLA_ROPE
    src[rot0:rot0 + half] = kpe0 + half + np.arange(half)
    sign[rot0:rot0 + half] = -1.0
    src[rot0 + half:rot0 + MLA_ROPE] = kpe0 + np.arange(half)
    src[COL_U:COL_U + POOL_W] = kpe0 + MLA_ROPE + np.arange(POOL_W)
    return src, sign


def _uq_columns():
    per = MLA_NOPE + MLA_ROPE
    half = MLA_ROPE // 2
    src = np.full((UQ_COLS,), MLA_HEADS * per, np.int32)
    sign = np.ones((UQ_COLS,), np.float32)
    for h in range(MLA_HEADS):
        src[h * HEAD_PAD:h * HEAD_PAD + per] = h * per + np.arange(per)
        r0 = UQ_ROT + h * MLA_ROPE
        src[r0:r0 + half] = h * per + MLA_NOPE + half + np.arange(half)
        sign[r0:r0 + half] = -1.0
        src[r0 + half:r0 + MLA_ROPE] = h * per + MLA_NOPE + np.arange(half)
    return src, sign


def _ukv_columns():
    per = MLA_NOPE + MLA_V
    src = np.zeros((UKV_COLS,), np.int32)
    for h in range(MLA_HEADS):
        src[h * MLA_NOPE:(h + 1) * MLA_NOPE] = h * per + np.arange(MLA_NOPE)
        src[UKV_V + h * MLA_V:UKV_V + (h + 1) * MLA_V] = h * per + MLA_NOPE + np.arange(MLA_V)
    return src, np.ones(src.shape, np.float32)


def _take_cols(w, cols):
    src, sign = cols
    zero, n = w.shape[-1], len(src)
    wb = w.astype(BF16)
    pieces, i = [], 0
    while i < n:
        j = i + 1
        if src[i] == zero:
            while j < n and src[j] == zero:
                j += 1
            pieces.append(jnp.zeros(w.shape[:-1] + (j - i,), BF16))
        else:
            while j < n and src[j] != zero and src[j] == src[j - 1] + 1 and sign[j] == sign[i]:
                j += 1
            piece = wb[..., src[i]:src[i] + j - i]
            pieces.append(-piece if sign[i] < 0 else piece)
        i = j
    return jnp.concatenate(pieces, axis=-1)


def _rope_angles(n_rows, dim):
    pos = np.arange(n_rows)
    n_freq = dim // 4
    inv = np.float32(ROPE_BASE) ** (-np.arange(n_freq, dtype=np.float32) / np.float32(n_freq))
    return np.concatenate([(pos // GRID_W).astype(np.float32)[:, None] * inv,
                           (pos % GRID_W).astype(np.float32)[:, None] * inv], axis=-1)


def _ret_rope_tables(n_rows):
    ang = _rope_angles(n_rows, RET_DIM)
    cos, sin = np.cos(ang), np.sin(ang)
    zero = np.zeros_like(sin)
    reps = LANES // RET_DIM
    c = np.tile(np.concatenate([cos, cos], axis=-1), (1, reps))
    sa = np.tile(np.concatenate([-sin, zero], axis=-1), (1, reps))
    sb = np.tile(np.concatenate([zero, sin], axis=-1), (1, reps))
    return tuple(jnp.asarray(a, F32) for a in (c, sa, sb))


def _mla_rope_tables(n_rows):
    ang = _rope_angles(n_rows, MLA_ROPE)
    cos, sin = np.cos(ang), np.sin(ang)
    c = np.ones((n_rows, HEAD_PAD), np.float32)
    s = np.zeros((n_rows, HEAD_PAD), np.float32)
    c[:, MLA_NOPE:MLA_NOPE + MLA_ROPE] = np.concatenate([cos, cos], axis=-1)
    s[:, MLA_NOPE:MLA_NOPE + MLA_ROPE] = np.concatenate([sin, sin], axis=-1)
    return jnp.asarray(c, F32), jnp.asarray(s, F32)


def _pool_bands():
    t = np.arange(POOL_TILE)[:, None]
    s = np.arange(POOL_TILE + 2 * POOL_HALO)[None, :]
    bands = [(s >= t + POOL_HALO - w // 2) & (s < t + POOL_HALO - w // 2 + w) for w in POOL_WINDOWS]
    return jnp.asarray(np.stack(bands).astype(np.float32), BF16)


def _block_diag(w_pool):
    depth, g, c, _ = w_pool.shape
    rows = []
    for i in range(g):
        blocks = [w_pool[:, i] if j == i else jnp.zeros((depth, c, c), w_pool.dtype) for j in range(g)]
        rows.append(jnp.concatenate(blocks, axis=2))
    return jnp.concatenate(rows, axis=1)


def kernel(x, c, ctx, c_ctx, w_ada, b_ada, norm_mix, w_in, q_norm, w_uq, kv_norm, w_ukv,
           ret_decay_logit, w_pool, pool_scale, w_out, norm_mlp, w_ff1, w_ff2, norm_final):
    b, n_lat, d = x.shape
    n_ctx = ctx.shape[1]
    depth = w_ada.shape[0]
    lat_tile = min(LAT_TILE, n_lat)
    assert n_lat % lat_tile == 0 and n_lat % ATTN_TILE == 0 and n_lat % GRID_W == 0
    assert n_ctx % RET_CHUNK == 0 and n_ctx % POOL_TILE == 0 and n_lat % POOL_TILE == 0
    assert d == 4 * RET_W and w_ff1.shape[2] % FF_CHUNK == 0

    in_cols, uq_cols, ukv_cols = _in_proj_columns(), _uq_columns(), _ukv_columns()
    ret_tabs = _ret_rope_tables(n_lat)
    mla_tabs = _mla_rope_tables(n_lat)

    rows = -(-(b + 1) // 8) * 8
    cc = jnp.zeros((rows, d), F32).at[:b].set(c).at[b].set(c_ctx)
    mod = _ada(cc, w_ada, b_ada).reshape(depth, rows, 6, d)
    lat_row, ctx_row = (lambda i: i), (lambda i: b)
    lg = jax.nn.log_sigmoid(ret_decay_logit.astype(F32))

    def rows3(a):
        return a.reshape(depth, 1, a.shape[-1])

    proj_w = (rows3(norm_mix), _take_cols(w_in, in_cols), rows3(q_norm), _take_cols(w_uq, uq_cols),
              rows3(kv_norm), _take_cols(w_ukv, ukv_cols))
    pool_w = (_pool_bands(), _block_diag(w_pool).astype(BF16), rows3(pool_scale))
    mlp_w = (w_out.astype(BF16), rows3(norm_mlp), w_ff1.astype(BF16), w_ff2.astype(BF16), norm_final[None])

    h_ctx = ctx
    for l in range(depth):
        last = l == depth - 1
        ret_l, u_l, q_l, k_l, vt_l = _inproj(x, mod, lat_row, l, *proj_w, mla_tabs, lat_tile)
        ret_c, u_c, q_c, k_c, vt_c = _inproj(h_ctx, mod, ctx_row, l, *proj_w, None, n_ctx)
        ret_o, pool_o = _ret_pool(lg, l, ret_l, ret_c, ret_tabs, (u_l,) if last else (u_l, u_c), *pool_w, not last)
        mla_l = _attention(q_l, k_c, vt_c, k_l, vt_l)
        if not last:
            mla_c = _attention(q_c, k_c, vt_c)
            h_ctx = _mix_mlp(h_ctx, ret_o[1], mla_c, pool_o[1], mod, ctx_row, l, *mlp_w, n_ctx, False)
        x = _mix_mlp(x, ret_o[0], mla_l, pool_o[0], mod, lat_row, l, *mlp_w, lat_tile, last)
    return x
```

```python
import functools
import math

import jax
import jax.numpy as jnp
import numpy as np
from jax import lax
from jax.experimental import pallas as pl
from jax.experimental.pallas import tpu as pltpu

GRID_W = 64
RET_HEADS = 4
RET_DIM = 64
RET_W = RET_HEADS * RET_DIM
RET_CHUNK = 128
MLA_HEADS = 8
MLA_NOPE = 64
MLA_ROPE = 32
MLA_V = 64
MLA_Q_RANK = 256
MLA_KV_RANK = 128
MLA_W = MLA_HEADS * MLA_V
POOL_GROUPS = 4
POOL_WINDOWS = (2, 4, 8, 16)
POOL_GDIM = 64
POOL_W = POOL_GROUPS * POOL_GDIM
ROPE_BASE = 10000.0
EPS = 1e-6

LANES = 128
LAT_TILE = 512
MLP_TILE = 1024
IN_SUB = 256
ATTN_TILE = 1024
Q_SUB = 256
KEY_CHUNK = 256
SCORE_LOOKAHEAD = 5
POOL_TILE = 256
HEAD_PAD = 128
VT_ROWS = MLA_V + 16
POOL_HALO = 128
FF_CHUNK = 1024
MLP_SUB = 256
VMEM_LIMIT = 56 * 1024 * 1024

COL_RET = 0
COL_CQ = 4 * RET_W
COL_CKV = COL_CQ + MLA_Q_RANK
COL_U = COL_CKV + MLA_KV_RANK
COL_KPE = COL_U + POOL_W
IN_COLS = COL_KPE + HEAD_PAD
UQ_ROT = MLA_HEADS * HEAD_PAD
UQ_COLS = UQ_ROT + MLA_HEADS * MLA_ROPE
UKV_V = MLA_HEADS * MLA_NOPE
UKV_COLS = UKV_V + MLA_HEADS * MLA_V

F32 = jnp.float32
BF16 = jnp.bfloat16


def _params(sem):
    return pltpu.CompilerParams(dimension_semantics=sem, vmem_limit_bytes=VMEM_LIMIT)


def _layer_spec(a, layer):
    tail = a.shape[1:]
    return pl.BlockSpec((1,) + tail, lambda *_: (layer,) + (0,) * len(tail), pipeline_mode=pl.Buffered(1))


def _dot(a, b):
    return jnp.dot(a, b, preferred_element_type=F32)


def _dot_nt(a, b):
    return lax.dot_general(a, b, (((1,), (1,)), ((), ())), preferred_element_type=F32)


def _rms(x, g):
    return x * lax.rsqrt(jnp.mean(x * x, axis=-1, keepdims=True) + EPS) * g


def _silu(x):
    return x * (1.0 / (1.0 + jnp.exp(-x)))


def _ada_kernel(c_ref, w_ref, b_ref, o_ref):
    s = _silu(c_ref[...]).astype(BF16)
    o_ref[0] = _dot(s, w_ref[0].astype(BF16)) + b_ref[0]


def _ada(cc, w_ada, b_ada):
    depth, d, n = w_ada.shape
    rows = cc.shape[0]
    tn = n // 4
    return pl.pallas_call(
        _ada_kernel,
        grid=(depth, n // tn),
        in_specs=[
            pl.BlockSpec((rows, d), lambda l, j: (0, 0)),
            pl.BlockSpec((1, d, tn), lambda l, j: (l, 0, j)),
            pl.BlockSpec((1, 1, tn), lambda l, j: (l, 0, j)),
        ],
        out_specs=pl.BlockSpec((1, rows, tn), lambda l, j: (l, 0, j)),
        out_shape=jax.ShapeDtypeStruct((depth, rows, n), F32),
        compiler_params=_params(("parallel", "parallel")),
        name="ada_mod",
    )(cc, w_ada, b_ada.reshape(depth, 1, n))


def _inproj_kernel(*refs, rope):
    if rope:
        (x_ref, mod_ref, g_ref, w_ref, gq_ref, wq_ref, gkv_ref, wkv_ref, c_ref, s_ref,
         ret_ref, u_ref, q_ref, k_ref, vt_ref) = refs
    else:
        (x_ref, mod_ref, g_ref, w_ref, gq_ref, wq_ref, gkv_ref, wkv_ref,
         ret_ref, u_ref, q_ref, k_ref, vt_ref) = refs
    nh = MLA_HEADS
    lane = lax.broadcasted_iota(jnp.int32, (1, HEAD_PAD), 1)
    nope_lanes = lane < MLA_NOPE
    rope_lanes = jnp.logical_and(lane >= MLA_NOPE, lane < MLA_NOPE + MLA_ROPE)
    sub = min(IN_SUB, x_ref.shape[1])
    den_rows = (lax.broadcasted_iota(jnp.int32, (VT_ROWS - MLA_V, sub), 0) == 0).astype(vt_ref.dtype)
    q_scale = (MLA_NOPE + MLA_ROPE) ** -0.5 * math.log2(math.e)

    def project(rows):
        h = _rms(x_ref[0, rows, :], g_ref[0]) * (1.0 + mod_ref[0, 0, 1:2, :]) + mod_ref[0, 0, 0:1, :]
        p = _dot(h.astype(BF16), w_ref[0])
        ret_ref[0, rows, :] = p[:, COL_RET:COL_CQ].astype(ret_ref.dtype)
        u_ref[0, rows, :] = p[:, COL_U:COL_KPE].astype(u_ref.dtype)
        return p

    def up_project(p, rows):
        yq = _rms(p[:, COL_CQ:COL_CKV], gq_ref[0]).astype(BF16)
        if rope:
            cos, sin = c_ref[rows, :], s_ref[rows, :]
            q2 = _dot(yq, wq_ref[0])
        else:
            q2 = _dot(yq, wq_ref[0, :, 0:UQ_ROT])
        ykv = _rms(p[:, COL_CKV:COL_U], gkv_ref[0]).astype(BF16)
        kv = _dot(ykv, wkv_ref[0])
        kx = p[:, COL_KPE:IN_COLS]
        kpe = pltpu.roll(kx, MLA_NOPE, 1)
        if rope:
            kpe = kpe * cos + pltpu.roll(kx, MLA_NOPE - MLA_ROPE, 1) * sin
        kpe = jnp.where(rope_lanes, kpe, 0.0)
        for hh in range(nh):
            qh = q2[:, hh * HEAD_PAD:(hh + 1) * HEAD_PAD]
            if rope:
                blk, j = divmod(hh * MLA_ROPE, HEAD_PAD)
                rot = q2[:, UQ_ROT + blk * HEAD_PAD:UQ_ROT + (blk + 1) * HEAD_PAD]
                shift = (MLA_NOPE - j) % HEAD_PAD
                qh = qh * cos + (pltpu.roll(rot, shift, 1) if shift else rot) * sin
            q_ref[0, hh, rows, :] = (qh * q_scale).astype(q_ref.dtype)
            blk, j = divmod(hh * MLA_NOPE, HEAD_PAD)
            kn = kv[:, blk * HEAD_PAD:(blk + 1) * HEAD_PAD]
            if j:
                kn = pltpu.roll(kn, HEAD_PAD - j, 1)
            k_ref[0, hh, rows, :] = jnp.where(nope_lanes, kn, kpe).astype(k_ref.dtype)
        for blk in range(nh * MLA_V // HEAD_PAD):
            vt = kv[:, UKV_V + blk * HEAD_PAD:UKV_V + (blk + 1) * HEAD_PAD].T.astype(vt_ref.dtype)
            for j in range(HEAD_PAD // MLA_V):
                hh = blk * (HEAD_PAD // MLA_V) + j
                vt_ref[0, hh, 0:MLA_V, rows] = vt[j * MLA_V:(j + 1) * MLA_V, :]
                vt_ref[0, hh, MLA_V:VT_ROWS, rows] = den_rows

    subs = [slice(r0, r0 + sub) for r0 in range(0, x_ref.shape[1], sub)]
    p_next = project(subs[0])
    for i, rows in enumerate(subs):
        p = p_next
        if i + 1 < len(subs):
            p_next = project(subs[i + 1])
        up_project(p, rows)


def _inproj(x, mod, mod_row, layer, g, w, gq, wq, gkv, wkv, tabs, tile):
    b, r, d = x.shape
    rope = tabs is not None
    nh = MLA_HEADS
    row = lambda width: pl.BlockSpec((1, tile, width), lambda i, j: (i, j, 0))
    head = pl.BlockSpec((1, nh, tile, HEAD_PAD), lambda i, j: (i, 0, j, 0))
    head_t = pl.BlockSpec((1, nh, VT_ROWS, tile), lambda i, j: (i, 0, 0, j))
    assert tile % min(IN_SUB, tile) == 0
    in_specs = [
        row(d),
        pl.BlockSpec((1, 1) + mod.shape[2:], lambda i, j: (layer, mod_row(i), 0, 0)),
    ] + [_layer_spec(a, layer) for a in (g, w, gq, wq, gkv, wkv)]
    args = [x, mod, g, w, gq, wq, gkv, wkv]
    if rope:
        in_specs += [pl.BlockSpec((tile, HEAD_PAD), lambda i, j: (j, 0))] * 2
        args += list(tabs)
    hshape = jax.ShapeDtypeStruct((b, nh, r, HEAD_PAD), BF16)
    return pl.pallas_call(
        functools.partial(_inproj_kernel, rope=rope),
        grid=(b, r // tile),
        in_specs=in_specs,
        out_specs=[row(4 * RET_W), row(POOL_W), head, head, head_t],
        out_shape=[jax.ShapeDtypeStruct((b, r, 4 * RET_W), BF16),
                   jax.ShapeDtypeStruct((b, r, POOL_W), BF16), hshape, hshape,
                   jax.ShapeDtypeStruct((b, nh, VT_ROWS, r), BF16)],
        compiler_params=_params(("parallel", "parallel")),
        name="in_proj",
    )(*args)


def _ret_pool_kernel(*refs, layer, n_lat, n_ctx, need_ctx):
    n_seg = 2 if need_ctx else 1
    it = iter(refs)
    lg_ref, rl_ref, rc_ref, c_ref, sa_ref, sb_ref = [next(it) for _ in range(6)]
    u_refs = [next(it) for _ in range(n_seg)]
    band_ref, w_ref, s_ref = next(it), next(it), next(it)
    ro_refs = [next(it) for _ in range(n_seg)]
    po_refs = [next(it) for _ in range(n_seg)]
    q_s, k_s, u_s, st_s, pad_s = it
    ch = RET_CHUNK
    pairs = RET_HEADS // 2
    n_all = n_lat + n_ctx
    lane = lax.broadcasted_iota(jnp.int32, (1, LANES), 1)
    lo = lane < RET_DIM
    sub_lo = lax.broadcasted_iota(jnp.int32, (LANES, 1), 0) < RET_DIM
    row = lax.broadcasted_iota(jnp.int32, (ch, 1), 0).astype(F32)
    col = lax.broadcasted_iota(jnp.int32, (1, ch), 1).astype(F32)
    diff = row - col
    head_of_row = lax.broadcasted_iota(jnp.int32, (LANES, 1), 0) // RET_DIM
    blockdiag = (head_of_row == lane // RET_DIM).astype(F32)
    k_scale = RET_DIM ** -0.5
    half = RET_DIM // 2

    def pair_tables(p):
        lf_a, lf_b = lg_ref[layer, 0, 2 * p], lg_ref[layer, 0, 2 * p + 1]
        lb_a, lb_b = lg_ref[layer, 1, 2 * p], lg_ref[layer, 1, 2 * p + 1]
        lf = jnp.where(lo, lf_a, lf_b)
        lb = jnp.where(lo, lb_a, lb_b)
        lf_t = jnp.where(sub_lo, lf_a, lf_b)
        lb_t = jnp.where(sub_lo, lb_a, lb_b)

        def decay(l_f, l_b):
            return jnp.where(diff >= 0, jnp.exp(l_f * jnp.maximum(diff, 0.0)),
                             jnp.exp(l_b * jnp.maximum(-diff, 0.0)))

        return dict(
            d_ab=jnp.concatenate([decay(lf_a, lb_a), decay(lf_b, lb_b)], axis=0) * k_scale,
            wq_fb=jnp.concatenate([jnp.exp(lf * (row + 1.0)), jnp.exp(lb * (ch - row))], axis=1),
            wk_fb_t=jnp.concatenate([jnp.exp(lf_t * (ch - 1.0 - col)), jnp.exp(lb_t * col)], axis=0) * k_scale,
            gf=jnp.exp(lf * ch), gb=jnp.exp(lb * ch))

    tables = [pair_tables(p) for p in range(pairs)]

    def source(n):
        if n < n_lat:
            return rl_ref, slice(n * ch, (n + 1) * ch), True
        return rc_ref, slice((n - n_lat) * ch, (n - n_lat + 1) * ch), False

    def part(p, which):
        c0 = (which * pairs + p) * LANES
        return slice(c0, c0 + LANES)

    lo_pool = lane < POOL_GDIM
    halo = POOL_HALO

    def pool_fill(u_ref, length):
        pad_s[0:halo, :] = jnp.zeros((halo, POOL_W), BF16)
        pad_s[halo:halo + length, :] = u_ref[0]
        pad_s[halo + length:2 * halo + length, :] = jnp.zeros((halo, POOL_W), BF16)

    def pool_slab(u_ref, o_ref, length, r0):
        t = (lax.broadcasted_iota(jnp.int32, (POOL_TILE, 1), 0) + r0).astype(F32)
        cols = []
        for hf in range(POOL_W // LANES):
            cs = slice(hf * LANES, (hf + 1) * LANES)
            tokens = pad_s[r0:r0 + POOL_TILE + 2 * halo, cs]

            def inv_count(w):
                if r0 - w // 2 >= 0 and r0 + POOL_TILE - w // 2 + w <= length:
                    return 1.0 / w
                cnt = (jnp.clip(t - w // 2 + w, 0.0, float(length))
                       - jnp.clip(t - w // 2, 0.0, float(length)))
                return 1.0 / cnt

            w_n, w_w = POOL_WINDOWS[2 * hf], POOL_WINDOWS[2 * hf + 1]
            pooled = jnp.where(lo_pool, _dot(band_ref[2 * hf], tokens) * inv_count(w_n),
                               _dot(band_ref[2 * hf + 1], tokens) * inv_count(w_w))
            cols.append(pooled - u_ref[0, r0:r0 + POOL_TILE, cs].astype(F32))
        pooled = jnp.concatenate(cols, axis=-1).astype(BF16)
        y = _dot(pooled, w_ref[0]) * s_ref[0]
        o_ref[0, r0:r0 + POOL_TILE, :] = y.astype(o_ref.dtype)

    pool_tasks = []
    for u_ref, o_ref in zip(u_refs, po_refs):
        length = u_ref.shape[1]
        pool_tasks.append(functools.partial(pool_fill, u_ref, length))
        pool_tasks += [functools.partial(pool_slab, u_ref, o_ref, length, r0)
                       for r0 in range(0, length, POOL_TILE)]
    pool_tasks.reverse()

    for n in range(n_all):
        src_ref, src_rows, roped = source(n)
        rows = slice(n * ch, (n + 1) * ch)
        for p in range(pairs):
            q = src_ref[0, src_rows, part(p, 0)].astype(F32)
            k = src_ref[0, src_rows, part(p, 1)].astype(F32)
            if roped:
                c, sa, sb = c_ref[src_rows, :], sa_ref[src_rows, :], sb_ref[src_rows, :]
                q = q * c + pltpu.roll(q, LANES - half, 1) * sa + pltpu.roll(q, half, 1) * sb
                k = k * c + pltpu.roll(k, LANES - half, 1) * sa + pltpu.roll(k, half, 1) * sb
            q_s[p, rows, :] = q
            k_s[p, rows, :] = k
            kt = k.T
            kt2 = jnp.concatenate([kt, kt], axis=0) * tables[p]["wk_fb_t"]
            u_s[p, n] = _dot(kt2.astype(BF16), src_ref[0, src_rows, part(p, 2)])

    order_f = list(range(n_lat, n_all)) + list(range(n_lat))
    order_b = list(range(n_all - 1, -1, -1))
    for p in range(pairs):
        s = jnp.zeros((LANES, LANES), F32)
        for n in order_f:
            st_s[p, n, 0:LANES, :] = (s * blockdiag).astype(BF16)
            s = tables[p]["gf"] * s + u_s[p, n, 0:LANES, :]
        s = jnp.zeros((LANES, LANES), F32)
        for n in order_b:
            st_s[p, n, LANES:2 * LANES, :] = (s * blockdiag).astype(BF16)
            s = tables[p]["gb"] * s + u_s[p, n, LANES:2 * LANES, :]

    n_out = n_all if need_ctx else n_lat

    def scores_and_state(n):
        rows = slice(n * ch, (n + 1) * ch)
        out = []
        for p in range(pairs):
            q = q_s[p, rows, :]
            kb = k_s[p, rows, :].astype(BF16)
            q_ab = jnp.concatenate([jnp.where(lo, q, 0.0), jnp.where(lo, 0.0, q)], axis=0).astype(BF16)
            a_ab = _dot_nt(q_ab, kb)
            carried = _dot((jnp.concatenate([q, q], axis=1) * tables[p]["wq_fb"]).astype(BF16), st_s[p, n])
            out.append((a_ab, carried))
        return out

    def finish(n, staged):
        src_ref, src_rows, _ = source(n)
        dst_ref = ro_refs[0] if n < n_lat else ro_refs[1]
        for p, (a_ab, carried) in enumerate(staged):
            vb = src_ref[0, src_rows, part(p, 2)]
            gate = src_ref[0, src_rows, part(p, 3)].astype(F32)
            o_ab = _dot((a_ab * tables[p]["d_ab"]).astype(BF16), vb)
            o = jnp.where(lo, o_ab[0:ch], o_ab[ch:2 * ch]) + carried
            inv = 1.0 / RET_DIM
            mu = jnp.where(lo, jnp.sum(jnp.where(lo, o, 0.0), axis=-1, keepdims=True),
                           jnp.sum(jnp.where(lo, 0.0, o), axis=-1, keepdims=True)) * inv
            dlt = o - mu
            sq = dlt * dlt
            var = jnp.where(lo, jnp.sum(jnp.where(lo, sq, 0.0), axis=-1, keepdims=True),
                            jnp.sum(jnp.where(lo, 0.0, sq), axis=-1, keepdims=True)) * inv
            dst_ref[0, src_rows, p * LANES:(p + 1) * LANES] = (
                _silu(gate) * (dlt * lax.rsqrt(var + EPS))).astype(dst_ref.dtype)

    staged = scores_and_state(0)
    for n in range(n_out):
        staged_next = scores_and_state(n + 1) if n + 1 < n_out else None
        if pool_tasks:
            pool_tasks.pop()()
        finish(n, staged)
        staged = staged_next
    while pool_tasks:
        pool_tasks.pop()()


def _ret_pool(lg, layer, ret_lat, ret_ctx, tabs, us, band, w_bd, scale, need_ctx):
    b, rl, _ = ret_lat.shape
    rc = ret_ctx.shape[1]
    ch = RET_CHUNK
    n_lat, n_ctx = rl // ch, rc // ch
    n_all = n_lat + n_ctx
    pairs = RET_HEADS // 2
    lengths = tuple(u.shape[1] for u in us)
    assert len(us) == (2 if need_ctx else 1)
    kern = functools.partial(_ret_pool_kernel, layer=layer, n_lat=n_lat, n_ctx=n_ctx, need_ctx=need_ctx)
    tab_spec = pl.BlockSpec((rl, LANES), lambda i: (0, 0), pipeline_mode=pl.Buffered(1))
    rows = lambda n, width: pl.BlockSpec((1, n, width), lambda i: (i, 0, 0))
    out_specs = [rows(n, RET_W) for n in lengths] + [rows(n, POOL_W) for n in lengths]
    out_shape = ([jax.ShapeDtypeStruct((b, n, RET_W), BF16) for n in lengths]
                 + [jax.ShapeDtypeStruct((b, n, POOL_W), BF16) for n in lengths])
    outs = pl.pallas_call(
        kern,
        grid=(b,),
        in_specs=[
            pl.BlockSpec(memory_space=pltpu.SMEM),
            rows(rl, 4 * RET_W), rows(rc, 4 * RET_W),
            tab_spec, tab_spec, tab_spec,
            *[rows(n, POOL_W) for n in lengths],
            pl.BlockSpec(band.shape, lambda i: (0, 0, 0), pipeline_mode=pl.Buffered(1)),
            _layer_spec(w_bd, layer), _layer_spec(scale, layer),
        ],
        out_specs=out_specs,
        out_shape=out_shape,
        scratch_shapes=[
            pltpu.VMEM((pairs, rl + rc, LANES), F32), pltpu.VMEM((pairs, rl + rc, LANES), F32),
            pltpu.VMEM((pairs, n_all, 2 * LANES, LANES), F32),
            pltpu.VMEM((pairs, n_all, 2 * LANES, LANES), BF16),
            pltpu.VMEM((max(lengths) + 2 * POOL_HALO, POOL_W), BF16),
        ],
        compiler_params=_params(("parallel",)),
        name="retention_pool",
    )(lg, ret_lat, ret_ctx, *tabs, *us, band, w_bd, scale)
    n = len(lengths)
    return outs[:n], outs[n:]


def _attn_kernel(*refs, use_lat):
    if use_lat:
        q_ref, kl_ref, vtl_ref, kc_ref, vtc_ref, o_ref = refs
    else:
        q_ref, kc_ref, vtc_ref, o_ref = refs

    sources = [(kc_ref, vtc_ref)]
    if use_lat:
        sources.append((kl_ref, vtl_ref))

    n_sub = q_ref.shape[2] // Q_SUB
    items = [((qs, h), k_ref, vt_ref, c0) for qs in range(n_sub) for h in range(MLA_HEADS)
             for k_ref, vt_ref in sources for c0 in range(0, k_ref.shape[2], KEY_CHUNK)]
    last_item = {g: i for i, (g, _, _, _) in enumerate(items)}

    scores, m_run, acc, done = {}, {}, {}, {}
    for t in range(len(items) + SCORE_LOOKAHEAD):
        if t < len(items):
            (qs, h), k_ref, _, c0 = items[t]
            q = q_ref[0, h, qs * Q_SUB:(qs + 1) * Q_SUB, :]
            scores[t] = _dot_nt(k_ref[0, h, c0:c0 + KEY_CHUNK, :], q)
        i = t - SCORE_LOOKAHEAD
        if i < 0:
            continue
        g, _, vt_ref, c0 = items[i]
        qs, h = g
        s = scores.pop(i)
        m_new = jnp.max(s, axis=0, keepdims=True)
        if g in m_run:
            m_new = jnp.maximum(m_run[g], m_new)
        pv = _dot(vt_ref[0, h, :, c0:c0 + KEY_CHUNK], jnp.exp2(s - m_new).astype(BF16))
        acc[g] = pv if g not in acc else acc[g] * jnp.exp2(m_run[g] - m_new) + pv
        m_run[g] = m_new
        if i == last_item[g]:
            a = acc.pop(g)
            done[g] = a[0:MLA_V, :] * (1.0 / a[MLA_V:MLA_V + 1, :])
            if h % 2 == 1:
                pair_t = jnp.concatenate([done.pop((qs, h - 1)), done.pop(g)], axis=0)
                o_ref[0, qs * Q_SUB:(qs + 1) * Q_SUB, (h // 2) * HEAD_PAD:(h // 2 + 1) * HEAD_PAD] = (
                    pair_t.T.astype(o_ref.dtype))


def _attention(q, k_ctx, vt_ctx, k_lat=None, vt_lat=None):
    b, nh, rows, _ = q.shape
    use_lat = k_lat is not None
    tile = min(ATTN_TILE, rows)

    def whole(a):
        return pl.BlockSpec((1,) + a.shape[1:], lambda i, j: (i, 0, 0, 0))

    in_specs = [pl.BlockSpec((1, nh, tile, HEAD_PAD), lambda i, j: (i, 0, j, 0))]
    args = [q]
    if use_lat:
        in_specs += [whole(k_lat), whole(vt_lat)]
        args += [k_lat, vt_lat]
    in_specs += [whole(k_ctx), whole(vt_ctx)]
    args += [k_ctx, vt_ctx]
    return pl.pallas_call(
        functools.partial(_attn_kernel, use_lat=use_lat),
        grid=(b, rows // tile),
        in_specs=in_specs,
        out_specs=pl.BlockSpec((1, tile, MLA_W), lambda i, j: (i, j, 0)),
        out_shape=jax.ShapeDtypeStruct((b, rows, MLA_W), BF16),
        compiler_params=_params(("parallel", "arbitrary")),
        name="mla_attention",
    )(*args)


def _mix_mlp_kernel(x_ref, r_ref, m_ref, p_ref, wo_ref, mod_ref, g_ref, w1_ref, w2_ref, gf_ref, o_ref,
                    *, final_norm):
    def mix(rows):
        y = (_dot(r_ref[0, rows, :], wo_ref[0, 0:RET_W, :])
             + _dot(m_ref[0, rows, :], wo_ref[0, RET_W:RET_W + MLA_W, :])
             + _dot(p_ref[0, rows, :], wo_ref[0, RET_W + MLA_W:, :]))
        x = x_ref[0, rows, :] + mod_ref[0, 0, 2:3, :] * y
        h = (_rms(x, g_ref[0]) * (1.0 + mod_ref[0, 0, 4:5, :]) + mod_ref[0, 0, 3:4, :]).astype(BF16)
        return x, h

    def mlp(x, h, rows):
        acc = None
        for c0 in range(0, w1_ref.shape[2], FF_CHUNK):
            a = jnp.maximum(_dot(h, w1_ref[0, :, c0:c0 + FF_CHUNK]), 0.0)
            part = _dot((a * a).astype(BF16), w2_ref[0, c0:c0 + FF_CHUNK, :])
            acc = part if acc is None else acc + part
        y = x + mod_ref[0, 0, 5:6, :] * acc
        if final_norm:
            y = _rms(y, gf_ref[...])
        o_ref[0, rows, :] = y

    sub = min(MLP_SUB, x_ref.shape[1])
    subs = [slice(r0, r0 + sub) for r0 in range(0, x_ref.shape[1], sub)]
    staged = mix(subs[0])
    for i, rows in enumerate(subs):
        x, h = staged
        if i + 1 < len(subs):
            staged = mix(subs[i + 1])
        mlp(x, h, rows)


def _mix_mlp(x, ret, mla, pool, mod, mod_row, layer, w_out, g, w1, w2, g_final, tile, final_norm):
    b, r, d = x.shape
    row = lambda width: pl.BlockSpec((1, tile, width), lambda i, j: (i, j, 0))
    return pl.pallas_call(
        functools.partial(_mix_mlp_kernel, final_norm=final_norm),
        grid=(b, r // tile),
        in_specs=[
            row(d), row(RET_W), row(MLA_W), row(POOL_W),
            _layer_spec(w_out, layer),
            pl.BlockSpec((1, 1) + mod.shape[2:], lambda i, j: (layer, mod_row(i), 0, 0)),
            _layer_spec(g, layer), _layer_spec(w1, layer), _layer_spec(w2, layer),
            pl.BlockSpec(g_final.shape, lambda i, j: (0, 0)),
        ],
        out_specs=row(d),
        out_shape=jax.ShapeDtypeStruct((b, r, d), F32),
        compiler_params=_params(("parallel", "parallel")),
        name="mix_mlp",
    )(x, ret, mla, pool, w_out, mod, g, w1, w2, g_final)


def _in_proj_columns():
    zero = 4 * RET_W + MLA_Q_RANK + MLA_KV_RANK + MLA_ROPE + POOL_W
    src = np.full((IN_COLS,), zero, np.int32)
    sign = np.ones((IN_COLS,), np.float32)
    kpe0 = COL_U
    src[0:kpe0] = np.arange(kpe0)
    half = MLA_ROPE // 2
    src[COL_KPE:COL_KPE + MLA_ROPE] = kpe0 + np.arange(MLA_ROPE)
    rot0 = COL_KPE + MLA_ROPE
    src[rot0:rot0 + half] = kpe0 + half + np.arange(half)
    sign[rot0:rot0 + half] = -1.0
    src[rot0 + half:rot0 + MLA_ROPE] = kpe0 + np.arange(half)
    src[COL_U:COL_U + POOL_W] = kpe0 + MLA_ROPE + np.arange(POOL_W)
    return src, sign


def _uq_columns():
    per = MLA_NOPE + MLA_ROPE
    half = MLA_ROPE // 2
    src = np.full((UQ_COLS,), MLA_HEADS * per, np.int32)
    sign = np.ones((UQ_COLS,), np.float32)
    for h in range(MLA_HEADS):
        src[h * HEAD_PAD:h * HEAD_PAD + per] = h * per + np.arange(per)
        r0 = UQ_ROT + h * MLA_ROPE
        src[r0:r0 + half] = h * per + MLA_NOPE + half + np.arange(half)
        sign[r0:r0 + half] = -1.0
        src[r0 + half:r0 + MLA_ROPE] = h * per + MLA_NOPE + np.arange(half)
    return src, sign


def _ukv_columns():
    per = MLA_NOPE + MLA_V
    src = np.zeros((UKV_COLS,), np.int32)
    for h in range(MLA_HEADS):
        src[h * MLA_NOPE:(h + 1) * MLA_NOPE] = h * per + np.arange(MLA_NOPE)
        src[UKV_V + h * MLA_V:UKV_V + (h + 1) * MLA_V] = h * per + MLA_NOPE + np.arange(MLA_V)
    return src, np.ones(src.shape, np.float32)


def _take_cols(w, cols):
    src, sign = cols
    zero, n = w.shape[-1], len(src)
    wb = w.astype(BF16)
    pieces, i = [], 0
    while i < n:
        j = i + 1
        if src[i] == zero:
            while j < n and src[j] == zero:
                j += 1
            pieces.append(jnp.zeros(w.shape[:-1] + (j - i,), BF16))
        else:
            while j < n and src[j] != zero and src[j] == src[j - 1] + 1 and sign[j] == sign[i]:
                j += 1
            piece = wb[..., src[i]:src[i] + j - i]
            pieces.append(-piece if sign[i] < 0 else piece)
        i = j
    return jnp.concatenate(pieces, axis=-1)


def _rope_angles(n_rows, dim):
    pos = np.arange(n_rows)
    n_freq = dim // 4
    inv = np.float32(ROPE_BASE) ** (-np.arange(n_freq, dtype=np.float32) / np.float32(n_freq))
    return np.concatenate([(pos // GRID_W).astype(np.float32)[:, None] * inv,
                           (pos % GRID_W).astype(np.float32)[:, None] * inv], axis=-1)


def _ret_rope_tables(n_rows):
    ang = _rope_angles(n_rows, RET_DIM)
    cos, sin = np.cos(ang), np.sin(ang)
    zero = np.zeros_like(sin)
    reps = LANES // RET_DIM
    c = np.tile(np.concatenate([cos, cos], axis=-1), (1, reps))
    sa = np.tile(np.concatenate([-sin, zero], axis=-1), (1, reps))
    sb = np.tile(np.concatenate([zero, sin], axis=-1), (1, reps))
    return tuple(jnp.asarray(a, F32) for a in (c, sa, sb))


def _mla_rope_tables(n_rows):
    ang = _rope_angles(n_rows, MLA_ROPE)
    cos, sin = np.cos(ang), np.sin(ang)
    c = np.ones((n_rows, HEAD_PAD), np.float32)
    s = np.zeros((n_rows, HEAD_PAD), np.float32)
    c[:, MLA_NOPE:MLA_NOPE + MLA_ROPE] = np.concatenate([cos, cos], axis=-1)
    s[:, MLA_NOPE:MLA_NOPE + MLA_ROPE] = np.concatenate([sin, sin], axis=-1)
    return jnp.asarray(c, F32), jnp.asarray(s, F32)


def _pool_bands():
    t = np.arange(POOL_TILE)[:, None]
    s = np.arange(POOL_TILE + 2 * POOL_HALO)[None, :]
    bands = [(s >= t + POOL_HALO - w // 2) & (s < t + POOL_HALO - w // 2 + w) for w in POOL_WINDOWS]
    return jnp.asarray(np.stack(bands).astype(np.float32), BF16)


def _block_diag(w_pool):
    depth, g, c, _ = w_pool.shape
    rows = []
    for i in range(g):
        blocks = [w_pool[:, i] if j == i else jnp.zeros((depth, c, c), w_pool.dtype) for j in range(g)]
        rows.append(jnp.concatenate(blocks, axis=2))
    return jnp.concatenate(rows, axis=1)


def kernel(x, c, ctx, c_ctx, w_ada, b_ada, norm_mix, w_in, q_norm, w_uq, kv_norm, w_ukv,
           ret_decay_logit, w_pool, pool_scale, w_out, norm_mlp, w_ff1, w_ff2, norm_final):
    b, n_lat, d = x.shape
    n_ctx = ctx.shape[1]
    depth = w_ada.shape[0]
    lat_tile, mlp_tile = min(LAT_TILE, n_lat), min(MLP_TILE, n_lat)
    assert n_lat % lat_tile == 0 and n_lat % mlp_tile == 0 and n_lat % min(ATTN_TILE, n_lat) == 0
    assert n_lat % GRID_W == 0 and n_lat % Q_SUB == 0
    assert n_ctx % RET_CHUNK == 0 and n_ctx % POOL_TILE == 0 and n_lat % POOL_TILE == 0
    assert d == 4 * RET_W and w_ff1.shape[2] % FF_CHUNK == 0

    in_cols, uq_cols, ukv_cols = _in_proj_columns(), _uq_columns(), _ukv_columns()
    ret_tabs = _ret_rope_tables(n_lat)
    mla_tabs = _mla_rope_tables(n_lat)

    rows = -(-(b + 1) // 8) * 8
    cc = jnp.zeros((rows, d), F32).at[:b].set(c).at[b].set(c_ctx)
    mod = _ada(cc, w_ada, b_ada).reshape(depth, rows, 6, d)
    lat_row, ctx_row = (lambda i: i), (lambda i: b)
    lg = jax.nn.log_sigmoid(ret_decay_logit.astype(F32))

    def rows3(a):
        return a.reshape(depth, 1, a.shape[-1])

    proj_w = (rows3(norm_mix), _take_cols(w_in, in_cols), rows3(q_norm), _take_cols(w_uq, uq_cols),
              rows3(kv_norm), _take_cols(w_ukv, ukv_cols))
    pool_w = (_pool_bands(), _block_diag(w_pool).astype(BF16), rows3(pool_scale))
    mlp_w = (w_out.astype(BF16), rows3(norm_mlp), w_ff1.astype(BF16), w_ff2.astype(BF16), norm_final[None])

    h_ctx = ctx
    for l in range(depth):
        last = l == depth - 1
        ret_l, u_l, q_l, k_l, vt_l = _inproj(x, mod, lat_row, l, *proj_w, mla_tabs, lat_tile)
        ret_c, u_c, q_c, k_c, vt_c = _inproj(h_ctx, mod, ctx_row, l, *proj_w, None, n_ctx)
        ret_o, pool_o = _ret_pool(lg, l, ret_l, ret_c, ret_tabs, (u_l,) if last else (u_l, u_c), *pool_w, not last)
        mla_l = _attention(q_l, k_c, vt_c, k_l, vt_l)
        if not last:
            mla_c = _attention(q_c, k_c, vt_c)
            h_ctx = _mix_mlp(h_ctx, ret_o[1], mla_c, pool_o[1], mod, ctx_row, l, *mlp_w, n_ctx, False)
        x = _mix_mlp(x, ret_o[0], mla_l, pool_o[0], mod, lat_row, l, *mlp_w, mlp_tile, last)
    return x
```

```python
import functools
import math

import jax
import jax.numpy as jnp
import numpy as np
from jax import lax
from jax.experimental import pallas as pl
from jax.experimental.pallas import tpu as pltpu

GRID_W = 64
RET_HEADS = 4
RET_DIM = 64
RET_W = RET_HEADS * RET_DIM
RET_CHUNK = 128
MLA_HEADS = 8
MLA_NOPE = 64
MLA_ROPE = 32
MLA_V = 64
MLA_Q_RANK = 256
MLA_KV_RANK = 128
MLA_W = MLA_HEADS * MLA_V
POOL_GROUPS = 4
POOL_WINDOWS = (2, 4, 8, 16)
POOL_GDIM = 64
POOL_W = POOL_GROUPS * POOL_GDIM
ROPE_BASE = 10000.0
EPS = 1e-6

LANES = 128
LAT_TILE = 512
MLP_TILE = 1024
IN_SUB = 256
ATTN_TILE = 512
Q_SUB = 256
KEY_CHUNK = 256
SCORE_LOOKAHEAD = 5
POOL_TILE = 256
HEAD_PAD = 128
VT_ROWS = MLA_V + 16
POOL_HALO = 128
FF_CHUNK = 1024
MLP_SUB = 256
VMEM_LIMIT = 56 * 1024 * 1024

COL_RET = 0
COL_CQ = 4 * RET_W
COL_CKV = COL_CQ + MLA_Q_RANK
COL_U = COL_CKV + MLA_KV_RANK
COL_KPE = COL_U + POOL_W
IN_COLS = COL_KPE + HEAD_PAD
UQ_ROT = MLA_HEADS * HEAD_PAD
UQ_COLS = UQ_ROT + MLA_HEADS * MLA_ROPE
UKV_V = MLA_HEADS * MLA_NOPE
UKV_COLS = UKV_V + MLA_HEADS * MLA_V

F32 = jnp.float32
BF16 = jnp.bfloat16


def _params(sem):
    return pltpu.CompilerParams(dimension_semantics=sem, vmem_limit_bytes=VMEM_LIMIT)


def _layer_spec(a, layer):
    tail = a.shape[1:]
    return pl.BlockSpec((1,) + tail, lambda *_: (layer,) + (0,) * len(tail), pipeline_mode=pl.Buffered(1))


def _dot(a, b):
    return jnp.dot(a, b, preferred_element_type=F32)


def _dot_nt(a, b):
    return lax.dot_general(a, b, (((1,), (1,)), ((), ())), preferred_element_type=F32)


def _rms(x, g):
    return x * lax.rsqrt(jnp.mean(x * x, axis=-1, keepdims=True) + EPS) * g


def _silu(x):
    return x * (1.0 / (1.0 + jnp.exp(-x)))


def _ada_kernel(c_ref, w_ref, b_ref, o_ref):
    s = _silu(c_ref[...]).astype(BF16)
    o_ref[0] = _dot(s, w_ref[0].astype(BF16)) + b_ref[0]


def _ada(cc, w_ada, b_ada):
    depth, d, n = w_ada.shape
    rows = cc.shape[0]
    tn = n // 4
    return pl.pallas_call(
        _ada_kernel,
        grid=(depth, n // tn),
        in_specs=[
            pl.BlockSpec((rows, d), lambda l, j: (0, 0)),
            pl.BlockSpec((1, d, tn), lambda l, j: (l, 0, j)),
            pl.BlockSpec((1, 1, tn), lambda l, j: (l, 0, j)),
        ],
        out_specs=pl.BlockSpec((1, rows, tn), lambda l, j: (l, 0, j)),
        out_shape=jax.ShapeDtypeStruct((depth, rows, n), F32),
        compiler_params=_params(("parallel", "parallel")),
        name="ada_mod",
    )(cc, w_ada, b_ada.reshape(depth, 1, n))


def _inproj_kernel(*refs, rope):
    if rope:
        (x_ref, mod_ref, g_ref, w_ref, gq_ref, wq_ref, gkv_ref, wkv_ref, c_ref, s_ref,
         ret_ref, u_ref, q_ref, k_ref, vt_ref) = refs
    else:
        (x_ref, mod_ref, g_ref, w_ref, gq_ref, wq_ref, gkv_ref, wkv_ref,
         ret_ref, u_ref, q_ref, k_ref, vt_ref) = refs
    nh = MLA_HEADS
    lane = lax.broadcasted_iota(jnp.int32, (1, HEAD_PAD), 1)
    nope_lanes = lane < MLA_NOPE
    rope_lanes = jnp.logical_and(lane >= MLA_NOPE, lane < MLA_NOPE + MLA_ROPE)
    sub = min(IN_SUB, x_ref.shape[1])
    den_rows = (lax.broadcasted_iota(jnp.int32, (VT_ROWS - MLA_V, sub), 0) == 0).astype(vt_ref.dtype)
    q_scale = (MLA_NOPE + MLA_ROPE) ** -0.5 * math.log2(math.e)

    def project(rows):
        h = _rms(x_ref[0, rows, :], g_ref[0]) * (1.0 + mod_ref[0, 0, 1:2, :]) + mod_ref[0, 0, 0:1, :]
        p = _dot(h.astype(BF16), w_ref[0])
        ret_ref[0, rows, :] = p[:, COL_RET:COL_CQ].astype(ret_ref.dtype)
        u_ref[0, rows, :] = p[:, COL_U:COL_KPE].astype(u_ref.dtype)
        return p

    def up_project(p, rows):
        yq = _rms(p[:, COL_CQ:COL_CKV], gq_ref[0]).astype(BF16)
        if rope:
            cos, sin = c_ref[rows, :], s_ref[rows, :]
            q2 = _dot(yq, wq_ref[0])
        else:
            q2 = _dot(yq, wq_ref[0, :, 0:UQ_ROT])
        ykv = _rms(p[:, COL_CKV:COL_U], gkv_ref[0]).astype(BF16)
        kv = _dot(ykv, wkv_ref[0])
        kx = p[:, COL_KPE:IN_COLS]
        kpe = pltpu.roll(kx, MLA_NOPE, 1)
        if rope:
            kpe = kpe * cos + pltpu.roll(kx, MLA_NOPE - MLA_ROPE, 1) * sin
        kpe = jnp.where(rope_lanes, kpe, 0.0)
        for hh in range(nh):
            qh = q2[:, hh * HEAD_PAD:(hh + 1) * HEAD_PAD]
            if rope:
                blk, j = divmod(hh * MLA_ROPE, HEAD_PAD)
                rot = q2[:, UQ_ROT + blk * HEAD_PAD:UQ_ROT + (blk + 1) * HEAD_PAD]
                shift = (MLA_NOPE - j) % HEAD_PAD
                qh = qh * cos + (pltpu.roll(rot, shift, 1) if shift else rot) * sin
            q_ref[0, hh, rows, :] = (qh * q_scale).astype(q_ref.dtype)
            blk, j = divmod(hh * MLA_NOPE, HEAD_PAD)
            kn = kv[:, blk * HEAD_PAD:(blk + 1) * HEAD_PAD]
            if j:
                kn = pltpu.roll(kn, HEAD_PAD - j, 1)
            k_ref[0, hh, rows, :] = jnp.where(nope_lanes, kn, kpe).astype(k_ref.dtype)
        for blk in range(nh * MLA_V // HEAD_PAD):
            vt = kv[:, UKV_V + blk * HEAD_PAD:UKV_V + (blk + 1) * HEAD_PAD].T.astype(vt_ref.dtype)
            for j in range(HEAD_PAD // MLA_V):
                hh = blk * (HEAD_PAD // MLA_V) + j
                vt_ref[0, hh, 0:MLA_V, rows] = vt[j * MLA_V:(j + 1) * MLA_V, :]
                vt_ref[0, hh, MLA_V:VT_ROWS, rows] = den_rows

    subs = [slice(r0, r0 + sub) for r0 in range(0, x_ref.shape[1], sub)]
    p_next = project(subs[0])
    for i, rows in enumerate(subs):
        p = p_next
        if i + 1 < len(subs):
            p_next = project(subs[i + 1])
        up_project(p, rows)


def _inproj(x, mod, mod_row, layer, g, w, gq, wq, gkv, wkv, tabs, tile):
    b, r, d = x.shape
    rope = tabs is not None
    nh = MLA_HEADS
    row = lambda width: pl.BlockSpec((1, tile, width), lambda i, j: (i, j, 0))
    head = pl.BlockSpec((1, nh, tile, HEAD_PAD), lambda i, j: (i, 0, j, 0))
    head_t = pl.BlockSpec((1, nh, VT_ROWS, tile), lambda i, j: (i, 0, 0, j))
    assert tile % min(IN_SUB, tile) == 0
    in_specs = [
        row(d),
        pl.BlockSpec((1, 1) + mod.shape[2:], lambda i, j: (layer, mod_row(i), 0, 0)),
    ] + [_layer_spec(a, layer) for a in (g, w, gq, wq, gkv, wkv)]
    args = [x, mod, g, w, gq, wq, gkv, wkv]
    if rope:
        in_specs += [pl.BlockSpec((tile, HEAD_PAD), lambda i, j: (j, 0))] * 2
        args += list(tabs)
    hshape = jax.ShapeDtypeStruct((b, nh, r, HEAD_PAD), BF16)
    return pl.pallas_call(
        functools.partial(_inproj_kernel, rope=rope),
        grid=(b, r // tile),
        in_specs=in_specs,
        out_specs=[row(4 * RET_W), row(POOL_W), head, head, head_t],
        out_shape=[jax.ShapeDtypeStruct((b, r, 4 * RET_W), BF16),
                   jax.ShapeDtypeStruct((b, r, POOL_W), BF16), hshape, hshape,
                   jax.ShapeDtypeStruct((b, nh, VT_ROWS, r), BF16)],
        compiler_params=_params(("parallel", "parallel")),
        name="in_proj",
    )(*args)


def _ret_pool_kernel(*refs, layer, n_lat, n_ctx, need_ctx):
    n_seg = 2 if need_ctx else 1
    it = iter(refs)
    lg_ref, rl_ref, rc_ref, c_ref, sa_ref, sb_ref = [next(it) for _ in range(6)]
    u_refs = [next(it) for _ in range(n_seg)]
    band_ref, w_ref, s_ref = next(it), next(it), next(it)
    ro_refs = [next(it) for _ in range(n_seg)]
    po_refs = [next(it) for _ in range(n_seg)]
    q_s, k_s, u_s, st_s, pad_s = it
    ch = RET_CHUNK
    pairs = RET_HEADS // 2
    n_all = n_lat + n_ctx
    lane = lax.broadcasted_iota(jnp.int32, (1, LANES), 1)
    lo = lane < RET_DIM
    sub_lo = lax.broadcasted_iota(jnp.int32, (LANES, 1), 0) < RET_DIM
    row = lax.broadcasted_iota(jnp.int32, (ch, 1), 0).astype(F32)
    col = lax.broadcasted_iota(jnp.int32, (1, ch), 1).astype(F32)
    diff = row - col
    head_of_row = lax.broadcasted_iota(jnp.int32, (LANES, 1), 0) // RET_DIM
    blockdiag = (head_of_row == lane // RET_DIM).astype(F32)
    k_scale = RET_DIM ** -0.5
    half = RET_DIM // 2

    def pair_tables(p):
        lf_a, lf_b = lg_ref[layer, 0, 2 * p], lg_ref[layer, 0, 2 * p + 1]
        lb_a, lb_b = lg_ref[layer, 1, 2 * p], lg_ref[layer, 1, 2 * p + 1]
        lf = jnp.where(lo, lf_a, lf_b)
        lb = jnp.where(lo, lb_a, lb_b)
        lf_t = jnp.where(sub_lo, lf_a, lf_b)
        lb_t = jnp.where(sub_lo, lb_a, lb_b)

        def decay(l_f, l_b):
            return jnp.where(diff >= 0, jnp.exp(l_f * jnp.maximum(diff, 0.0)),
                             jnp.exp(l_b * jnp.maximum(-diff, 0.0)))

        return dict(
            d_ab=jnp.concatenate([decay(lf_a, lb_a), decay(lf_b, lb_b)], axis=0) * k_scale,
            wq_fb=jnp.concatenate([jnp.exp(lf * (row + 1.0)), jnp.exp(lb * (ch - row))], axis=1),
            wk_fb_t=jnp.concatenate([jnp.exp(lf_t * (ch - 1.0 - col)), jnp.exp(lb_t * col)], axis=0) * k_scale,
            gf=jnp.exp(lf * ch), gb=jnp.exp(lb * ch))

    tables = [pair_tables(p) for p in range(pairs)]

    def source(n):
        if n < n_lat:
            return rl_ref, slice(n * ch, (n + 1) * ch), True
        return rc_ref, slice((n - n_lat) * ch, (n - n_lat + 1) * ch), False

    def part(p, which):
        c0 = (which * pairs + p) * LANES
        return slice(c0, c0 + LANES)

    lo_pool = lane < POOL_GDIM
    halo = POOL_HALO

    def pool_fill(u_ref, length):
        pad_s[0:halo, :] = jnp.zeros((halo, POOL_W), BF16)
        pad_s[halo:halo + length, :] = u_ref[0]
        pad_s[halo + length:2 * halo + length, :] = jnp.zeros((halo, POOL_W), BF16)

    def pool_slab(u_ref, o_ref, length, r0):
        t = (lax.broadcasted_iota(jnp.int32, (POOL_TILE, 1), 0) + r0).astype(F32)
        cols = []
        for hf in range(POOL_W // LANES):
            cs = slice(hf * LANES, (hf + 1) * LANES)
            tokens = pad_s[r0:r0 + POOL_TILE + 2 * halo, cs]

            def inv_count(w):
                if r0 - w // 2 >= 0 and r0 + POOL_TILE - w // 2 + w <= length:
                    return 1.0 / w
                cnt = (jnp.clip(t - w // 2 + w, 0.0, float(length))
                       - jnp.clip(t - w // 2, 0.0, float(length)))
                return 1.0 / cnt

            w_n, w_w = POOL_WINDOWS[2 * hf], POOL_WINDOWS[2 * hf + 1]
            pooled = jnp.where(lo_pool, _dot(band_ref[2 * hf], tokens) * inv_count(w_n),
                               _dot(band_ref[2 * hf + 1], tokens) * inv_count(w_w))
            cols.append(pooled - u_ref[0, r0:r0 + POOL_TILE, cs].astype(F32))
        pooled = jnp.concatenate(cols, axis=-1).astype(BF16)
        y = _dot(pooled, w_ref[0]) * s_ref[0]
        o_ref[0, r0:r0 + POOL_TILE, :] = y.astype(o_ref.dtype)

    pool_tasks = []
    for u_ref, o_ref in zip(u_refs, po_refs):
        length = u_ref.shape[1]
        pool_tasks.append(functools.partial(pool_fill, u_ref, length))
        pool_tasks += [functools.partial(pool_slab, u_ref, o_ref, length, r0)
                       for r0 in range(0, length, POOL_TILE)]
    pool_tasks.reverse()

    for n in range(n_all):
        src_ref, src_rows, roped = source(n)
        rows = slice(n * ch, (n + 1) * ch)
        for p in range(pairs):
            q = src_ref[0, src_rows, part(p, 0)].astype(F32)
            k = src_ref[0, src_rows, part(p, 1)].astype(F32)
            if roped:
                c, sa, sb = c_ref[src_rows, :], sa_ref[src_rows, :], sb_ref[src_rows, :]
                q = q * c + pltpu.roll(q, LANES - half, 1) * sa + pltpu.roll(q, half, 1) * sb
                k = k * c + pltpu.roll(k, LANES - half, 1) * sa + pltpu.roll(k, half, 1) * sb
            q_s[p, rows, :] = q
            k_s[p, rows, :] = k
            kt = k.T
            kt2 = jnp.concatenate([kt, kt], axis=0) * tables[p]["wk_fb_t"]
            u_s[p, n] = _dot(kt2.astype(BF16), src_ref[0, src_rows, part(p, 2)])

    order_f = list(range(n_lat, n_all)) + list(range(n_lat))
    order_b = list(range(n_all - 1, -1, -1))
    for p in range(pairs):
        s = jnp.zeros((LANES, LANES), F32)
        for n in order_f:
            st_s[p, n, 0:LANES, :] = (s * blockdiag).astype(BF16)
            s = tables[p]["gf"] * s + u_s[p, n, 0:LANES, :]
        s = jnp.zeros((LANES, LANES), F32)
        for n in order_b:
            st_s[p, n, LANES:2 * LANES, :] = (s * blockdiag).astype(BF16)
            s = tables[p]["gb"] * s + u_s[p, n, LANES:2 * LANES, :]

    n_out = n_all if need_ctx else n_lat

    def scores_and_state(n):
        rows = slice(n * ch, (n + 1) * ch)
        out = []
        for p in range(pairs):
            q = q_s[p, rows, :]
            kb = k_s[p, rows, :].astype(BF16)
            q_ab = jnp.concatenate([jnp.where(lo, q, 0.0), jnp.where(lo, 0.0, q)], axis=0).astype(BF16)
            a_ab = _dot_nt(q_ab, kb)
            carried = _dot((jnp.concatenate([q, q], axis=1) * tables[p]["wq_fb"]).astype(BF16), st_s[p, n])
            out.append((a_ab, carried))
        return out

    def finish(n, staged):
        src_ref, src_rows, _ = source(n)
        dst_ref = ro_refs[0] if n < n_lat else ro_refs[1]
        for p, (a_ab, carried) in enumerate(staged):
            vb = src_ref[0, src_rows, part(p, 2)]
            gate = src_ref[0, src_rows, part(p, 3)].astype(F32)
            o_ab = _dot((a_ab * tables[p]["d_ab"]).astype(BF16), vb)
            o = jnp.where(lo, o_ab[0:ch], o_ab[ch:2 * ch]) + carried
            inv = 1.0 / RET_DIM
            mu = jnp.where(lo, jnp.sum(jnp.where(lo, o, 0.0), axis=-1, keepdims=True),
                           jnp.sum(jnp.where(lo, 0.0, o), axis=-1, keepdims=True)) * inv
            dlt = o - mu
            sq = dlt * dlt
            var = jnp.where(lo, jnp.sum(jnp.where(lo, sq, 0.0), axis=-1, keepdims=True),
                            jnp.sum(jnp.where(lo, 0.0, sq), axis=-1, keepdims=True)) * inv
            dst_ref[0, src_rows, p * LANES:(p + 1) * LANES] = (
                _silu(gate) * (dlt * lax.rsqrt(var + EPS))).astype(dst_ref.dtype)

    staged = scores_and_state(0)
    for n in range(n_out):
        staged_next = scores_and_state(n + 1) if n + 1 < n_out else None
        if pool_tasks:
            pool_tasks.pop()()
        finish(n, staged)
        staged = staged_next
    while pool_tasks:
        pool_tasks.pop()()


def _ret_pool(lg, layer, ret_lat, ret_ctx, tabs, us, band, w_bd, scale, need_ctx):
    b, rl, _ = ret_lat.shape
    rc = ret_ctx.shape[1]
    ch = RET_CHUNK
    n_lat, n_ctx = rl // ch, rc // ch
    n_all = n_lat + n_ctx
    pairs = RET_HEADS // 2
    lengths = tuple(u.shape[1] for u in us)
    assert len(us) == (2 if need_ctx else 1)
    kern = functools.partial(_ret_pool_kernel, layer=layer, n_lat=n_lat, n_ctx=n_ctx, need_ctx=need_ctx)
    tab_spec = pl.BlockSpec((rl, LANES), lambda i: (0, 0), pipeline_mode=pl.Buffered(1))
    rows = lambda n, width: pl.BlockSpec((1, n, width), lambda i: (i, 0, 0))
    out_specs = [rows(n, RET_W) for n in lengths] + [rows(n, POOL_W) for n in lengths]
    out_shape = ([jax.ShapeDtypeStruct((b, n, RET_W), BF16) for n in lengths]
                 + [jax.ShapeDtypeStruct((b, n, POOL_W), BF16) for n in lengths])
    outs = pl.pallas_call(
        kern,
        grid=(b,),
        in_specs=[
            pl.BlockSpec(memory_space=pltpu.SMEM),
            rows(rl, 4 * RET_W), rows(rc, 4 * RET_W),
            tab_spec, tab_spec, tab_spec,
            *[rows(n, POOL_W) for n in lengths],
            pl.BlockSpec(band.shape, lambda i: (0, 0, 0), pipeline_mode=pl.Buffered(1)),
            _layer_spec(w_bd, layer), _layer_spec(scale, layer),
        ],
        out_specs=out_specs,
        out_shape=out_shape,
        scratch_shapes=[
            pltpu.VMEM((pairs, rl + rc, LANES), F32), pltpu.VMEM((pairs, rl + rc, LANES), F32),
            pltpu.VMEM((pairs, n_all, 2 * LANES, LANES), F32),
            pltpu.VMEM((pairs, n_all, 2 * LANES, LANES), BF16),
            pltpu.VMEM((max(lengths) + 2 * POOL_HALO, POOL_W), BF16),
        ],
        compiler_params=_params(("parallel",)),
        name="retention_pool",
    )(lg, ret_lat, ret_ctx, *tabs, *us, band, w_bd, scale)
    n = len(lengths)
    return outs[:n], outs[n:]


def _attn_kernel(*refs, use_lat):
    if use_lat:
        q_ref, kl_ref, vtl_ref, kc_ref, vtc_ref, o_ref = refs
    else:
        q_ref, kc_ref, vtc_ref, o_ref = refs

    sources = [(kc_ref, vtc_ref)]
    if use_lat:
        sources.append((kl_ref, vtl_ref))

    n_sub = q_ref.shape[2] // Q_SUB
    items = [((qs, h), k_ref, vt_ref, c0) for qs in range(n_sub) for h in range(MLA_HEADS)
             for k_ref, vt_ref in sources for c0 in range(0, k_ref.shape[2], KEY_CHUNK)]
    last_item = {g: i for i, (g, _, _, _) in enumerate(items)}

    scores, m_run, acc, done = {}, {}, {}, {}
    for t in range(len(items) + SCORE_LOOKAHEAD):
        if t < len(items):
            (qs, h), k_ref, _, c0 = items[t]
            q = q_ref[0, h, qs * Q_SUB:(qs + 1) * Q_SUB, :]
            scores[t] = _dot_nt(k_ref[0, h, c0:c0 + KEY_CHUNK, :], q)
        i = t - SCORE_LOOKAHEAD
        if i < 0:
            continue
        g, _, vt_ref, c0 = items[i]
        qs, h = g
        s = scores.pop(i)
        m_new = jnp.max(s, axis=0, keepdims=True)
        if g in m_run:
            m_new = jnp.maximum(m_run[g], m_new)
        pv = _dot(vt_ref[0, h, :, c0:c0 + KEY_CHUNK], jnp.exp2(s - m_new).astype(BF16))
        acc[g] = pv if g not in acc else acc[g] * jnp.exp2(m_run[g] - m_new) + pv
        m_run[g] = m_new
        if i == last_item[g]:
            a = acc.pop(g)
            done[g] = a[0:MLA_V, :] * (1.0 / a[MLA_V:MLA_V + 1, :])
            if h % 2 == 1:
                pair_t = jnp.concatenate([done.pop((qs, h - 1)), done.pop(g)], axis=0)
                o_ref[0, qs * Q_SUB:(qs + 1) * Q_SUB, (h // 2) * HEAD_PAD:(h // 2 + 1) * HEAD_PAD] = (
                    pair_t.T.astype(o_ref.dtype))


def _attention(q, k_ctx, vt_ctx, k_lat=None, vt_lat=None):
    b, nh, rows, _ = q.shape
    use_lat = k_lat is not None
    tile = min(ATTN_TILE, rows)

    def whole(a):
        return pl.BlockSpec((1,) + a.shape[1:], lambda i, j: (i, 0, 0, 0))

    in_specs = [pl.BlockSpec((1, nh, tile, HEAD_PAD), lambda i, j: (i, 0, j, 0))]
    args = [q]
    if use_lat:
        in_specs += [whole(k_lat), whole(vt_lat)]
        args += [k_lat, vt_lat]
    in_specs += [whole(k_ctx), whole(vt_ctx)]
    args += [k_ctx, vt_ctx]
    return pl.pallas_call(
        functools.partial(_attn_kernel, use_lat=use_lat),
        grid=(b, rows // tile),
        in_specs=in_specs,
        out_specs=pl.BlockSpec((1, tile, MLA_W), lambda i, j: (i, j, 0)),
        out_shape=jax.ShapeDtypeStruct((b, rows, MLA_W), BF16),
        compiler_params=_params(("parallel", "arbitrary")),
        name="mla_attention",
    )(*args)


def _mix_mlp_kernel(x_ref, r_ref, m_ref, p_ref, wo_ref, mod_ref, g_ref, w1_ref, w2_ref, gf_ref, o_ref,
                    *, final_norm):
    def mix(rows):
        y = (_dot(r_ref[0, rows, :], wo_ref[0, 0:RET_W, :])
             + _dot(m_ref[0, rows, :], wo_ref[0, RET_W:RET_W + MLA_W, :])
             + _dot(p_ref[0, rows, :], wo_ref[0, RET_W + MLA_W:, :]))
        x = x_ref[0, rows, :] + mod_ref[0, 0, 2:3, :] * y
        h = (_rms(x, g_ref[0]) * (1.0 + mod_ref[0, 0, 4:5, :]) + mod_ref[0, 0, 3:4, :]).astype(BF16)
        return x, h

    def mlp(x, h, rows):
        acc = None
        for c0 in range(0, w1_ref.shape[2], FF_CHUNK):
            a = jnp.maximum(_dot(h, w1_ref[0, :, c0:c0 + FF_CHUNK]), 0.0)
            part = _dot((a * a).astype(BF16), w2_ref[0, c0:c0 + FF_CHUNK, :])
            acc = part if acc is None else acc + part
        y = x + mod_ref[0, 0, 5:6, :] * acc
        if final_norm:
            y = _rms(y, gf_ref[...])
        o_ref[0, rows, :] = y

    sub = min(MLP_SUB, x_ref.shape[1])
    subs = [slice(r0, r0 + sub) for r0 in range(0, x_ref.shape[1], sub)]
    staged = mix(subs[0])
    for i, rows in enumerate(subs):
        x, h = staged
        if i + 1 < len(subs):
            staged = mix(subs[i + 1])
        mlp(x, h, rows)


def _mix_mlp(x, ret, mla, pool, mod, mod_row, layer, w_out, g, w1, w2, g_final, tile, final_norm):
    b, r, d = x.shape
    row = lambda width: pl.BlockSpec((1, tile, width), lambda i, j: (i, j, 0))
    return pl.pallas_call(
        functools.partial(_mix_mlp_kernel, final_norm=final_norm),
        grid=(b, r // tile),
        in_specs=[
            row(d), row(RET_W), row(MLA_W), row(POOL_W),
            _layer_spec(w_out, layer),
            pl.BlockSpec((1, 1) + mod.shape[2:], lambda i, j: (layer, mod_row(i), 0, 0)),
            _layer_spec(g, layer), _layer_spec(w1, layer), _layer_spec(w2, layer),
            pl.BlockSpec(g_final.shape, lambda i, j: (0, 0)),
        ],
        out_specs=row(d),
        out_shape=jax.ShapeDtypeStruct((b, r, d), F32),
        compiler_params=_params(("parallel", "parallel")),
        name="mix_mlp",
    )(x, ret, mla, pool, w_out, mod, g, w1, w2, g_final)


def _prep_w_uq(w_uq):
    depth, rank, _ = w_uq.shape
    half = MLA_ROPE // 2
    heads = w_uq.astype(BF16).reshape(depth, rank, MLA_HEADS, MLA_NOPE + MLA_ROPE)
    main = jnp.pad(heads, ((0, 0), (0, 0), (0, 0), (0, HEAD_PAD - MLA_NOPE - MLA_ROPE)))
    rope = heads[..., MLA_NOPE:]
    partner = jnp.concatenate([-rope[..., half:], rope[..., :half]], axis=-1)
    return jnp.concatenate([main.reshape(depth, rank, UQ_ROT),
                            partner.reshape(depth, rank, MLA_HEADS * MLA_ROPE)], axis=-1)


def _prep_w_ukv(w_ukv):
    depth, rank, _ = w_ukv.shape
    heads = w_ukv.astype(BF16).reshape(depth, rank, MLA_HEADS, MLA_NOPE + MLA_V)
    return jnp.concatenate([heads[..., :MLA_NOPE].reshape(depth, rank, UKV_V),
                            heads[..., MLA_NOPE:].reshape(depth, rank, UKV_COLS - UKV_V)], axis=-1)


def _prep_w_in_kernel(w_ref, o_ref):
    half = MLA_ROPE // 2
    o_ref[0, :, 0:COL_U] = w_ref[0, :, 0:COL_U].astype(o_ref.dtype)
    o_ref[0, :, COL_U:COL_KPE] = w_ref[0, :, COL_U + MLA_ROPE:COL_U + MLA_ROPE + POOL_W].astype(o_ref.dtype)
    v = w_ref[0, :, COL_U:COL_U + HEAD_PAD]
    lane = lax.broadcasted_iota(jnp.int32, (1, HEAD_PAD), 1)
    kx = jnp.where(lane < MLA_ROPE, v,
                   jnp.where(lane < MLA_ROPE + half, -pltpu.roll(v, half, 1),
                             jnp.where(lane < 2 * MLA_ROPE, pltpu.roll(v, MLA_ROPE + half, 1), 0.0)))
    o_ref[0, :, COL_KPE:IN_COLS] = kx.astype(o_ref.dtype)


def _prep_w_in(w_in):
    depth, d, n = w_in.shape
    assert n == COL_U + MLA_ROPE + POOL_W
    tile = 256
    return pl.pallas_call(
        _prep_w_in_kernel,
        grid=(depth, d // tile),
        in_specs=[pl.BlockSpec((1, tile, n), lambda l, i: (l, i, 0))],
        out_specs=pl.BlockSpec((1, tile, IN_COLS), lambda l, i: (l, i, 0)),
        out_shape=jax.ShapeDtypeStruct((depth, d, IN_COLS), BF16),
        compiler_params=_params(("parallel", "parallel")),
        name="prep_w_in",
    )(w_in)


def _rope_angles(n_rows, dim):
    pos = np.arange(n_rows)
    n_freq = dim // 4
    inv = np.float32(ROPE_BASE) ** (-np.arange(n_freq, dtype=np.float32) / np.float32(n_freq))
    return np.concatenate([(pos // GRID_W).astype(np.float32)[:, None] * inv,
                           (pos % GRID_W).astype(np.float32)[:, None] * inv], axis=-1)


def _ret_rope_tables(n_rows):
    ang = _rope_angles(n_rows, RET_DIM)
    cos, sin = np.cos(ang), np.sin(ang)
    zero = np.zeros_like(sin)
    reps = LANES // RET_DIM
    c = np.tile(np.concatenate([cos, cos], axis=-1), (1, reps))
    sa = np.tile(np.concatenate([-sin, zero], axis=-1), (1, reps))
    sb = np.tile(np.concatenate([zero, sin], axis=-1), (1, reps))
    return tuple(jnp.asarray(a, F32) for a in (c, sa, sb))


def _mla_rope_tables(n_rows):
    ang = _rope_angles(n_rows, MLA_ROPE)
    cos, sin = np.cos(ang), np.sin(ang)
    c = np.ones((n_rows, HEAD_PAD), np.float32)
    s = np.zeros((n_rows, HEAD_PAD), np.float32)
    c[:, MLA_NOPE:MLA_NOPE + MLA_ROPE] = np.concatenate([cos, cos], axis=-1)
    s[:, MLA_NOPE:MLA_NOPE + MLA_ROPE] = np.concatenate([sin, sin], axis=-1)
    return jnp.asarray(c, F32), jnp.asarray(s, F32)


def _pool_bands():
    t = np.arange(POOL_TILE)[:, None]
    s = np.arange(POOL_TILE + 2 * POOL_HALO)[None, :]
    bands = [(s >= t + POOL_HALO - w // 2) & (s < t + POOL_HALO - w // 2 + w) for w in POOL_WINDOWS]
    return jnp.asarray(np.stack(bands).astype(np.float32), BF16)


def _block_diag(w_pool):
    depth, g, c, _ = w_pool.shape
    rows = []
    for i in range(g):
        blocks = [w_pool[:, i] if j == i else jnp.zeros((depth, c, c), w_pool.dtype) for j in range(g)]
        rows.append(jnp.concatenate(blocks, axis=2))
    return jnp.concatenate(rows, axis=1)


def kernel(x, c, ctx, c_ctx, w_ada, b_ada, norm_mix, w_in, q_norm, w_uq, kv_norm, w_ukv,
           ret_decay_logit, w_pool, pool_scale, w_out, norm_mlp, w_ff1, w_ff2, norm_final):
    b, n_lat, d = x.shape
    n_ctx = ctx.shape[1]
    depth = w_ada.shape[0]
    lat_tile, mlp_tile = min(LAT_TILE, n_lat), min(MLP_TILE, n_lat)
    assert n_lat % lat_tile == 0 and n_lat % mlp_tile == 0 and n_lat % min(ATTN_TILE, n_lat) == 0
    assert n_lat % GRID_W == 0 and n_lat % Q_SUB == 0
    assert n_ctx % RET_CHUNK == 0 and n_ctx % POOL_TILE == 0 and n_lat % POOL_TILE == 0
    assert d == 4 * RET_W and w_ff1.shape[2] % FF_CHUNK == 0

    ret_tabs = _ret_rope_tables(n_lat)
    mla_tabs = _mla_rope_tables(n_lat)

    rows = -(-(b + 1) // 8) * 8
    cc = jnp.zeros((rows, d), F32).at[:b].set(c).at[b].set(c_ctx)
    mod = _ada(cc, w_ada, b_ada).reshape(depth, rows, 6, d)
    lat_row, ctx_row = (lambda i: i), (lambda i: b)
    lg = jax.nn.log_sigmoid(ret_decay_logit.astype(F32))

    def rows3(a):
        return a.reshape(depth, 1, a.shape[-1])

    proj_w = (rows3(norm_mix), _prep_w_in(w_in), rows3(q_norm), _prep_w_uq(w_uq),
              rows3(kv_norm), _prep_w_ukv(w_ukv))
    pool_w = (_pool_bands(), _block_diag(w_pool).astype(BF16), rows3(pool_scale))
    mlp_w = (w_out.astype(BF16), rows3(norm_mlp), w_ff1.astype(BF16), w_ff2.astype(BF16), norm_final[None])

    h_ctx = ctx
    for l in range(depth):
        last = l == depth - 1
        ret_l, u_l, q_l, k_l, vt_l = _inproj(x, mod, lat_row, l, *proj_w, mla_tabs, lat_tile)
        ret_c, u_c, q_c, k_c, vt_c = _inproj(h_ctx, mod, ctx_row, l, *proj_w, None, n_ctx)
        ret_o, pool_o = _ret_pool(lg, l, ret_l, ret_c, ret_tabs, (u_l,) if last else (u_l, u_c), *pool_w, not last)
        mla_l = _attention(q_l, k_c, vt_c, k_l, vt_l)
        if not last:
            mla_c = _attention(q_c, k_c, vt_c)
            h_ctx = _mix_mlp(h_ctx, ret_o[1], mla_c, pool_o[1], mod, ctx_row, l, *mlp_w, n_ctx, False)
        x = _mix_mlp(x, ret_o[0], mla_l, pool_o[0], mod, lat_row, l, *mlp_w, mlp_tile, last)
    return x
```

```python
import functools
import math

import jax
import jax.numpy as jnp
import numpy as np
from jax import lax
from jax.experimental import pallas as pl
from jax.experimental.pallas import tpu as pltpu

GRID_W = 64
RET_HEADS = 4
RET_DIM = 64
RET_W = RET_HEADS * RET_DIM
RET_CHUNK = 128
MLA_HEADS = 8
MLA_NOPE = 64
MLA_ROPE = 32
MLA_V = 64
MLA_Q_RANK = 256
MLA_KV_RANK = 128
MLA_W = MLA_HEADS * MLA_V
POOL_GROUPS = 4
POOL_WINDOWS = (2, 4, 8, 16)
POOL_GDIM = 64
POOL_W = POOL_GROUPS * POOL_GDIM
ROPE_BASE = 10000.0
EPS = 1e-6

LANES = 128
LAT_TILE = 1024
MLP_TILE = 1024
IN_SUB = 256
ATTN_TILE = 1024
Q_SUB = 256
KEY_CHUNK = 256
SCORE_LOOKAHEAD = 5
POOL_TILE = 256
HEAD_PAD = 128
VT_ROWS = MLA_V + 16
POOL_HALO = 128
FF_CHUNK = 1024
MLP_SUB = 256
VMEM_LIMIT = 56 * 1024 * 1024

COL_RET = 0
COL_CQ = 4 * RET_W
COL_CKV = COL_CQ + MLA_Q_RANK
COL_U = COL_CKV + MLA_KV_RANK
COL_KPE = COL_U + POOL_W
IN_COLS = COL_KPE + HEAD_PAD
UQ_ROT = MLA_HEADS * HEAD_PAD
UQ_COLS = UQ_ROT + MLA_HEADS * MLA_ROPE
UKV_V = MLA_HEADS * MLA_NOPE
UKV_COLS = UKV_V + MLA_HEADS * MLA_V

F32 = jnp.float32
BF16 = jnp.bfloat16


def _params(sem):
    return pltpu.CompilerParams(dimension_semantics=sem, vmem_limit_bytes=VMEM_LIMIT)


def _layer_spec(a, layer):
    tail = a.shape[1:]
    return pl.BlockSpec((1,) + tail, lambda *_: (layer,) + (0,) * len(tail), pipeline_mode=pl.Buffered(1))


def _dot(a, b):
    return jnp.dot(a, b, preferred_element_type=F32)


def _dot_nt(a, b):
    return lax.dot_general(a, b, (((1,), (1,)), ((), ())), preferred_element_type=F32)


def _rms(x, g):
    return x * lax.rsqrt(jnp.mean(x * x, axis=-1, keepdims=True) + EPS) * g


def _silu(x):
    return x * (1.0 / (1.0 + jnp.exp(-x)))


def _ada_kernel(c_ref, w_ref, b_ref, o_ref):
    s = _silu(c_ref[...]).astype(BF16)
    o_ref[0] = _dot(s, w_ref[0].astype(BF16)) + b_ref[0]


def _ada(cc, w_ada, b_ada):
    depth, d, n = w_ada.shape
    rows = cc.shape[0]
    tn = n // 4
    return pl.pallas_call(
        _ada_kernel,
        grid=(depth, n // tn),
        in_specs=[
            pl.BlockSpec((rows, d), lambda l, j: (0, 0)),
            pl.BlockSpec((1, d, tn), lambda l, j: (l, 0, j)),
            pl.BlockSpec((1, 1, tn), lambda l, j: (l, 0, j)),
        ],
        out_specs=pl.BlockSpec((1, rows, tn), lambda l, j: (l, 0, j)),
        out_shape=jax.ShapeDtypeStruct((depth, rows, n), F32),
        compiler_params=_params(("parallel", "parallel")),
        name="ada_mod",
    )(cc, w_ada, b_ada.reshape(depth, 1, n))


def _inproj_kernel(*refs, rope):
    if rope:
        (x_ref, mod_ref, g_ref, w_ref, gq_ref, wq_ref, gkv_ref, wkv_ref, c_ref, s_ref,
         ret_ref, u_ref, q_ref, k_ref, vt_ref) = refs
    else:
        (x_ref, mod_ref, g_ref, w_ref, gq_ref, wq_ref, gkv_ref, wkv_ref,
         ret_ref, u_ref, q_ref, k_ref, vt_ref) = refs
    nh = MLA_HEADS
    lane = lax.broadcasted_iota(jnp.int32, (1, HEAD_PAD), 1)
    nope_lanes = lane < MLA_NOPE
    rope_lanes = jnp.logical_and(lane >= MLA_NOPE, lane < MLA_NOPE + MLA_ROPE)
    sub = min(IN_SUB, x_ref.shape[1])
    den_rows = (lax.broadcasted_iota(jnp.int32, (VT_ROWS - MLA_V, sub), 0) == 0).astype(vt_ref.dtype)
    q_scale = (MLA_NOPE + MLA_ROPE) ** -0.5 * math.log2(math.e)

    def project(rows):
        h = _rms(x_ref[0, rows, :], g_ref[0]) * (1.0 + mod_ref[0, 0, 1:2, :]) + mod_ref[0, 0, 0:1, :]
        p = _dot(h.astype(BF16), w_ref[0])
        ret_ref[0, rows, :] = p[:, COL_RET:COL_CQ].astype(ret_ref.dtype)
        u_ref[0, rows, :] = p[:, COL_U:COL_KPE].astype(u_ref.dtype)
        return p

    def up_project(p, rows):
        yq = _rms(p[:, COL_CQ:COL_CKV], gq_ref[0]).astype(BF16)
        if rope:
            cos, sin = c_ref[rows, :], s_ref[rows, :]
            q2 = _dot(yq, wq_ref[0])
        else:
            q2 = _dot(yq, wq_ref[0, :, 0:UQ_ROT])
        ykv = _rms(p[:, COL_CKV:COL_U], gkv_ref[0]).astype(BF16)
        kv = _dot(ykv, wkv_ref[0])
        kx = p[:, COL_KPE:IN_COLS]
        kpe = pltpu.roll(kx, MLA_NOPE, 1)
        if rope:
            kpe = kpe * cos + pltpu.roll(kx, MLA_NOPE - MLA_ROPE, 1) * sin
        kpe = jnp.where(rope_lanes, kpe, 0.0)
        for hh in range(nh):
            qh = q2[:, hh * HEAD_PAD:(hh + 1) * HEAD_PAD]
            if rope:
                blk, j = divmod(hh * MLA_ROPE, HEAD_PAD)
                rot = q2[:, UQ_ROT + blk * HEAD_PAD:UQ_ROT + (blk + 1) * HEAD_PAD]
                shift = (MLA_NOPE - j) % HEAD_PAD
                qh = qh * cos + (pltpu.roll(rot, shift, 1) if shift else rot) * sin
            q_ref[0, hh, rows, :] = (qh * q_scale).astype(q_ref.dtype)
            blk, j = divmod(hh * MLA_NOPE, HEAD_PAD)
            kn = kv[:, blk * HEAD_PAD:(blk + 1) * HEAD_PAD]
            if j:
                kn = pltpu.roll(kn, HEAD_PAD - j, 1)
            k_ref[0, hh, rows, :] = jnp.where(nope_lanes, kn, kpe).astype(k_ref.dtype)
        for blk in range(nh * MLA_V // HEAD_PAD):
            vt = kv[:, UKV_V + blk * HEAD_PAD:UKV_V + (blk + 1) * HEAD_PAD].T.astype(vt_ref.dtype)
            for j in range(HEAD_PAD // MLA_V):
                hh = blk * (HEAD_PAD // MLA_V) + j
                vt_ref[0, hh, 0:MLA_V, rows] = vt[j * MLA_V:(j + 1) * MLA_V, :]
                vt_ref[0, hh, MLA_V:VT_ROWS, rows] = den_rows

    subs = [slice(r0, r0 + sub) for r0 in range(0, x_ref.shape[1], sub)]
    p_next = project(subs[0])
    for i, rows in enumerate(subs):
        p = p_next
        if i + 1 < len(subs):
            p_next = project(subs[i + 1])
        up_project(p, rows)


def _inproj(x, mod, mod_row, layer, g, w, gq, wq, gkv, wkv, tabs, tile):
    b, r, d = x.shape
    rope = tabs is not None
    nh = MLA_HEADS
    row = lambda width: pl.BlockSpec((1, tile, width), lambda i, j: (i, j, 0))
    head = pl.BlockSpec((1, nh, tile, HEAD_PAD), lambda i, j: (i, 0, j, 0))
    head_t = pl.BlockSpec((1, nh, VT_ROWS, tile), lambda i, j: (i, 0, 0, j))
    assert tile % min(IN_SUB, tile) == 0
    in_specs = [
        row(d),
        pl.BlockSpec((1, 1) + mod.shape[2:], lambda i, j: (layer, mod_row(i), 0, 0)),
    ] + [_layer_spec(a, layer) for a in (g, w, gq, wq, gkv, wkv)]
    args = [x, mod, g, w, gq, wq, gkv, wkv]
    if rope:
        in_specs += [pl.BlockSpec((tile, HEAD_PAD), lambda i, j: (j, 0))] * 2
        args += list(tabs)
    hshape = jax.ShapeDtypeStruct((b, nh, r, HEAD_PAD), BF16)
    return pl.pallas_call(
        functools.partial(_inproj_kernel, rope=rope),
        grid=(b, r // tile),
        in_specs=in_specs,
        out_specs=[row(4 * RET_W), row(POOL_W), head, head, head_t],
        out_shape=[jax.ShapeDtypeStruct((b, r, 4 * RET_W), BF16),
                   jax.ShapeDtypeStruct((b, r, POOL_W), BF16), hshape, hshape,
                   jax.ShapeDtypeStruct((b, nh, VT_ROWS, r), BF16)],
        compiler_params=_params(("parallel", "parallel")),
        name="in_proj",
    )(*args)


def _ret_pool_kernel(*refs, layer, n_lat, n_ctx, need_ctx):
    n_seg = 2 if need_ctx else 1
    it = iter(refs)
    lg_ref, rl_ref, rc_ref, c_ref, sa_ref, sb_ref = [next(it) for _ in range(6)]
    u_refs = [next(it) for _ in range(n_seg)]
    band_ref, w_ref, s_ref = next(it), next(it), next(it)
    ro_refs = [next(it) for _ in range(n_seg)]
    po_refs = [next(it) for _ in range(n_seg)]
    q_s, k_s, u_s, st_s, pad_s = it
    ch = RET_CHUNK
    pairs = RET_HEADS // 2
    n_all = n_lat + n_ctx
    lane = lax.broadcasted_iota(jnp.int32, (1, LANES), 1)
    lo = lane < RET_DIM
    sub_lo = lax.broadcasted_iota(jnp.int32, (LANES, 1), 0) < RET_DIM
    row = lax.broadcasted_iota(jnp.int32, (ch, 1), 0).astype(F32)
    col = lax.broadcasted_iota(jnp.int32, (1, ch), 1).astype(F32)
    diff = row - col
    head_of_row = lax.broadcasted_iota(jnp.int32, (LANES, 1), 0) // RET_DIM
    blockdiag = (head_of_row == lane // RET_DIM).astype(F32)
    k_scale = RET_DIM ** -0.5
    half = RET_DIM // 2

    def pair_tables(p):
        lf_a, lf_b = lg_ref[layer, 0, 2 * p], lg_ref[layer, 0, 2 * p + 1]
        lb_a, lb_b = lg_ref[layer, 1, 2 * p], lg_ref[layer, 1, 2 * p + 1]
        lf = jnp.where(lo, lf_a, lf_b)
        lb = jnp.where(lo, lb_a, lb_b)
        lf_t = jnp.where(sub_lo, lf_a, lf_b)
        lb_t = jnp.where(sub_lo, lb_a, lb_b)

        def decay(l_f, l_b):
            return jnp.where(diff >= 0, jnp.exp(l_f * jnp.maximum(diff, 0.0)),
                             jnp.exp(l_b * jnp.maximum(-diff, 0.0)))

        return dict(
            d_ab=jnp.concatenate([decay(lf_a, lb_a), decay(lf_b, lb_b)], axis=0) * k_scale,
            wq_fb=jnp.concatenate([jnp.exp(lf * (row + 1.0)), jnp.exp(lb * (ch - row))], axis=1),
            wk_fb_t=jnp.concatenate([jnp.exp(lf_t * (ch - 1.0 - col)), jnp.exp(lb_t * col)], axis=0) * k_scale,
            gf=jnp.exp(lf * ch), gb=jnp.exp(lb * ch))

    tables = [pair_tables(p) for p in range(pairs)]

    def source(n):
        if n < n_lat:
            return rl_ref, slice(n * ch, (n + 1) * ch), True
        return rc_ref, slice((n - n_lat) * ch, (n - n_lat + 1) * ch), False

    def part(p, which):
        c0 = (which * pairs + p) * LANES
        return slice(c0, c0 + LANES)

    lo_pool = lane < POOL_GDIM
    halo = POOL_HALO

    def pool_fill(u_ref, length):
        pad_s[0:halo, :] = jnp.zeros((halo, POOL_W), BF16)
        pad_s[halo:halo + length, :] = u_ref[0]
        pad_s[halo + length:2 * halo + length, :] = jnp.zeros((halo, POOL_W), BF16)

    def pool_slab(u_ref, o_ref, length, r0):
        t = (lax.broadcasted_iota(jnp.int32, (POOL_TILE, 1), 0) + r0).astype(F32)
        cols = []
        for hf in range(POOL_W // LANES):
            cs = slice(hf * LANES, (hf + 1) * LANES)
            tokens = pad_s[r0:r0 + POOL_TILE + 2 * halo, cs]

            def inv_count(w):
                if r0 - w // 2 >= 0 and r0 + POOL_TILE - w // 2 + w <= length:
                    return 1.0 / w
                cnt = (jnp.clip(t - w // 2 + w, 0.0, float(length))
                       - jnp.clip(t - w // 2, 0.0, float(length)))
                return 1.0 / cnt

            w_n, w_w = POOL_WINDOWS[2 * hf], POOL_WINDOWS[2 * hf + 1]
            pooled = jnp.where(lo_pool, _dot(band_ref[2 * hf], tokens) * inv_count(w_n),
                               _dot(band_ref[2 * hf + 1], tokens) * inv_count(w_w))
            cols.append(pooled - u_ref[0, r0:r0 + POOL_TILE, cs].astype(F32))
        pooled = jnp.concatenate(cols, axis=-1).astype(BF16)
        y = _dot(pooled, w_ref[0]) * s_ref[0]
        o_ref[0, r0:r0 + POOL_TILE, :] = y.astype(o_ref.dtype)

    pool_tasks = []
    for u_ref, o_ref in zip(u_refs, po_refs):
        length = u_ref.shape[1]
        pool_tasks.append(functools.partial(pool_fill, u_ref, length))
        pool_tasks += [functools.partial(pool_slab, u_ref, o_ref, length, r0)
                       for r0 in range(0, length, POOL_TILE)]
    pool_tasks.reverse()

    for n in range(n_all):
        src_ref, src_rows, roped = source(n)
        rows = slice(n * ch, (n + 1) * ch)
        for p in range(pairs):
            q = src_ref[0, src_rows, part(p, 0)].astype(F32)
            k = src_ref[0, src_rows, part(p, 1)].astype(F32)
            if roped:
                c, sa, sb = c_ref[src_rows, :], sa_ref[src_rows, :], sb_ref[src_rows, :]
                q = q * c + pltpu.roll(q, LANES - half, 1) * sa + pltpu.roll(q, half, 1) * sb
                k = k * c + pltpu.roll(k, LANES - half, 1) * sa + pltpu.roll(k, half, 1) * sb
            q_s[p, rows, :] = q
            k_s[p, rows, :] = k
            kt = k.T
            kt2 = jnp.concatenate([kt, kt], axis=0) * tables[p]["wk_fb_t"]
            u_s[p, n] = _dot(kt2.astype(BF16), src_ref[0, src_rows, part(p, 2)])

    order_f = list(range(n_lat, n_all)) + list(range(n_lat))
    order_b = list(range(n_all - 1, -1, -1))
    for p in range(pairs):
        s = jnp.zeros((LANES, LANES), F32)
        for n in order_f:
            st_s[p, n, 0:LANES, :] = (s * blockdiag).astype(BF16)
            s = tables[p]["gf"] * s + u_s[p, n, 0:LANES, :]
        s = jnp.zeros((LANES, LANES), F32)
        for n in order_b:
            st_s[p, n, LANES:2 * LANES, :] = (s * blockdiag).astype(BF16)
            s = tables[p]["gb"] * s + u_s[p, n, LANES:2 * LANES, :]

    n_out = n_all if need_ctx else n_lat

    def scores_and_state(n):
        rows = slice(n * ch, (n + 1) * ch)
        out = []
        for p in range(pairs):
            q = q_s[p, rows, :]
            kb = k_s[p, rows, :].astype(BF16)
            q_ab = jnp.concatenate([jnp.where(lo, q, 0.0), jnp.where(lo, 0.0, q)], axis=0).astype(BF16)
            a_ab = _dot_nt(q_ab, kb)
            carried = _dot((jnp.concatenate([q, q], axis=1) * tables[p]["wq_fb"]).astype(BF16), st_s[p, n])
            out.append((a_ab, carried))
        return out

    def finish(n, staged):
        src_ref, src_rows, _ = source(n)
        dst_ref = ro_refs[0] if n < n_lat else ro_refs[1]
        for p, (a_ab, carried) in enumerate(staged):
            vb = src_ref[0, src_rows, part(p, 2)]
            gate = src_ref[0, src_rows, part(p, 3)].astype(F32)
            o_ab = _dot((a_ab * tables[p]["d_ab"]).astype(BF16), vb)
            o = jnp.where(lo, o_ab[0:ch], o_ab[ch:2 * ch]) + carried
            inv = 1.0 / RET_DIM
            mu = jnp.where(lo, jnp.sum(jnp.where(lo, o, 0.0), axis=-1, keepdims=True),
                           jnp.sum(jnp.where(lo, 0.0, o), axis=-1, keepdims=True)) * inv
            dlt = o - mu
            sq = dlt * dlt
            var = jnp.where(lo, jnp.sum(jnp.where(lo, sq, 0.0), axis=-1, keepdims=True),
                            jnp.sum(jnp.where(lo, 0.0, sq), axis=-1, keepdims=True)) * inv
            dst_ref[0, src_rows, p * LANES:(p + 1) * LANES] = (
                _silu(gate) * (dlt * lax.rsqrt(var + EPS))).astype(dst_ref.dtype)

    staged = scores_and_state(0)
    for n in range(n_out):
        staged_next = scores_and_state(n + 1) if n + 1 < n_out else None
        if pool_tasks:
            pool_tasks.pop()()
        finish(n, staged)
        staged = staged_next
    while pool_tasks:
        pool_tasks.pop()()


def _ret_pool(lg, layer, ret_lat, ret_ctx, tabs, us, band, w_bd, scale, need_ctx):
    b, rl, _ = ret_lat.shape
    rc = ret_ctx.shape[1]
    ch = RET_CHUNK
    n_lat, n_ctx = rl // ch, rc // ch
    n_all = n_lat + n_ctx
    pairs = RET_HEADS // 2
    lengths = tuple(u.shape[1] for u in us)
    assert len(us) == (2 if need_ctx else 1)
    kern = functools.partial(_ret_pool_kernel, layer=layer, n_lat=n_lat, n_ctx=n_ctx, need_ctx=need_ctx)
    tab_spec = pl.BlockSpec((rl, LANES), lambda i: (0, 0), pipeline_mode=pl.Buffered(1))
    rows = lambda n, width: pl.BlockSpec((1, n, width), lambda i: (i, 0, 0))
    out_specs = [rows(n, RET_W) for n in lengths] + [rows(n, POOL_W) for n in lengths]
    out_shape = ([jax.ShapeDtypeStruct((b, n, RET_W), BF16) for n in lengths]
                 + [jax.ShapeDtypeStruct((b, n, POOL_W), BF16) for n in lengths])
    outs = pl.pallas_call(
        kern,
        grid=(b,),
        in_specs=[
            pl.BlockSpec(memory_space=pltpu.SMEM),
            rows(rl, 4 * RET_W), rows(rc, 4 * RET_W),
            tab_spec, tab_spec, tab_spec,
            *[rows(n, POOL_W) for n in lengths],
            pl.BlockSpec(band.shape, lambda i: (0, 0, 0), pipeline_mode=pl.Buffered(1)),
            _layer_spec(w_bd, layer), _layer_spec(scale, layer),
        ],
        out_specs=out_specs,
        out_shape=out_shape,
        scratch_shapes=[
            pltpu.VMEM((pairs, rl + rc, LANES), F32), pltpu.VMEM((pairs, rl + rc, LANES), F32),
            pltpu.VMEM((pairs, n_all, 2 * LANES, LANES), F32),
            pltpu.VMEM((pairs, n_all, 2 * LANES, LANES), BF16),
            pltpu.VMEM((max(lengths) + 2 * POOL_HALO, POOL_W), BF16),
        ],
        compiler_params=_params(("parallel",)),
        name="retention_pool",
    )(lg, ret_lat, ret_ctx, *tabs, *us, band, w_bd, scale)
    n = len(lengths)
    return outs[:n], outs[n:]


def _attn_kernel(*refs, use_lat):
    if use_lat:
        q_ref, kl_ref, vtl_ref, kc_ref, vtc_ref, o_ref = refs
    else:
        q_ref, kc_ref, vtc_ref, o_ref = refs

    sources = [(kc_ref, vtc_ref)]
    if use_lat:
        sources.append((kl_ref, vtl_ref))

    n_sub = q_ref.shape[2] // Q_SUB
    items = [((qs, h), k_ref, vt_ref, c0) for qs in range(n_sub) for h in range(MLA_HEADS)
             for k_ref, vt_ref in sources for c0 in range(0, k_ref.shape[2], KEY_CHUNK)]
    last_item = {g: i for i, (g, _, _, _) in enumerate(items)}

    scores, m_run, acc, done = {}, {}, {}, {}
    for t in range(len(items) + SCORE_LOOKAHEAD):
        if t < len(items):
            (qs, h), k_ref, _, c0 = items[t]
            q = q_ref[0, h, qs * Q_SUB:(qs + 1) * Q_SUB, :]
            scores[t] = _dot_nt(k_ref[0, h, c0:c0 + KEY_CHUNK, :], q)
        i = t - SCORE_LOOKAHEAD
        if i < 0:
            continue
        g, _, vt_ref, c0 = items[i]
        qs, h = g
        s = scores.pop(i)
        m_new = jnp.max(s, axis=0, keepdims=True)
        if g in m_run:
            m_new = jnp.maximum(m_run[g], m_new)
        pv = _dot(vt_ref[0, h, :, c0:c0 + KEY_CHUNK], jnp.exp2(s - m_new).astype(BF16))
        acc[g] = pv if g not in acc else acc[g] * jnp.exp2(m_run[g] - m_new) + pv
        m_run[g] = m_new
        if i == last_item[g]:
            a = acc.pop(g)
            done[g] = a[0:MLA_V, :] * (1.0 / a[MLA_V:MLA_V + 1, :])
            if h % 2 == 1:
                pair_t = jnp.concatenate([done.pop((qs, h - 1)), done.pop(g)], axis=0)
                o_ref[0, qs * Q_SUB:(qs + 1) * Q_SUB, (h // 2) * HEAD_PAD:(h // 2 + 1) * HEAD_PAD] = (
                    pair_t.T.astype(o_ref.dtype))


def _attention(q, k_ctx, vt_ctx, k_lat=None, vt_lat=None):
    b, nh, rows, _ = q.shape
    use_lat = k_lat is not None
    tile = min(ATTN_TILE, rows)

    def whole(a):
        return pl.BlockSpec((1,) + a.shape[1:], lambda i, j: (i, 0, 0, 0))

    in_specs = [pl.BlockSpec((1, nh, tile, HEAD_PAD), lambda i, j: (i, 0, j, 0))]
    args = [q]
    if use_lat:
        in_specs += [whole(k_lat), whole(vt_lat)]
        args += [k_lat, vt_lat]
    in_specs += [whole(k_ctx), whole(vt_ctx)]
    args += [k_ctx, vt_ctx]
    return pl.pallas_call(
        functools.partial(_attn_kernel, use_lat=use_lat),
        grid=(b, rows // tile),
        in_specs=in_specs,
        out_specs=pl.BlockSpec((1, tile, MLA_W), lambda i, j: (i, j, 0)),
        out_shape=jax.ShapeDtypeStruct((b, rows, MLA_W), BF16),
        compiler_params=_params(("parallel", "arbitrary")),
        name="mla_attention",
    )(*args)


def _mix_mlp_kernel(x_ref, r_ref, m_ref, p_ref, wo_ref, mod_ref, g_ref, w1_ref, w2_ref, gf_ref, o_ref,
                    *, final_norm):
    def mix(rows):
        y = (_dot(r_ref[0, rows, :], wo_ref[0, 0:RET_W, :])
             + _dot(m_ref[0, rows, :], wo_ref[0, RET_W:RET_W + MLA_W, :])
             + _dot(p_ref[0, rows, :], wo_ref[0, RET_W + MLA_W:, :]))
        x = x_ref[0, rows, :] + mod_ref[0, 0, 2:3, :] * y
        h = (_rms(x, g_ref[0]) * (1.0 + mod_ref[0, 0, 4:5, :]) + mod_ref[0, 0, 3:4, :]).astype(BF16)
        return x, h

    def mlp(x, h, rows):
        acc = None
        for c0 in range(0, w1_ref.shape[2], FF_CHUNK):
            a = jnp.maximum(_dot(h, w1_ref[0, :, c0:c0 + FF_CHUNK]), 0.0)
            part = _dot((a * a).astype(BF16), w2_ref[0, c0:c0 + FF_CHUNK, :])
            acc = part if acc is None else acc + part
        y = x + mod_ref[0, 0, 5:6, :] * acc
        if final_norm:
            y = _rms(y, gf_ref[...])
        o_ref[0, rows, :] = y

    sub = min(MLP_SUB, x_ref.shape[1])
    subs = [slice(r0, r0 + sub) for r0 in range(0, x_ref.shape[1], sub)]
    staged = mix(subs[0])
    for i, rows in enumerate(subs):
        x, h = staged
        if i + 1 < len(subs):
            staged = mix(subs[i + 1])
        mlp(x, h, rows)


def _mix_mlp(x, ret, mla, pool, mod, mod_row, layer, w_out, g, w1, w2, g_final, tile, final_norm):
    b, r, d = x.shape
    row = lambda width: pl.BlockSpec((1, tile, width), lambda i, j: (i, j, 0))
    return pl.pallas_call(
        functools.partial(_mix_mlp_kernel, final_norm=final_norm),
        grid=(b, r // tile),
        in_specs=[
            row(d), row(RET_W), row(MLA_W), row(POOL_W),
            _layer_spec(w_out, layer),
            pl.BlockSpec((1, 1) + mod.shape[2:], lambda i, j: (layer, mod_row(i), 0, 0)),
            _layer_spec(g, layer), _layer_spec(w1, layer), _layer_spec(w2, layer),
            pl.BlockSpec(g_final.shape, lambda i, j: (0, 0)),
        ],
        out_specs=row(d),
        out_shape=jax.ShapeDtypeStruct((b, r, d), F32),
        compiler_params=_params(("parallel", "parallel")),
        name="mix_mlp",
    )(x, ret, mla, pool, w_out, mod, g, w1, w2, g_final)


def _prep_w_uq(w_uq):
    depth, rank, _ = w_uq.shape
    half = MLA_ROPE // 2
    heads = w_uq.astype(BF16).reshape(depth, rank, MLA_HEADS, MLA_NOPE + MLA_ROPE)
    main = jnp.pad(heads, ((0, 0), (0, 0), (0, 0), (0, HEAD_PAD - MLA_NOPE - MLA_ROPE)))
    rope = heads[..., MLA_NOPE:]
    partner = jnp.concatenate([-rope[..., half:], rope[..., :half]], axis=-1)
    return jnp.concatenate([main.reshape(depth, rank, UQ_ROT),
                            partner.reshape(depth, rank, MLA_HEADS * MLA_ROPE)], axis=-1)


def _prep_w_ukv(w_ukv):
    depth, rank, _ = w_ukv.shape
    heads = w_ukv.astype(BF16).reshape(depth, rank, MLA_HEADS, MLA_NOPE + MLA_V)
    return jnp.concatenate([heads[..., :MLA_NOPE].reshape(depth, rank, UKV_V),
                            heads[..., MLA_NOPE:].reshape(depth, rank, UKV_COLS - UKV_V)], axis=-1)


def _prep_w_in_kernel(w_ref, o_ref):
    half = MLA_ROPE // 2
    o_ref[0, :, 0:COL_U] = w_ref[0, :, 0:COL_U].astype(o_ref.dtype)
    o_ref[0, :, COL_U:COL_KPE] = w_ref[0, :, COL_U + MLA_ROPE:COL_U + MLA_ROPE + POOL_W].astype(o_ref.dtype)
    v = w_ref[0, :, COL_U:COL_U + HEAD_PAD].astype(F32)
    lane = lax.broadcasted_iota(jnp.int32, (1, HEAD_PAD), 1)
    kx = jnp.where(lane < MLA_ROPE, v,
                   jnp.where(lane < MLA_ROPE + half, -pltpu.roll(v, half, 1),
                             jnp.where(lane < 2 * MLA_ROPE, pltpu.roll(v, MLA_ROPE + half, 1), 0.0)))
    o_ref[0, :, COL_KPE:IN_COLS] = kx.astype(o_ref.dtype)


def _prep_w_in(w_in):
    depth, d, n = w_in.shape
    assert n == COL_U + MLA_ROPE + POOL_W
    tile = 256
    return pl.pallas_call(
        _prep_w_in_kernel,
        grid=(depth, d // tile),
        in_specs=[pl.BlockSpec((1, tile, n), lambda l, i: (l, i, 0))],
        out_specs=pl.BlockSpec((1, tile, IN_COLS), lambda l, i: (l, i, 0)),
        out_shape=jax.ShapeDtypeStruct((depth, d, IN_COLS), BF16),
        compiler_params=_params(("parallel", "parallel")),
        name="prep_w_in",
    )(w_in)


def _rope_angles(n_rows, dim):
    pos = np.arange(n_rows)
    n_freq = dim // 4
    inv = np.float32(ROPE_BASE) ** (-np.arange(n_freq, dtype=np.float32) / np.float32(n_freq))
    return np.concatenate([(pos // GRID_W).astype(np.float32)[:, None] * inv,
                           (pos % GRID_W).astype(np.float32)[:, None] * inv], axis=-1)


def _ret_rope_tables(n_rows):
    ang = _rope_angles(n_rows, RET_DIM)
    cos, sin = np.cos(ang), np.sin(ang)
    zero = np.zeros_like(sin)
    reps = LANES // RET_DIM
    c = np.tile(np.concatenate([cos, cos], axis=-1), (1, reps))
    sa = np.tile(np.concatenate([-sin, zero], axis=-1), (1, reps))
    sb = np.tile(np.concatenate([zero, sin], axis=-1), (1, reps))
    return tuple(jnp.asarray(a, F32) for a in (c, sa, sb))


def _mla_rope_tables(n_rows):
    ang = _rope_angles(n_rows, MLA_ROPE)
    cos, sin = np.cos(ang), np.sin(ang)
    c = np.ones((n_rows, HEAD_PAD), np.float32)
    s = np.zeros((n_rows, HEAD_PAD), np.float32)
    c[:, MLA_NOPE:MLA_NOPE + MLA_ROPE] = np.concatenate([cos, cos], axis=-1)
    s[:, MLA_NOPE:MLA_NOPE + MLA_ROPE] = np.concatenate([sin, sin], axis=-1)
    return jnp.asarray(c, F32), jnp.asarray(s, F32)


def _pool_bands():
    t = np.arange(POOL_TILE)[:, None]
    s = np.arange(POOL_TILE + 2 * POOL_HALO)[None, :]
    bands = [(s >= t + POOL_HALO - w // 2) & (s < t + POOL_HALO - w // 2 + w) for w in POOL_WINDOWS]
    return jnp.asarray(np.stack(bands).astype(np.float32), BF16)


def _block_diag(w_pool):
    depth, g, c, _ = w_pool.shape
    rows = []
    for i in range(g):
        blocks = [w_pool[:, i] if j == i else jnp.zeros((depth, c, c), w_pool.dtype) for j in range(g)]
        rows.append(jnp.concatenate(blocks, axis=2))
    return jnp.concatenate(rows, axis=1)


def kernel(x, c, ctx, c_ctx, w_ada, b_ada, norm_mix, w_in, q_norm, w_uq, kv_norm, w_ukv,
           ret_decay_logit, w_pool, pool_scale, w_out, norm_mlp, w_ff1, w_ff2, norm_final):
    b, n_lat, d = x.shape
    n_ctx = ctx.shape[1]
    depth = w_ada.shape[0]
    lat_tile, mlp_tile = min(LAT_TILE, n_lat), min(MLP_TILE, n_lat)
    assert n_lat % lat_tile == 0 and n_lat % mlp_tile == 0 and n_lat % min(ATTN_TILE, n_lat) == 0
    assert n_lat % GRID_W == 0 and n_lat % Q_SUB == 0
    assert n_ctx % RET_CHUNK == 0 and n_ctx % POOL_TILE == 0 and n_lat % POOL_TILE == 0
    assert d == 4 * RET_W and w_ff1.shape[2] % FF_CHUNK == 0

    ret_tabs = _ret_rope_tables(n_lat)
    mla_tabs = _mla_rope_tables(n_lat)

    rows = -(-(b + 1) // 8) * 8
    cc = jnp.zeros((rows, d), F32).at[:b].set(c).at[b].set(c_ctx)
    mod = _ada(cc, w_ada, b_ada).reshape(depth, rows, 6, d)
    lat_row, ctx_row = (lambda i: i), (lambda i: b)
    lg = jax.nn.log_sigmoid(ret_decay_logit.astype(F32))

    def rows3(a):
        return a.reshape(depth, 1, a.shape[-1])

    proj_w = (rows3(norm_mix), _prep_w_in(w_in.astype(BF16)), rows3(q_norm), _prep_w_uq(w_uq),
              rows3(kv_norm), _prep_w_ukv(w_ukv))
    pool_w = (_pool_bands(), _block_diag(w_pool).astype(BF16), rows3(pool_scale))
    mlp_w = (w_out.astype(BF16), rows3(norm_mlp), w_ff1.astype(BF16), w_ff2.astype(BF16), norm_final[None])

    h_ctx = ctx
    for l in range(depth):
        last = l == depth - 1
        ret_l, u_l, q_l, k_l, vt_l = _inproj(x, mod, lat_row, l, *proj_w, mla_tabs, lat_tile)
        ret_c, u_c, q_c, k_c, vt_c = _inproj(h_ctx, mod, ctx_row, l, *proj_w, None, n_ctx)
        ret_o, pool_o = _ret_pool(lg, l, ret_l, ret_c, ret_tabs, (u_l,) if last else (u_l, u_c), *pool_w, not last)
        mla_l = _attention(q_l, k_c, vt_c, k_l, vt_l)
        if not last:
            mla_c = _attention(q_c, k_c, vt_c)
            h_ctx = _mix_mlp(h_ctx, ret_o[1], mla_c, pool_o[1], mod, ctx_row, l, *mlp_w, n_ctx, False)
        x = _mix_mlp(x, ret_o[0], mla_l, pool_o[0], mod, lat_row, l, *mlp_w, mlp_tile, last)
    return x
```

```python
import functools
import math

import jax
import jax.numpy as jnp
import numpy as np
from jax import lax
from jax.experimental import pallas as pl
from jax.experimental.pallas import tpu as pltpu

GRID_W = 64
RET_HEADS = 4
RET_DIM = 64
RET_W = RET_HEADS * RET_DIM
RET_CHUNK = 128
MLA_HEADS = 8
MLA_NOPE = 64
MLA_ROPE = 32
MLA_V = 64
MLA_Q_RANK = 256
MLA_KV_RANK = 128
MLA_W = MLA_HEADS * MLA_V
POOL_GROUPS = 4
POOL_WINDOWS = (2, 4, 8, 16)
POOL_GDIM = 64
POOL_W = POOL_GROUPS * POOL_GDIM
ROPE_BASE = 10000.0
EPS = 1e-6

LANES = 128
LAT_TILE = 1024
MLP_TILE = 1024
IN_SUB = 256
ATTN_TILE = 1024
Q_SUB = 256
KEY_CHUNK = 256
SCORE_LOOKAHEAD = 5
POOL_TILE = 256
HEAD_PAD = 128
VT_ROWS = MLA_V + 16
POOL_HALO = 128
FF_CHUNK = 1024
MLP_SUB = 256
VMEM_LIMIT = 56 * 1024 * 1024

COL_RET = 0
COL_CQ = 4 * RET_W
COL_CKV = COL_CQ + MLA_Q_RANK
COL_U = COL_CKV + MLA_KV_RANK
COL_KPE = COL_U + POOL_W
IN_COLS = COL_KPE + HEAD_PAD
UQ_ROT = MLA_HEADS * HEAD_PAD
UQ_COLS = UQ_ROT + MLA_HEADS * MLA_ROPE
UKV_V = MLA_HEADS * MLA_NOPE
UKV_COLS = UKV_V + MLA_HEADS * MLA_V

F32 = jnp.float32
BF16 = jnp.bfloat16


def _params(sem):
    return pltpu.CompilerParams(dimension_semantics=sem, vmem_limit_bytes=VMEM_LIMIT)


def _layer_spec(a, layer):
    tail = a.shape[1:]
    return pl.BlockSpec((1,) + tail, lambda *_: (layer,) + (0,) * len(tail), pipeline_mode=pl.Buffered(1))


def _dot(a, b):
    return jnp.dot(a, b, preferred_element_type=F32)


def _dot_nt(a, b):
    return lax.dot_general(a, b, (((1,), (1,)), ((), ())), preferred_element_type=F32)


def _rms(x, g):
    return x * lax.rsqrt(jnp.mean(x * x, axis=-1, keepdims=True) + EPS) * g


def _silu(x):
    return x * (1.0 / (1.0 + jnp.exp(-x)))


def _ada_kernel(c_ref, w_ref, b_ref, o_ref):
    s = _silu(c_ref[...]).astype(BF16)
    o_ref[0] = _dot(s, w_ref[0].astype(BF16)) + b_ref[0]


def _ada(cc, w_ada, b_ada):
    depth, d, n = w_ada.shape
    rows = cc.shape[0]
    tn = n // 4
    return pl.pallas_call(
        _ada_kernel,
        grid=(depth, n // tn),
        in_specs=[
            pl.BlockSpec((rows, d), lambda l, j: (0, 0)),
            pl.BlockSpec((1, d, tn), lambda l, j: (l, 0, j)),
            pl.BlockSpec((1, 1, tn), lambda l, j: (l, 0, j)),
        ],
        out_specs=pl.BlockSpec((1, rows, tn), lambda l, j: (l, 0, j)),
        out_shape=jax.ShapeDtypeStruct((depth, rows, n), F32),
        compiler_params=_params(("parallel", "parallel")),
        name="ada_mod",
    )(cc, w_ada, b_ada.reshape(depth, 1, n))


def _inproj_kernel(*refs, rope):
    if rope:
        (x_ref, mod_ref, g_ref, w_ref, gq_ref, wq_ref, gkv_ref, wkv_ref, c_ref, s_ref,
         ret_ref, u_ref, q_ref, k_ref, vt_ref) = refs
    else:
        (x_ref, mod_ref, g_ref, w_ref, gq_ref, wq_ref, gkv_ref, wkv_ref,
         ret_ref, u_ref, q_ref, k_ref, vt_ref) = refs
    nh = MLA_HEADS
    lane = lax.broadcasted_iota(jnp.int32, (1, HEAD_PAD), 1)
    nope_lanes = lane < MLA_NOPE
    rope_lanes = jnp.logical_and(lane >= MLA_NOPE, lane < MLA_NOPE + MLA_ROPE)
    sub = min(IN_SUB, x_ref.shape[1])
    den_rows = (lax.broadcasted_iota(jnp.int32, (VT_ROWS - MLA_V, sub), 0) == 0).astype(vt_ref.dtype)
    q_scale = (MLA_NOPE + MLA_ROPE) ** -0.5 * math.log2(math.e)

    def project(rows):
        h = _rms(x_ref[0, rows, :], g_ref[0]) * (1.0 + mod_ref[0, 0, 1:2, :]) + mod_ref[0, 0, 0:1, :]
        p = _dot(h.astype(BF16), w_ref[0])
        ret_ref[0, rows, :] = p[:, COL_RET:COL_CQ].astype(ret_ref.dtype)
        u_ref[0, rows, :] = p[:, COL_U:COL_KPE].astype(u_ref.dtype)
        return p

    def up_project(p, rows):
        yq = _rms(p[:, COL_CQ:COL_CKV], gq_ref[0]).astype(BF16)
        if rope:
            cos, sin = c_ref[rows, :], s_ref[rows, :]
            q2 = _dot(yq, wq_ref[0])
        else:
            q2 = _dot(yq, wq_ref[0, :, 0:UQ_ROT])
        ykv = _rms(p[:, COL_CKV:COL_U], gkv_ref[0]).astype(BF16)
        kv = _dot(ykv, wkv_ref[0])
        kx = p[:, COL_KPE:IN_COLS]
        kpe = pltpu.roll(kx, MLA_NOPE, 1)
        if rope:
            kpe = kpe * cos + pltpu.roll(kx, MLA_NOPE - MLA_ROPE, 1) * sin
        kpe = jnp.where(rope_lanes, kpe, 0.0)
        for hh in range(nh):
            qh = q2[:, hh * HEAD_PAD:(hh + 1) * HEAD_PAD]
            if rope:
                blk, j = divmod(hh * MLA_ROPE, HEAD_PAD)
                rot = q2[:, UQ_ROT + blk * HEAD_PAD:UQ_ROT + (blk + 1) * HEAD_PAD]
                shift = (MLA_NOPE - j) % HEAD_PAD
                qh = qh * cos + (pltpu.roll(rot, shift, 1) if shift else rot) * sin
            q_ref[0, hh, rows, :] = (qh * q_scale).astype(q_ref.dtype)
            blk, j = divmod(hh * MLA_NOPE, HEAD_PAD)
            kn = kv[:, blk * HEAD_PAD:(blk + 1) * HEAD_PAD]
            if j:
                kn = pltpu.roll(kn, HEAD_PAD - j, 1)
            k_ref[0, hh, rows, :] = jnp.where(nope_lanes, kn, kpe).astype(k_ref.dtype)
        for blk in range(nh * MLA_V // HEAD_PAD):
            vt = kv[:, UKV_V + blk * HEAD_PAD:UKV_V + (blk + 1) * HEAD_PAD].T.astype(vt_ref.dtype)
            for j in range(HEAD_PAD // MLA_V):
                hh = blk * (HEAD_PAD // MLA_V) + j
                vt_ref[0, hh, 0:MLA_V, rows] = vt[j * MLA_V:(j + 1) * MLA_V, :]
                vt_ref[0, hh, MLA_V:VT_ROWS, rows] = den_rows

    subs = [slice(r0, r0 + sub) for r0 in range(0, x_ref.shape[1], sub)]
    p_next = project(subs[0])
    for i, rows in enumerate(subs):
        p = p_next
        if i + 1 < len(subs):
            p_next = project(subs[i + 1])
        up_project(p, rows)


def _inproj(x, mod, mod_row, layer, g, w, gq, wq, gkv, wkv, tabs, tile):
    b, r, d = x.shape
    rope = tabs is not None
    nh = MLA_HEADS
    row = lambda width: pl.BlockSpec((1, tile, width), lambda i, j: (i, j, 0))
    head = pl.BlockSpec((1, nh, tile, HEAD_PAD), lambda i, j: (i, 0, j, 0))
    head_t = pl.BlockSpec((1, nh, VT_ROWS, tile), lambda i, j: (i, 0, 0, j))
    assert tile % min(IN_SUB, tile) == 0
    in_specs = [
        row(d),
        pl.BlockSpec((1, 1) + mod.shape[2:], lambda i, j: (layer, mod_row(i), 0, 0)),
    ] + [_layer_spec(a, layer) for a in (g, w, gq, wq, gkv, wkv)]
    args = [x, mod, g, w, gq, wq, gkv, wkv]
    if rope:
        in_specs += [pl.BlockSpec((tile, HEAD_PAD), lambda i, j: (j, 0))] * 2
        args += list(tabs)
    hshape = jax.ShapeDtypeStruct((b, nh, r, HEAD_PAD), BF16)
    return pl.pallas_call(
        functools.partial(_inproj_kernel, rope=rope),
        grid=(b, r // tile),
        in_specs=in_specs,
        out_specs=[row(4 * RET_W), row(POOL_W), head, head, head_t],
        out_shape=[jax.ShapeDtypeStruct((b, r, 4 * RET_W), BF16),
                   jax.ShapeDtypeStruct((b, r, POOL_W), BF16), hshape, hshape,
                   jax.ShapeDtypeStruct((b, nh, VT_ROWS, r), BF16)],
        compiler_params=_params(("parallel", "parallel")),
        name="in_proj",
    )(*args)


def _ret_pool_kernel(*refs, layer, n_lat, n_ctx, need_ctx):
    n_seg = 2 if need_ctx else 1
    it = iter(refs)
    lg_ref, rl_ref, rc_ref, c_ref, sa_ref, sb_ref = [next(it) for _ in range(6)]
    u_refs = [next(it) for _ in range(n_seg)]
    band_ref, w_ref, s_ref = next(it), next(it), next(it)
    ro_refs = [next(it) for _ in range(n_seg)]
    po_refs = [next(it) for _ in range(n_seg)]
    q_s, k_s, u_s, st_s, pad_s = it
    ch = RET_CHUNK
    pairs = RET_HEADS // 2
    n_all = n_lat + n_ctx
    lane = lax.broadcasted_iota(jnp.int32, (1, LANES), 1)
    lo = lane < RET_DIM
    sub_lo = lax.broadcasted_iota(jnp.int32, (LANES, 1), 0) < RET_DIM
    row = lax.broadcasted_iota(jnp.int32, (ch, 1), 0).astype(F32)
    col = lax.broadcasted_iota(jnp.int32, (1, ch), 1).astype(F32)
    diff = row - col
    head_of_row = lax.broadcasted_iota(jnp.int32, (LANES, 1), 0) // RET_DIM
    blockdiag = (head_of_row == lane // RET_DIM).astype(F32)
    k_scale = RET_DIM ** -0.5
    half = RET_DIM // 2

    def pair_tables(p):
        lf_a, lf_b = lg_ref[layer, 0, 2 * p], lg_ref[layer, 0, 2 * p + 1]
        lb_a, lb_b = lg_ref[layer, 1, 2 * p], lg_ref[layer, 1, 2 * p + 1]
        lf = jnp.where(lo, lf_a, lf_b)
        lb = jnp.where(lo, lb_a, lb_b)
        lf_t = jnp.where(sub_lo, lf_a, lf_b)
        lb_t = jnp.where(sub_lo, lb_a, lb_b)

        def decay(l_f, l_b):
            return jnp.where(diff >= 0, jnp.exp(l_f * jnp.maximum(diff, 0.0)),
                             jnp.exp(l_b * jnp.maximum(-diff, 0.0)))

        return dict(
            d_ab=jnp.concatenate([decay(lf_a, lb_a), decay(lf_b, lb_b)], axis=0) * k_scale,
            wq_fb=jnp.concatenate([jnp.exp(lf * (row + 1.0)), jnp.exp(lb * (ch - row))], axis=1),
            wk_fb_t=jnp.concatenate([jnp.exp(lf_t * (ch - 1.0 - col)), jnp.exp(lb_t * col)], axis=0) * k_scale,
            gf=jnp.exp(lf * ch), gb=jnp.exp(lb * ch))

    tables = [pair_tables(p) for p in range(pairs)]

    def source(n):
        if n < n_lat:
            return rl_ref, slice(n * ch, (n + 1) * ch), True
        return rc_ref, slice((n - n_lat) * ch, (n - n_lat + 1) * ch), False

    def part(p, which):
        c0 = (which * pairs + p) * LANES
        return slice(c0, c0 + LANES)

    lo_pool = lane < POOL_GDIM
    halo = POOL_HALO

    def pool_fill(u_ref, length):
        pad_s[0:halo, :] = jnp.zeros((halo, POOL_W), BF16)
        pad_s[halo:halo + length, :] = u_ref[0]
        pad_s[halo + length:2 * halo + length, :] = jnp.zeros((halo, POOL_W), BF16)

    def pool_slab(u_ref, o_ref, length, r0):
        t = (lax.broadcasted_iota(jnp.int32, (POOL_TILE, 1), 0) + r0).astype(F32)
        cols = []
        for hf in range(POOL_W // LANES):
            cs = slice(hf * LANES, (hf + 1) * LANES)
            tokens = pad_s[r0:r0 + POOL_TILE + 2 * halo, cs]

            def inv_count(w):
                if r0 - w // 2 >= 0 and r0 + POOL_TILE - w // 2 + w <= length:
                    return 1.0 / w
                cnt = (jnp.clip(t - w // 2 + w, 0.0, float(length))
                       - jnp.clip(t - w // 2, 0.0, float(length)))
                return 1.0 / cnt

            w_n, w_w = POOL_WINDOWS[2 * hf], POOL_WINDOWS[2 * hf + 1]
            pooled = jnp.where(lo_pool, _dot(band_ref[2 * hf], tokens) * inv_count(w_n),
                               _dot(band_ref[2 * hf + 1], tokens) * inv_count(w_w))
            cols.append(pooled - u_ref[0, r0:r0 + POOL_TILE, cs].astype(F32))
        pooled = jnp.concatenate(cols, axis=-1).astype(BF16)
        y = _dot(pooled, w_ref[0]) * s_ref[0]
        o_ref[0, r0:r0 + POOL_TILE, :] = y.astype(o_ref.dtype)

    pool_tasks = []
    for u_ref, o_ref in zip(u_refs, po_refs):
        length = u_ref.shape[1]
        pool_tasks.append(functools.partial(pool_fill, u_ref, length))
        pool_tasks += [functools.partial(pool_slab, u_ref, o_ref, length, r0)
                       for r0 in range(0, length, POOL_TILE)]
    pool_tasks.reverse()

    for n in range(n_all):
        src_ref, src_rows, roped = source(n)
        rows = slice(n * ch, (n + 1) * ch)
        for p in range(pairs):
            q = src_ref[0, src_rows, part(p, 0)].astype(F32)
            k = src_ref[0, src_rows, part(p, 1)].astype(F32)
            if roped:
                c, sa, sb = c_ref[src_rows, :], sa_ref[src_rows, :], sb_ref[src_rows, :]
                q = q * c + pltpu.roll(q, LANES - half, 1) * sa + pltpu.roll(q, half, 1) * sb
                k = k * c + pltpu.roll(k, LANES - half, 1) * sa + pltpu.roll(k, half, 1) * sb
            q_s[p, rows, :] = q
            k_s[p, rows, :] = k
            kt = k.T
            kt2 = jnp.concatenate([kt, kt], axis=0) * tables[p]["wk_fb_t"]
            u_s[p, n] = _dot(kt2.astype(BF16), src_ref[0, src_rows, part(p, 2)])

    order_f = list(range(n_lat, n_all)) + list(range(n_lat))
    order_b = list(range(n_all - 1, -1, -1))
    for p in range(pairs):
        s = jnp.zeros((LANES, LANES), F32)
        for n in order_f:
            st_s[p, n, 0:LANES, :] = (s * blockdiag).astype(BF16)
            s = tables[p]["gf"] * s + u_s[p, n, 0:LANES, :]
        s = jnp.zeros((LANES, LANES), F32)
        for n in order_b:
            st_s[p, n, LANES:2 * LANES, :] = (s * blockdiag).astype(BF16)
            s = tables[p]["gb"] * s + u_s[p, n, LANES:2 * LANES, :]

    n_out = n_all if need_ctx else n_lat

    def scores_and_state(n):
        rows = slice(n * ch, (n + 1) * ch)
        out = []
        for p in range(pairs):
            q = q_s[p, rows, :]
            kb = k_s[p, rows, :].astype(BF16)
            q_ab = jnp.concatenate([jnp.where(lo, q, 0.0), jnp.where(lo, 0.0, q)], axis=0).astype(BF16)
            a_ab = _dot_nt(q_ab, kb)
            carried = _dot((jnp.concatenate([q, q], axis=1) * tables[p]["wq_fb"]).astype(BF16), st_s[p, n])
            out.append((a_ab, carried))
        return out

    def finish(n, staged):
        src_ref, src_rows, _ = source(n)
        dst_ref = ro_refs[0] if n < n_lat else ro_refs[1]
        for p, (a_ab, carried) in enumerate(staged):
            vb = src_ref[0, src_rows, part(p, 2)]
            gate = src_ref[0, src_rows, part(p, 3)].astype(F32)
            o_ab = _dot((a_ab * tables[p]["d_ab"]).astype(BF16), vb)
            o = jnp.where(lo, o_ab[0:ch], o_ab[ch:2 * ch]) + carried
            inv = 1.0 / RET_DIM
            mu = jnp.where(lo, jnp.sum(jnp.where(lo, o, 0.0), axis=-1, keepdims=True),
                           jnp.sum(jnp.where(lo, 0.0, o), axis=-1, keepdims=True)) * inv
            dlt = o - mu
            sq = dlt * dlt
            var = jnp.where(lo, jnp.sum(jnp.where(lo, sq, 0.0), axis=-1, keepdims=True),
                            jnp.sum(jnp.where(lo, 0.0, sq), axis=-1, keepdims=True)) * inv
            dst_ref[0, src_rows, p * LANES:(p + 1) * LANES] = (
                _silu(gate) * (dlt * lax.rsqrt(var + EPS))).astype(dst_ref.dtype)

    staged = scores_and_state(0)
    for n in range(n_out):
        staged_next = scores_and_state(n + 1) if n + 1 < n_out else None
        if pool_tasks:
            pool_tasks.pop()()
        finish(n, staged)
        staged = staged_next
    while pool_tasks:
        pool_tasks.pop()()


def _ret_pool(lg, layer, ret_lat, ret_ctx, tabs, us, band, w_bd, scale, need_ctx):
    b, rl, _ = ret_lat.shape
    rc = ret_ctx.shape[1]
    ch = RET_CHUNK
    n_lat, n_ctx = rl // ch, rc // ch
    n_all = n_lat + n_ctx
    pairs = RET_HEADS // 2
    lengths = tuple(u.shape[1] for u in us)
    assert len(us) == (2 if need_ctx else 1)
    kern = functools.partial(_ret_pool_kernel, layer=layer, n_lat=n_lat, n_ctx=n_ctx, need_ctx=need_ctx)
    tab_spec = pl.BlockSpec((rl, LANES), lambda i: (0, 0), pipeline_mode=pl.Buffered(1))
    rows = lambda n, width: pl.BlockSpec((1, n, width), lambda i: (i, 0, 0))
    out_specs = [rows(n, RET_W) for n in lengths] + [rows(n, POOL_W) for n in lengths]
    out_shape = ([jax.ShapeDtypeStruct((b, n, RET_W), BF16) for n in lengths]
                 + [jax.ShapeDtypeStruct((b, n, POOL_W), BF16) for n in lengths])
    outs = pl.pallas_call(
        kern,
        grid=(b,),
        in_specs=[
            pl.BlockSpec(memory_space=pltpu.SMEM),
            rows(rl, 4 * RET_W), rows(rc, 4 * RET_W),
            tab_spec, tab_spec, tab_spec,
            *[rows(n, POOL_W) for n in lengths],
            pl.BlockSpec(band.shape, lambda i: (0, 0, 0), pipeline_mode=pl.Buffered(1)),
            _layer_spec(w_bd, layer), _layer_spec(scale, layer),
        ],
        out_specs=out_specs,
        out_shape=out_shape,
        scratch_shapes=[
            pltpu.VMEM((pairs, rl + rc, LANES), F32), pltpu.VMEM((pairs, rl + rc, LANES), F32),
            pltpu.VMEM((pairs, n_all, 2 * LANES, LANES), F32),
            pltpu.VMEM((pairs, n_all, 2 * LANES, LANES), BF16),
            pltpu.VMEM((max(lengths) + 2 * POOL_HALO, POOL_W), BF16),
        ],
        compiler_params=_params(("parallel",)),
        name="retention_pool",
    )(lg, ret_lat, ret_ctx, *tabs, *us, band, w_bd, scale)
    n = len(lengths)
    return outs[:n], outs[n:]


def _attn_kernel(*refs, use_lat):
    if use_lat:
        q_ref, kl_ref, vtl_ref, kc_ref, vtc_ref, o_ref = refs
    else:
        q_ref, kc_ref, vtc_ref, o_ref = refs

    sources = [(kc_ref, vtc_ref)]
    if use_lat:
        sources.append((kl_ref, vtl_ref))

    n_sub = q_ref.shape[2] // Q_SUB
    items = [((qs, h), k_ref, vt_ref, c0) for qs in range(n_sub) for h in range(MLA_HEADS)
             for k_ref, vt_ref in sources for c0 in range(0, k_ref.shape[2], KEY_CHUNK)]
    last_item = {g: i for i, (g, _, _, _) in enumerate(items)}

    scores, m_run, acc, done = {}, {}, {}, {}
    for t in range(len(items) + SCORE_LOOKAHEAD):
        if t < len(items):
            (qs, h), k_ref, _, c0 = items[t]
            q = q_ref[0, h, qs * Q_SUB:(qs + 1) * Q_SUB, :]
            scores[t] = _dot_nt(k_ref[0, h, c0:c0 + KEY_CHUNK, :], q)
        i = t - SCORE_LOOKAHEAD
        if i < 0:
            continue
        g, _, vt_ref, c0 = items[i]
        qs, h = g
        s = scores.pop(i)
        m_new = jnp.max(s, axis=0, keepdims=True)
        if g in m_run:
            m_new = jnp.maximum(m_run[g], m_new)
        pv = _dot(vt_ref[0, h, :, c0:c0 + KEY_CHUNK], jnp.exp2(s - m_new).astype(BF16))
        acc[g] = pv if g not in acc else acc[g] * jnp.exp2(m_run[g] - m_new) + pv
        m_run[g] = m_new
        if i == last_item[g]:
            a = acc.pop(g)
            done[g] = a[0:MLA_V, :] * (1.0 / a[MLA_V:MLA_V + 1, :])
            if h % 2 == 1:
                pair_t = jnp.concatenate([done.pop((qs, h - 1)), done.pop(g)], axis=0)
                o_ref[0, qs * Q_SUB:(qs + 1) * Q_SUB, (h // 2) * HEAD_PAD:(h // 2 + 1) * HEAD_PAD] = (
                    pair_t.T.astype(o_ref.dtype))


def _attention(q, k_ctx, vt_ctx, n_ctx, k_lat=None, vt_lat=None):
    nh = q.shape[1]
    b = k_ctx.shape[2] // n_ctx
    use_lat = k_lat is not None
    rows = q.shape[2] if use_lat else n_ctx
    tile = min(ATTN_TILE, rows)

    def whole(a):
        return pl.BlockSpec((1,) + a.shape[1:], lambda i, j: (i, 0, 0, 0))

    if use_lat:
        in_specs = [pl.BlockSpec((1, nh, tile, HEAD_PAD), lambda i, j: (i, 0, j, 0)), whole(k_lat), whole(vt_lat)]
        args = [q, k_lat, vt_lat]
    else:
        in_specs = [pl.BlockSpec((1, nh, tile, HEAD_PAD), lambda i, j: (0, 0, i * (rows // tile) + j, 0))]
        args = [q]
    in_specs += [pl.BlockSpec((1, nh, n_ctx, HEAD_PAD), lambda i, j: (0, 0, i, 0)),
                 pl.BlockSpec((1, nh, VT_ROWS, n_ctx), lambda i, j: (0, 0, 0, i))]
    args += [k_ctx, vt_ctx]
    return pl.pallas_call(
        functools.partial(_attn_kernel, use_lat=use_lat),
        grid=(b, rows // tile),
        in_specs=in_specs,
        out_specs=pl.BlockSpec((1, tile, MLA_W), lambda i, j: (i, j, 0)),
        out_shape=jax.ShapeDtypeStruct((b, rows, MLA_W), BF16),
        compiler_params=_params(("parallel", "arbitrary")),
        name="mla_attention",
    )(*args)


def _mix_mlp_kernel(x_ref, r_ref, m_ref, p_ref, wo_ref, mod_ref, g_ref, w1_ref, w2_ref, gf_ref, o_ref,
                    *, final_norm):
    def mix(rows):
        y = (_dot(r_ref[0, rows, :], wo_ref[0, 0:RET_W, :])
             + _dot(m_ref[0, rows, :], wo_ref[0, RET_W:RET_W + MLA_W, :])
             + _dot(p_ref[0, rows, :], wo_ref[0, RET_W + MLA_W:, :]))
        x = x_ref[0, rows, :] + mod_ref[0, 0, 2:3, :] * y
        h = (_rms(x, g_ref[0]) * (1.0 + mod_ref[0, 0, 4:5, :]) + mod_ref[0, 0, 3:4, :]).astype(BF16)
        return x, h

    def mlp(x, h, rows):
        acc = None
        for c0 in range(0, w1_ref.shape[2], FF_CHUNK):
            a = jnp.maximum(_dot(h, w1_ref[0, :, c0:c0 + FF_CHUNK]), 0.0)
            part = _dot((a * a).astype(BF16), w2_ref[0, c0:c0 + FF_CHUNK, :])
            acc = part if acc is None else acc + part
        y = x + mod_ref[0, 0, 5:6, :] * acc
        if final_norm:
            y = _rms(y, gf_ref[...])
        o_ref[0, rows, :] = y

    sub = min(MLP_SUB, x_ref.shape[1])
    subs = [slice(r0, r0 + sub) for r0 in range(0, x_ref.shape[1], sub)]
    staged = mix(subs[0])
    for i, rows in enumerate(subs):
        x, h = staged
        if i + 1 < len(subs):
            staged = mix(subs[i + 1])
        mlp(x, h, rows)


def _mix_mlp(x, ret, mla, pool, mod, mod_row, layer, w_out, g, w1, w2, g_final, tile, final_norm):
    b, r, d = x.shape
    row = lambda width: pl.BlockSpec((1, tile, width), lambda i, j: (i, j, 0))
    return pl.pallas_call(
        functools.partial(_mix_mlp_kernel, final_norm=final_norm),
        grid=(b, r // tile),
        in_specs=[
            row(d), row(RET_W), row(MLA_W), row(POOL_W),
            _layer_spec(w_out, layer),
            pl.BlockSpec((1, 1) + mod.shape[2:], lambda i, j: (layer, mod_row(i), 0, 0)),
            _layer_spec(g, layer), _layer_spec(w1, layer), _layer_spec(w2, layer),
            pl.BlockSpec(g_final.shape, lambda i, j: (0, 0)),
        ],
        out_specs=row(d),
        out_shape=jax.ShapeDtypeStruct((b, r, d), F32),
        compiler_params=_params(("parallel", "parallel")),
        name="mix_mlp",
    )(x, ret, mla, pool, w_out, mod, g, w1, w2, g_final)


def _prep_w_uq(w_uq):
    depth, rank, _ = w_uq.shape
    half = MLA_ROPE // 2
    heads = w_uq.astype(BF16).reshape(depth, rank, MLA_HEADS, MLA_NOPE + MLA_ROPE)
    main = jnp.pad(heads, ((0, 0), (0, 0), (0, 0), (0, HEAD_PAD - MLA_NOPE - MLA_ROPE)))
    rope = heads[..., MLA_NOPE:]
    partner = jnp.concatenate([-rope[..., half:], rope[..., :half]], axis=-1)
    return jnp.concatenate([main.reshape(depth, rank, UQ_ROT),
                            partner.reshape(depth, rank, MLA_HEADS * MLA_ROPE)], axis=-1)


def _prep_w_ukv(w_ukv):
    depth, rank, _ = w_ukv.shape
    heads = w_ukv.astype(BF16).reshape(depth, rank, MLA_HEADS, MLA_NOPE + MLA_V)
    return jnp.concatenate([heads[..., :MLA_NOPE].reshape(depth, rank, UKV_V),
                            heads[..., MLA_NOPE:].reshape(depth, rank, UKV_COLS - UKV_V)], axis=-1)


def _prep_w_in_kernel(w_ref, o_ref):
    half = MLA_ROPE // 2
    o_ref[0, :, 0:COL_U] = w_ref[0, :, 0:COL_U].astype(o_ref.dtype)
    o_ref[0, :, COL_U:COL_KPE] = w_ref[0, :, COL_U + MLA_ROPE:COL_U + MLA_ROPE + POOL_W].astype(o_ref.dtype)
    v = w_ref[0, :, COL_U:COL_U + HEAD_PAD]
    lane = lax.broadcasted_iota(jnp.int32, (1, HEAD_PAD), 1)
    kx = jnp.where(lane < MLA_ROPE, v,
                   jnp.where(lane < MLA_ROPE + half, -pltpu.roll(v, half, 1),
                             jnp.where(lane < 2 * MLA_ROPE, pltpu.roll(v, MLA_ROPE + half, 1), 0.0)))
    o_ref[0, :, COL_KPE:IN_COLS] = kx.astype(o_ref.dtype)


def _prep_w_in(w_in):
    depth, d, n = w_in.shape
    assert n == COL_U + MLA_ROPE + POOL_W
    tile = 256
    return pl.pallas_call(
        _prep_w_in_kernel,
        grid=(depth, d // tile),
        in_specs=[pl.BlockSpec((1, tile, n), lambda l, i: (l, i, 0))],
        out_specs=pl.BlockSpec((1, tile, IN_COLS), lambda l, i: (l, i, 0)),
        out_shape=jax.ShapeDtypeStruct((depth, d, IN_COLS), BF16),
        compiler_params=_params(("parallel", "parallel")),
        name="prep_w_in",
    )(w_in)


def _rope_angles(n_rows, dim):
    pos = np.arange(n_rows)
    n_freq = dim // 4
    inv = np.float32(ROPE_BASE) ** (-np.arange(n_freq, dtype=np.float32) / np.float32(n_freq))
    return np.concatenate([(pos // GRID_W).astype(np.float32)[:, None] * inv,
                           (pos % GRID_W).astype(np.float32)[:, None] * inv], axis=-1)


def _ret_rope_tables(n_rows):
    ang = _rope_angles(n_rows, RET_DIM)
    cos, sin = np.cos(ang), np.sin(ang)
    zero = np.zeros_like(sin)
    reps = LANES // RET_DIM
    c = np.tile(np.concatenate([cos, cos], axis=-1), (1, reps))
    sa = np.tile(np.concatenate([-sin, zero], axis=-1), (1, reps))
    sb = np.tile(np.concatenate([zero, sin], axis=-1), (1, reps))
    return tuple(jnp.asarray(a, F32) for a in (c, sa, sb))


def _mla_rope_tables(n_rows):
    ang = _rope_angles(n_rows, MLA_ROPE)
    cos, sin = np.cos(ang), np.sin(ang)
    c = np.ones((n_rows, HEAD_PAD), np.float32)
    s = np.zeros((n_rows, HEAD_PAD), np.float32)
    c[:, MLA_NOPE:MLA_NOPE + MLA_ROPE] = np.concatenate([cos, cos], axis=-1)
    s[:, MLA_NOPE:MLA_NOPE + MLA_ROPE] = np.concatenate([sin, sin], axis=-1)
    return jnp.asarray(c, F32), jnp.asarray(s, F32)


def _pool_bands():
    t = np.arange(POOL_TILE)[:, None]
    s = np.arange(POOL_TILE + 2 * POOL_HALO)[None, :]
    bands = [(s >= t + POOL_HALO - w // 2) & (s < t + POOL_HALO - w // 2 + w) for w in POOL_WINDOWS]
    return jnp.asarray(np.stack(bands).astype(np.float32), BF16)


def _block_diag(w_pool):
    depth, g, c, _ = w_pool.shape
    rows = []
    for i in range(g):
        blocks = [w_pool[:, i] if j == i else jnp.zeros((depth, c, c), w_pool.dtype) for j in range(g)]
        rows.append(jnp.concatenate(blocks, axis=2))
    return jnp.concatenate(rows, axis=1)


def kernel(x, c, ctx, c_ctx, w_ada, b_ada, norm_mix, w_in, q_norm, w_uq, kv_norm, w_ukv,
           ret_decay_logit, w_pool, pool_scale, w_out, norm_mlp, w_ff1, w_ff2, norm_final):
    b, n_lat, d = x.shape
    n_ctx = ctx.shape[1]
    depth = w_ada.shape[0]
    lat_tile, mlp_tile = min(LAT_TILE, n_lat), min(MLP_TILE, n_lat)
    assert n_lat % lat_tile == 0 and n_lat % mlp_tile == 0 and n_lat % min(ATTN_TILE, n_lat) == 0
    assert n_lat % GRID_W == 0 and n_lat % Q_SUB == 0
    assert n_ctx % RET_CHUNK == 0 and n_ctx % POOL_TILE == 0 and n_lat % POOL_TILE == 0
    assert d == 4 * RET_W and w_ff1.shape[2] % FF_CHUNK == 0

    ret_tabs = _ret_rope_tables(n_lat)
    mla_tabs = _mla_rope_tables(n_lat)

    rows = -(-(b + 1) // 8) * 8
    cc = jnp.zeros((rows, d), F32).at[:b].set(c).at[b].set(c_ctx)
    mod = _ada(cc, w_ada, b_ada).reshape(depth, rows, 6, d)
    lat_row, ctx_row = (lambda i: i), (lambda i: b)
    lg = jax.nn.log_sigmoid(ret_decay_logit.astype(F32))

    def rows3(a):
        return a.reshape(depth, 1, a.shape[-1])

    proj_w = (rows3(norm_mix), _prep_w_in(w_in), rows3(q_norm), _prep_w_uq(w_uq),
              rows3(kv_norm), _prep_w_ukv(w_ukv))
    pool_w = (_pool_bands(), _block_diag(w_pool).astype(BF16), rows3(pool_scale))
    mlp_w = (w_out.astype(BF16), rows3(norm_mlp), w_ff1.astype(BF16), w_ff2.astype(BF16), norm_final[None])

    flat = lambda a: a.reshape(1, b * n_ctx, a.shape[-1])
    per_batch = lambda a: a.reshape(b, n_ctx, a.shape[-1])
    ctx_tile, ctx_mlp_tile = min(LAT_TILE, b * n_ctx), min(MLP_TILE, b * n_ctx)
    assert (b * n_ctx) % ctx_tile == 0 and (b * n_ctx) % ctx_mlp_tile == 0
    h_ctx = flat(ctx)
    for l in range(depth):
        last = l == depth - 1
        ret_l, u_l, q_l, k_l, vt_l = _inproj(x, mod, lat_row, l, *proj_w, mla_tabs, lat_tile)
        ret_c, u_c, q_c, k_c, vt_c = _inproj(h_ctx, mod, ctx_row, l, *proj_w, None, ctx_tile)
        us = (u_l,) if last else (u_l, per_batch(u_c))
        ret_o, pool_o = _ret_pool(lg, l, ret_l, per_batch(ret_c), ret_tabs, us, *pool_w, not last)
        mla_l = _attention(q_l, k_c, vt_c, n_ctx, k_l, vt_l)
        if not last:
            mla_c = _attention(q_c, k_c, vt_c, n_ctx)
            h_ctx = _mix_mlp(h_ctx, flat(ret_o[1]), flat(mla_c), flat(pool_o[1]), mod, ctx_row, l, *mlp_w,
                             ctx_mlp_tile, False)
        x = _mix_mlp(x, ret_o[0], mla_l, pool_o[0], mod, lat_row, l, *mlp_w, mlp_tile, last)
    return x
```

```python
import functools
import math

import jax
import jax.numpy as jnp
import numpy as np
from jax import lax
from jax.experimental import pallas as pl
from jax.experimental.pallas import tpu as pltpu

GRID_W = 64
RET_HEADS = 4
RET_DIM = 64
RET_W = RET_HEADS * RET_DIM
RET_CHUNK = 128
MLA_HEADS = 8
MLA_NOPE = 64
MLA_ROPE = 32
MLA_V = 64
MLA_Q_RANK = 256
MLA_KV_RANK = 128
MLA_W = MLA_HEADS * MLA_V
POOL_GROUPS = 4
POOL_WINDOWS = (2, 4, 8, 16)
POOL_GDIM = 64
POOL_W = POOL_GROUPS * POOL_GDIM
ROPE_BASE = 10000.0
EPS = 1e-6

LANES = 128
LAT_TILE = 1024
MLP_TILE = 1024
IN_SUB = 256
ATTN_TILE = 1024
Q_SUB = 256
KEY_CHUNK = 256
SCORE_LOOKAHEAD = 5
POOL_TILE = 256
POOL_EVERY = 2
HEAD_PAD = 128
VT_ROWS = MLA_V + 16
POOL_HALO = 128
FF_CHUNK = 1024
MLP_SUB = 256
VMEM_LIMIT = 56 * 1024 * 1024

COL_RET = 0
COL_CQ = 4 * RET_W
COL_CKV = COL_CQ + MLA_Q_RANK
COL_U = COL_CKV + MLA_KV_RANK
COL_KPE = COL_U + POOL_W
IN_COLS = COL_KPE + HEAD_PAD
UQ_ROT = MLA_HEADS * HEAD_PAD
UQ_COLS = UQ_ROT + MLA_HEADS * MLA_ROPE
UKV_V = MLA_HEADS * MLA_NOPE
UKV_COLS = UKV_V + MLA_HEADS * MLA_V

F32 = jnp.float32
BF16 = jnp.bfloat16


def _params(sem):
    return pltpu.CompilerParams(dimension_semantics=sem, vmem_limit_bytes=VMEM_LIMIT)


def _layer_spec(a, layer):
    tail = a.shape[1:]
    return pl.BlockSpec((1,) + tail, lambda *_: (layer,) + (0,) * len(tail), pipeline_mode=pl.Buffered(1))


def _dot(a, b):
    return jnp.dot(a, b, preferred_element_type=F32)


def _dot_nt(a, b):
    return lax.dot_general(a, b, (((1,), (1,)), ((), ())), preferred_element_type=F32)


def _rms(x, g):
    return x * lax.rsqrt(jnp.mean(x * x, axis=-1, keepdims=True) + EPS) * g


def _silu(x):
    return x * (1.0 / (1.0 + jnp.exp(-x)))


def _ada_kernel(c_ref, w_ref, b_ref, o_ref):
    s = _silu(c_ref[...]).astype(BF16)
    o_ref[0] = _dot(s, w_ref[0].astype(BF16)) + b_ref[0]


def _ada(cc, w_ada, b_ada):
    depth, d, n = w_ada.shape
    rows = cc.shape[0]
    tn = n // 4
    return pl.pallas_call(
        _ada_kernel,
        grid=(depth, n // tn),
        in_specs=[
            pl.BlockSpec((rows, d), lambda l, j: (0, 0)),
            pl.BlockSpec((1, d, tn), lambda l, j: (l, 0, j)),
            pl.BlockSpec((1, 1, tn), lambda l, j: (l, 0, j)),
        ],
        out_specs=pl.BlockSpec((1, rows, tn), lambda l, j: (l, 0, j)),
        out_shape=jax.ShapeDtypeStruct((depth, rows, n), F32),
        compiler_params=_params(("parallel", "parallel")),
        name="ada_mod",
    )(cc, w_ada, b_ada.reshape(depth, 1, n))


def _inproj_kernel(*refs, rope):
    if rope:
        (x_ref, mod_ref, g_ref, w_ref, gq_ref, wq_ref, gkv_ref, wkv_ref, c_ref, s_ref,
         ret_ref, u_ref, q_ref, k_ref, vt_ref) = refs
    else:
        (x_ref, mod_ref, g_ref, w_ref, gq_ref, wq_ref, gkv_ref, wkv_ref,
         ret_ref, u_ref, q_ref, k_ref, vt_ref) = refs
    nh = MLA_HEADS
    lane = lax.broadcasted_iota(jnp.int32, (1, HEAD_PAD), 1)
    nope_lanes = lane < MLA_NOPE
    rope_lanes = jnp.logical_and(lane >= MLA_NOPE, lane < MLA_NOPE + MLA_ROPE)
    sub = min(IN_SUB, x_ref.shape[1])
    den_rows = (lax.broadcasted_iota(jnp.int32, (VT_ROWS - MLA_V, sub), 0) == 0).astype(vt_ref.dtype)
    q_scale = (MLA_NOPE + MLA_ROPE) ** -0.5 * math.log2(math.e)

    def project(rows):
        h = _rms(x_ref[0, rows, :], g_ref[0]) * (1.0 + mod_ref[0, 0, 1:2, :]) + mod_ref[0, 0, 0:1, :]
        p = _dot(h.astype(BF16), w_ref[0])
        ret_ref[0, rows, :] = p[:, COL_RET:COL_CQ].astype(ret_ref.dtype)
        u_ref[0, rows, :] = p[:, COL_U:COL_KPE].astype(u_ref.dtype)
        return p

    def up_project(p, rows):
        yq = _rms(p[:, COL_CQ:COL_CKV], gq_ref[0]).astype(BF16)
        if rope:
            cos, sin = c_ref[rows, :], s_ref[rows, :]
            q2 = _dot(yq, wq_ref[0])
        else:
            q2 = _dot(yq, wq_ref[0, :, 0:UQ_ROT])
        ykv = _rms(p[:, COL_CKV:COL_U], gkv_ref[0]).astype(BF16)
        kv = _dot(ykv, wkv_ref[0])
        kx = p[:, COL_KPE:IN_COLS]
        kpe = pltpu.roll(kx, MLA_NOPE, 1)
        if rope:
            kpe = kpe * cos + pltpu.roll(kx, MLA_NOPE - MLA_ROPE, 1) * sin
        kpe = jnp.where(rope_lanes, kpe, 0.0)
        for hh in range(nh):
            qh = q2[:, hh * HEAD_PAD:(hh + 1) * HEAD_PAD]
            if rope:
                blk, j = divmod(hh * MLA_ROPE, HEAD_PAD)
                rot = q2[:, UQ_ROT + blk * HEAD_PAD:UQ_ROT + (blk + 1) * HEAD_PAD]
                shift = (MLA_NOPE - j) % HEAD_PAD
                qh = qh * cos + (pltpu.roll(rot, shift, 1) if shift else rot) * sin
            q_ref[0, hh, rows, :] = (qh * q_scale).astype(q_ref.dtype)
            blk, j = divmod(hh * MLA_NOPE, HEAD_PAD)
            kn = kv[:, blk * HEAD_PAD:(blk + 1) * HEAD_PAD]
            if j:
                kn = pltpu.roll(kn, HEAD_PAD - j, 1)
            k_ref[0, hh, rows, :] = jnp.where(nope_lanes, kn, kpe).astype(k_ref.dtype)
        for blk in range(nh * MLA_V // HEAD_PAD):
            vt = kv[:, UKV_V + blk * HEAD_PAD:UKV_V + (blk + 1) * HEAD_PAD].T.astype(vt_ref.dtype)
            for j in range(HEAD_PAD // MLA_V):
                hh = blk * (HEAD_PAD // MLA_V) + j
                vt_ref[0, hh, 0:MLA_V, rows] = vt[j * MLA_V:(j + 1) * MLA_V, :]
                vt_ref[0, hh, MLA_V:VT_ROWS, rows] = den_rows

    subs = [slice(r0, r0 + sub) for r0 in range(0, x_ref.shape[1], sub)]
    p_next = project(subs[0])
    for i, rows in enumerate(subs):
        p = p_next
        if i + 1 < len(subs):
            p_next = project(subs[i + 1])
        up_project(p, rows)


def _inproj(x, mod, mod_row, layer, g, w, gq, wq, gkv, wkv, tabs, tile):
    b, r, d = x.shape
    rope = tabs is not None
    nh = MLA_HEADS
    row = lambda width: pl.BlockSpec((1, tile, width), lambda i, j: (i, j, 0))
    head = pl.BlockSpec((1, nh, tile, HEAD_PAD), lambda i, j: (i, 0, j, 0))
    head_t = pl.BlockSpec((1, nh, VT_ROWS, tile), lambda i, j: (i, 0, 0, j))
    assert tile % min(IN_SUB, tile) == 0
    in_specs = [
        row(d),
        pl.BlockSpec((1, 1) + mod.shape[2:], lambda i, j: (layer, mod_row(i), 0, 0)),
    ] + [_layer_spec(a, layer) for a in (g, w, gq, wq, gkv, wkv)]
    args = [x, mod, g, w, gq, wq, gkv, wkv]
    if rope:
        in_specs += [pl.BlockSpec((tile, HEAD_PAD), lambda i, j: (j, 0))] * 2
        args += list(tabs)
    hshape = jax.ShapeDtypeStruct((b, nh, r, HEAD_PAD), BF16)
    return pl.pallas_call(
        functools.partial(_inproj_kernel, rope=rope),
        grid=(b, r // tile),
        in_specs=in_specs,
        out_specs=[row(4 * RET_W), row(POOL_W), head, head, head_t],
        out_shape=[jax.ShapeDtypeStruct((b, r, 4 * RET_W), BF16),
                   jax.ShapeDtypeStruct((b, r, POOL_W), BF16), hshape, hshape,
                   jax.ShapeDtypeStruct((b, nh, VT_ROWS, r), BF16)],
        compiler_params=_params(("parallel", "parallel")),
        name="in_proj",
    )(*args)


def _ret_pool_kernel(*refs, layer, n_lat, n_ctx, need_ctx):
    n_seg = 2 if need_ctx else 1
    it = iter(refs)
    lg_ref, rl_ref, rc_ref, c_ref, sa_ref, sb_ref = [next(it) for _ in range(6)]
    u_refs = [next(it) for _ in range(n_seg)]
    band_ref, w_ref, s_ref = next(it), next(it), next(it)
    ro_refs = [next(it) for _ in range(n_seg)]
    po_refs = [next(it) for _ in range(n_seg)]
    q_s, k_s, u_s, st_s, pad_s = it
    ch = RET_CHUNK
    pairs = RET_HEADS // 2
    n_all = n_lat + n_ctx
    lane = lax.broadcasted_iota(jnp.int32, (1, LANES), 1)
    lo = lane < RET_DIM
    sub_lo = lax.broadcasted_iota(jnp.int32, (LANES, 1), 0) < RET_DIM
    row = lax.broadcasted_iota(jnp.int32, (ch, 1), 0).astype(F32)
    col = lax.broadcasted_iota(jnp.int32, (1, ch), 1).astype(F32)
    diff = row - col
    head_of_row = lax.broadcasted_iota(jnp.int32, (LANES, 1), 0) // RET_DIM
    blockdiag = (head_of_row == lane // RET_DIM).astype(F32)
    k_scale = RET_DIM ** -0.5
    half = RET_DIM // 2

    def pair_tables(p):
        lf_a, lf_b = lg_ref[layer, 0, 2 * p], lg_ref[layer, 0, 2 * p + 1]
        lb_a, lb_b = lg_ref[layer, 1, 2 * p], lg_ref[layer, 1, 2 * p + 1]
        lf = jnp.where(lo, lf_a, lf_b)
        lb = jnp.where(lo, lb_a, lb_b)
        lf_t = jnp.where(sub_lo, lf_a, lf_b)
        lb_t = jnp.where(sub_lo, lb_a, lb_b)

        def decay(l_f, l_b):
            return jnp.where(diff >= 0, jnp.exp(l_f * jnp.maximum(diff, 0.0)),
                             jnp.exp(l_b * jnp.maximum(-diff, 0.0)))

        return dict(
            d_ab=jnp.concatenate([decay(lf_a, lb_a), decay(lf_b, lb_b)], axis=0) * k_scale,
            wq_fb=jnp.concatenate([jnp.exp(lf * (row + 1.0)), jnp.exp(lb * (ch - row))], axis=1),
            wk_fb_t=jnp.concatenate([jnp.exp(lf_t * (ch - 1.0 - col)), jnp.exp(lb_t * col)], axis=0) * k_scale,
            gf=jnp.exp(lf * ch), gb=jnp.exp(lb * ch))

    tables = [pair_tables(p) for p in range(pairs)]

    def source(n):
        if n < n_lat:
            return rl_ref, slice(n * ch, (n + 1) * ch), True
        return rc_ref, slice((n - n_lat) * ch, (n - n_lat + 1) * ch), False

    def part(p, which):
        c0 = (which * pairs + p) * LANES
        return slice(c0, c0 + LANES)

    lo_pool = lane < POOL_GDIM
    halo = POOL_HALO

    def pool_fill(u_ref, length):
        pad_s[0:halo, :] = jnp.zeros((halo, POOL_W), BF16)
        pad_s[halo:halo + length, :] = u_ref[0]
        pad_s[halo + length:2 * halo + length, :] = jnp.zeros((halo, POOL_W), BF16)

    def pool_slab(u_ref, o_ref, length, r0):
        t = (lax.broadcasted_iota(jnp.int32, (POOL_TILE, 1), 0) + r0).astype(F32)
        cols = []
        for hf in range(POOL_W // LANES):
            cs = slice(hf * LANES, (hf + 1) * LANES)
            tokens = pad_s[r0:r0 + POOL_TILE + 2 * halo, cs]

            def inv_count(w):
                if r0 - w // 2 >= 0 and r0 + POOL_TILE - w // 2 + w <= length:
                    return 1.0 / w
                cnt = (jnp.clip(t - w // 2 + w, 0.0, float(length))
                       - jnp.clip(t - w // 2, 0.0, float(length)))
                return 1.0 / cnt

            w_n, w_w = POOL_WINDOWS[2 * hf], POOL_WINDOWS[2 * hf + 1]
            pooled = jnp.where(lo_pool, _dot(band_ref[2 * hf], tokens) * inv_count(w_n),
                               _dot(band_ref[2 * hf + 1], tokens) * inv_count(w_w))
            cols.append(pooled - u_ref[0, r0:r0 + POOL_TILE, cs].astype(F32))
        pooled = jnp.concatenate(cols, axis=-1).astype(BF16)
        y = _dot(pooled, w_ref[0]) * s_ref[0]
        o_ref[0, r0:r0 + POOL_TILE, :] = y.astype(o_ref.dtype)

    pool_tasks = []
    for u_ref, o_ref in zip(u_refs, po_refs):
        length = u_ref.shape[1]
        pool_tasks.append(functools.partial(pool_fill, u_ref, length))
        pool_tasks += [functools.partial(pool_slab, u_ref, o_ref, length, r0)
                       for r0 in range(0, length, POOL_TILE)]
    pool_tasks.reverse()

    for n in range(n_all):
        src_ref, src_rows, roped = source(n)
        rows = slice(n * ch, (n + 1) * ch)
        for p in range(pairs):
            q = src_ref[0, src_rows, part(p, 0)].astype(F32)
            k = src_ref[0, src_rows, part(p, 1)].astype(F32)
            if roped:
                c, sa, sb = c_ref[src_rows, :], sa_ref[src_rows, :], sb_ref[src_rows, :]
                q = q * c + pltpu.roll(q, LANES - half, 1) * sa + pltpu.roll(q, half, 1) * sb
                k = k * c + pltpu.roll(k, LANES - half, 1) * sa + pltpu.roll(k, half, 1) * sb
            q_s[p, rows, :] = q
            k_s[p, rows, :] = k
            kt = k.T
            kt2 = jnp.concatenate([kt, kt], axis=0) * tables[p]["wk_fb_t"]
            u_s[p, n] = _dot(kt2.astype(BF16), src_ref[0, src_rows, part(p, 2)])
        if n % POOL_EVERY == POOL_EVERY - 1 and pool_tasks:
            pool_tasks.pop()()

    order_f = list(range(n_lat, n_all)) + list(range(n_lat))
    order_b = list(range(n_all - 1, -1, -1))
    for p in range(pairs):
        s = jnp.zeros((LANES, LANES), F32)
        for n in order_f:
            st_s[p, n, 0:LANES, :] = (s * blockdiag).astype(BF16)
            s = tables[p]["gf"] * s + u_s[p, n, 0:LANES, :]
        s = jnp.zeros((LANES, LANES), F32)
        for n in order_b:
            st_s[p, n, LANES:2 * LANES, :] = (s * blockdiag).astype(BF16)
            s = tables[p]["gb"] * s + u_s[p, n, LANES:2 * LANES, :]

    n_out = n_all if need_ctx else n_lat

    def scores_and_state(n):
        rows = slice(n * ch, (n + 1) * ch)
        out = []
        for p in range(pairs):
            q = q_s[p, rows, :]
            kb = k_s[p, rows, :].astype(BF16)
            q_ab = jnp.concatenate([jnp.where(lo, q, 0.0), jnp.where(lo, 0.0, q)], axis=0).astype(BF16)
            a_ab = _dot_nt(q_ab, kb)
            carried = _dot((jnp.concatenate([q, q], axis=1) * tables[p]["wq_fb"]).astype(BF16), st_s[p, n])
            out.append((a_ab, carried))
        return out

    def finish(n, staged):
        src_ref, src_rows, _ = source(n)
        dst_ref = ro_refs[0] if n < n_lat else ro_refs[1]
        for p, (a_ab, carried) in enumerate(staged):
            vb = src_ref[0, src_rows, part(p, 2)]
            gate = src_ref[0, src_rows, part(p, 3)].astype(F32)
            o_ab = _dot((a_ab * tables[p]["d_ab"]).astype(BF16), vb)
            o = jnp.where(lo, o_ab[0:ch], o_ab[ch:2 * ch]) + carried
            inv = 1.0 / RET_DIM
            mu = jnp.where(lo, jnp.sum(jnp.where(lo, o, 0.0), axis=-1, keepdims=True),
                           jnp.sum(jnp.where(lo, 0.0, o), axis=-1, keepdims=True)) * inv
            dlt = o - mu
            sq = dlt * dlt
            var = jnp.where(lo, jnp.sum(jnp.where(lo, sq, 0.0), axis=-1, keepdims=True),
                            jnp.sum(jnp.where(lo, 0.0, sq), axis=-1, keepdims=True)) * inv
            dst_ref[0, src_rows, p * LANES:(p + 1) * LANES] = (
                _silu(gate) * (dlt * lax.rsqrt(var + EPS))).astype(dst_ref.dtype)

    staged = scores_and_state(0)
    for n in range(n_out):
        staged_next = scores_and_state(n + 1) if n + 1 < n_out else None
        if pool_tasks:
            pool_tasks.pop()()
        finish(n, staged)
        staged = staged_next
    while pool_tasks:
        pool_tasks.pop()()


def _ret_pool(lg, layer, ret_lat, ret_ctx, tabs, us, band, w_bd, scale, need_ctx):
    b, rl, _ = ret_lat.shape
    rc = ret_ctx.shape[1]
    ch = RET_CHUNK
    n_lat, n_ctx = rl // ch, rc // ch
    n_all = n_lat + n_ctx
    pairs = RET_HEADS // 2
    lengths = tuple(u.shape[1] for u in us)
    assert len(us) == (2 if need_ctx else 1)
    kern = functools.partial(_ret_pool_kernel, layer=layer, n_lat=n_lat, n_ctx=n_ctx, need_ctx=need_ctx)
    tab_spec = pl.BlockSpec((rl, LANES), lambda i: (0, 0), pipeline_mode=pl.Buffered(1))
    rows = lambda n, width: pl.BlockSpec((1, n, width), lambda i: (i, 0, 0))
    out_specs = [rows(n, RET_W) for n in lengths] + [rows(n, POOL_W) for n in lengths]
    out_shape = ([jax.ShapeDtypeStruct((b, n, RET_W), BF16) for n in lengths]
                 + [jax.ShapeDtypeStruct((b, n, POOL_W), BF16) for n in lengths])
    outs = pl.pallas_call(
        kern,
        grid=(b,),
        in_specs=[
            pl.BlockSpec(memory_space=pltpu.SMEM),
            rows(rl, 4 * RET_W), rows(rc, 4 * RET_W),
            tab_spec, tab_spec, tab_spec,
            *[rows(n, POOL_W) for n in lengths],
            pl.BlockSpec(band.shape, lambda i: (0, 0, 0), pipeline_mode=pl.Buffered(1)),
            _layer_spec(w_bd, layer), _layer_spec(scale, layer),
        ],
        out_specs=out_specs,
        out_shape=out_shape,
        scratch_shapes=[
            pltpu.VMEM((pairs, rl + rc, LANES), F32), pltpu.VMEM((pairs, rl + rc, LANES), F32),
            pltpu.VMEM((pairs, n_all, 2 * LANES, LANES), F32),
            pltpu.VMEM((pairs, n_all, 2 * LANES, LANES), BF16),
            pltpu.VMEM((max(lengths) + 2 * POOL_HALO, POOL_W), BF16),
        ],
        compiler_params=_params(("parallel",)),
        name="retention_pool",
    )(lg, ret_lat, ret_ctx, *tabs, *us, band, w_bd, scale)
    n = len(lengths)
    return outs[:n], outs[n:]


def _attn_kernel(*refs, use_lat):
    if use_lat:
        q_ref, kl_ref, vtl_ref, kc_ref, vtc_ref, o_ref = refs
    else:
        q_ref, kc_ref, vtc_ref, o_ref = refs

    sources = [(kc_ref, vtc_ref)]
    if use_lat:
        sources.append((kl_ref, vtl_ref))

    n_sub = q_ref.shape[2] // Q_SUB
    items = [((qs, h), k_ref, vt_ref, c0) for qs in range(n_sub) for h in range(MLA_HEADS)
             for k_ref, vt_ref in sources for c0 in range(0, k_ref.shape[2], KEY_CHUNK)]
    last_item = {g: i for i, (g, _, _, _) in enumerate(items)}

    scores, m_run, acc, done = {}, {}, {}, {}
    for t in range(len(items) + SCORE_LOOKAHEAD):
        if t < len(items):
            (qs, h), k_ref, _, c0 = items[t]
            q = q_ref[0, h, qs * Q_SUB:(qs + 1) * Q_SUB, :]
            scores[t] = _dot_nt(k_ref[0, h, c0:c0 + KEY_CHUNK, :], q)
        i = t - SCORE_LOOKAHEAD
        if i < 0:
            continue
        g, _, vt_ref, c0 = items[i]
        qs, h = g
        s = scores.pop(i)
        m_new = jnp.max(s, axis=0, keepdims=True)
        if g in m_run:
            m_new = jnp.maximum(m_run[g], m_new)
        pv = _dot(vt_ref[0, h, :, c0:c0 + KEY_CHUNK], jnp.exp2(s - m_new).astype(BF16))
        acc[g] = pv if g not in acc else acc[g] * jnp.exp2(m_run[g] - m_new) + pv
        m_run[g] = m_new
        if i == last_item[g]:
            a = acc.pop(g)
            done[g] = a[0:MLA_V, :] * (1.0 / a[MLA_V:MLA_V + 1, :])
            if h % 2 == 1:
                pair_t = jnp.concatenate([done.pop((qs, h - 1)), done.pop(g)], axis=0)
                o_ref[0, qs * Q_SUB:(qs + 1) * Q_SUB, (h // 2) * HEAD_PAD:(h // 2 + 1) * HEAD_PAD] = (
                    pair_t.T.astype(o_ref.dtype))


def _attention(q, k_ctx, vt_ctx, k_lat=None, vt_lat=None):
    b, nh, rows, _ = q.shape
    use_lat = k_lat is not None
    tile = min(ATTN_TILE, rows)

    def whole(a):
        return pl.BlockSpec((1,) + a.shape[1:], lambda i, j: (i, 0, 0, 0))

    in_specs = [pl.BlockSpec((1, nh, tile, HEAD_PAD), lambda i, j: (i, 0, j, 0))]
    args = [q]
    if use_lat:
        in_specs += [whole(k_lat), whole(vt_lat)]
        args += [k_lat, vt_lat]
    in_specs += [whole(k_ctx), whole(vt_ctx)]
    args += [k_ctx, vt_ctx]
    return pl.pallas_call(
        functools.partial(_attn_kernel, use_lat=use_lat),
        grid=(b, rows // tile),
        in_specs=in_specs,
        out_specs=pl.BlockSpec((1, tile, MLA_W), lambda i, j: (i, j, 0)),
        out_shape=jax.ShapeDtypeStruct((b, rows, MLA_W), BF16),
        compiler_params=_params(("parallel", "arbitrary")),
        name="mla_attention",
    )(*args)


def _mix_mlp_kernel(x_ref, r_ref, m_ref, p_ref, wo_ref, mod_ref, g_ref, w1_ref, w2_ref, gf_ref, o_ref,
                    *, final_norm):
    def mix(rows):
        y = (_dot(r_ref[0, rows, :], wo_ref[0, 0:RET_W, :])
             + _dot(m_ref[0, rows, :], wo_ref[0, RET_W:RET_W + MLA_W, :])
             + _dot(p_ref[0, rows, :], wo_ref[0, RET_W + MLA_W:, :]))
        x = x_ref[0, rows, :] + mod_ref[0, 0, 2:3, :] * y
        h = (_rms(x, g_ref[0]) * (1.0 + mod_ref[0, 0, 4:5, :]) + mod_ref[0, 0, 3:4, :]).astype(BF16)
        return x, h

    def mlp(x, h, rows):
        acc = None
        for c0 in range(0, w1_ref.shape[2], FF_CHUNK):
            a = jnp.maximum(_dot(h, w1_ref[0, :, c0:c0 + FF_CHUNK]), 0.0)
            part = _dot((a * a).astype(BF16), w2_ref[0, c0:c0 + FF_CHUNK, :])
            acc = part if acc is None else acc + part
        y = x + mod_ref[0, 0, 5:6, :] * acc
        if final_norm:
            y = _rms(y, gf_ref[...])
        o_ref[0, rows, :] = y

    sub = min(MLP_SUB, x_ref.shape[1])
    subs = [slice(r0, r0 + sub) for r0 in range(0, x_ref.shape[1], sub)]
    staged = mix(subs[0])
    for i, rows in enumerate(subs):
        x, h = staged
        if i + 1 < len(subs):
            staged = mix(subs[i + 1])
        mlp(x, h, rows)


def _mix_mlp(x, ret, mla, pool, mod, mod_row, layer, w_out, g, w1, w2, g_final, tile, final_norm):
    b, r, d = x.shape
    row = lambda width: pl.BlockSpec((1, tile, width), lambda i, j: (i, j, 0))
    return pl.pallas_call(
        functools.partial(_mix_mlp_kernel, final_norm=final_norm),
        grid=(b, r // tile),
        in_specs=[
            row(d), row(RET_W), row(MLA_W), row(POOL_W),
            _layer_spec(w_out, layer),
            pl.BlockSpec((1, 1) + mod.shape[2:], lambda i, j: (layer, mod_row(i), 0, 0)),
            _layer_spec(g, layer), _layer_spec(w1, layer), _layer_spec(w2, layer),
            pl.BlockSpec(g_final.shape, lambda i, j: (0, 0)),
        ],
        out_specs=row(d),
        out_shape=jax.ShapeDtypeStruct((b, r, d), F32),
        compiler_params=_params(("parallel", "parallel")),
        name="mix_mlp",
    )(x, ret, mla, pool, w_out, mod, g, w1, w2, g_final)


def _prep_w_uq(w_uq):
    depth, rank, _ = w_uq.shape
    half = MLA_ROPE // 2
    heads = w_uq.astype(BF16).reshape(depth, rank, MLA_HEADS, MLA_NOPE + MLA_ROPE)
    main = jnp.pad(heads, ((0, 0), (0, 0), (0, 0), (0, HEAD_PAD - MLA_NOPE - MLA_ROPE)))
    rope = heads[..., MLA_NOPE:]
    partner = jnp.concatenate([-rope[..., half:], rope[..., :half]], axis=-1)
    return jnp.concatenate([main.reshape(depth, rank, UQ_ROT),
                            partner.reshape(depth, rank, MLA_HEADS * MLA_ROPE)], axis=-1)


def _prep_w_ukv(w_ukv):
    depth, rank, _ = w_ukv.shape
    heads = w_ukv.astype(BF16).reshape(depth, rank, MLA_HEADS, MLA_NOPE + MLA_V)
    return jnp.concatenate([heads[..., :MLA_NOPE].reshape(depth, rank, UKV_V),
                            heads[..., MLA_NOPE:].reshape(depth, rank, UKV_COLS - UKV_V)], axis=-1)


def _prep_w_in_kernel(w_ref, o_ref):
    half = MLA_ROPE // 2
    o_ref[0, :, 0:COL_U] = w_ref[0, :, 0:COL_U].astype(o_ref.dtype)
    o_ref[0, :, COL_U:COL_KPE] = w_ref[0, :, COL_U + MLA_ROPE:COL_U + MLA_ROPE + POOL_W].astype(o_ref.dtype)
    v = w_ref[0, :, COL_U:COL_U + HEAD_PAD]
    lane = lax.broadcasted_iota(jnp.int32, (1, HEAD_PAD), 1)
    kx = jnp.where(lane < MLA_ROPE, v,
                   jnp.where(lane < MLA_ROPE + half, -pltpu.roll(v, half, 1),
                             jnp.where(lane < 2 * MLA_ROPE, pltpu.roll(v, MLA_ROPE + half, 1), 0.0)))
    o_ref[0, :, COL_KPE:IN_COLS] = kx.astype(o_ref.dtype)


def _prep_w_in(w_in):
    depth, d, n = w_in.shape
    assert n == COL_U + MLA_ROPE + POOL_W
    tile = 256
    return pl.pallas_call(
        _prep_w_in_kernel,
        grid=(depth, d // tile),
        in_specs=[pl.BlockSpec((1, tile, n), lambda l, i: (l, i, 0))],
        out_specs=pl.BlockSpec((1, tile, IN_COLS), lambda l, i: (l, i, 0)),
        out_shape=jax.ShapeDtypeStruct((depth, d, IN_COLS), BF16),
        compiler_params=_params(("parallel", "parallel")),
        name="prep_w_in",
    )(w_in)


def _rope_angles(n_rows, dim):
    pos = np.arange(n_rows)
    n_freq = dim // 4
    inv = np.float32(ROPE_BASE) ** (-np.arange(n_freq, dtype=np.float32) / np.float32(n_freq))
    return np.concatenate([(pos // GRID_W).astype(np.float32)[:, None] * inv,
                           (pos % GRID_W).astype(np.float32)[:, None] * inv], axis=-1)


def _ret_rope_tables(n_rows):
    ang = _rope_angles(n_rows, RET_DIM)
    cos, sin = np.cos(ang), np.sin(ang)
    zero = np.zeros_like(sin)
    reps = LANES // RET_DIM
    c = np.tile(np.concatenate([cos, cos], axis=-1), (1, reps))
    sa = np.tile(np.concatenate([-sin, zero], axis=-1), (1, reps))
    sb = np.tile(np.concatenate([zero, sin], axis=-1), (1, reps))
    return tuple(jnp.asarray(a, F32) for a in (c, sa, sb))


def _mla_rope_tables(n_rows):
    ang = _rope_angles(n_rows, MLA_ROPE)
    cos, sin = np.cos(ang), np.sin(ang)
    c = np.ones((n_rows, HEAD_PAD), np.float32)
    s = np.zeros((n_rows, HEAD_PAD), np.float32)
    c[:, MLA_NOPE:MLA_NOPE + MLA_ROPE] = np.concatenate([cos, cos], axis=-1)
    s[:, MLA_NOPE:MLA_NOPE + MLA_ROPE] = np.concatenate([sin, sin], axis=-1)
    return jnp.asarray(c, F32), jnp.asarray(s, F32)


def _pool_bands():
    t = np.arange(POOL_TILE)[:, None]
    s = np.arange(POOL_TILE + 2 * POOL_HALO)[None, :]
    bands = [(s >= t + POOL_HALO - w // 2) & (s < t + POOL_HALO - w // 2 + w) for w in POOL_WINDOWS]
    return jnp.asarray(np.stack(bands).astype(np.float32), BF16)


def _block_diag(w_pool):
    depth, g, c, _ = w_pool.shape
    rows = []
    for i in range(g):
        blocks = [w_pool[:, i] if j == i else jnp.zeros((depth, c, c), w_pool.dtype) for j in range(g)]
        rows.append(jnp.concatenate(blocks, axis=2))
    return jnp.concatenate(rows, axis=1)


def kernel(x, c, ctx, c_ctx, w_ada, b_ada, norm_mix, w_in, q_norm, w_uq, kv_norm, w_ukv,
           ret_decay_logit, w_pool, pool_scale, w_out, norm_mlp, w_ff1, w_ff2, norm_final):
    b, n_lat, d = x.shape
    n_ctx = ctx.shape[1]
    depth = w_ada.shape[0]
    lat_tile, mlp_tile = min(LAT_TILE, n_lat), min(MLP_TILE, n_lat)
    assert n_lat % lat_tile == 0 and n_lat % mlp_tile == 0 and n_lat % min(ATTN_TILE, n_lat) == 0
    assert n_lat % GRID_W == 0 and n_lat % Q_SUB == 0
    assert n_ctx % RET_CHUNK == 0 and n_ctx % POOL_TILE == 0 and n_lat % POOL_TILE == 0
    assert d == 4 * RET_W and w_ff1.shape[2] % FF_CHUNK == 0

    ret_tabs = _ret_rope_tables(n_lat)
    mla_tabs = _mla_rope_tables(n_lat)

    rows = -(-(b + 1) // 8) * 8
    cc = jnp.zeros((rows, d), F32).at[:b].set(c).at[b].set(c_ctx)
    mod = _ada(cc, w_ada, b_ada).reshape(depth, rows, 6, d)
    lat_row, ctx_row = (lambda i: i), (lambda i: b)
    lg = jax.nn.log_sigmoid(ret_decay_logit.astype(F32))

    def rows3(a):
        return a.reshape(depth, 1, a.shape[-1])

    proj_w = (rows3(norm_mix), _prep_w_in(w_in), rows3(q_norm), _prep_w_uq(w_uq),
              rows3(kv_norm), _prep_w_ukv(w_ukv))
    pool_w = (_pool_bands(), _block_diag(w_pool).astype(BF16), rows3(pool_scale))
    mlp_w = (w_out.astype(BF16), rows3(norm_mlp), w_ff1.astype(BF16), w_ff2.astype(BF16), norm_final[None])

    h_ctx = ctx
    for l in range(depth):
        last = l == depth - 1
        ret_l, u_l, q_l, k_l, vt_l = _inproj(x, mod, lat_row, l, *proj_w, mla_tabs, lat_tile)
        ret_c, u_c, q_c, k_c, vt_c = _inproj(h_ctx, mod, ctx_row, l, *proj_w, None, n_ctx)
        ret_o, pool_o = _ret_pool(lg, l, ret_l, ret_c, ret_tabs, (u_l,) if last else (u_l, u_c), *pool_w, not last)
        mla_l = _attention(q_l, k_c, vt_c, k_l, vt_l)
        if not last:
            mla_c = _attention(q_c, k_c, vt_c)
            h_ctx = _mix_mlp(h_ctx, ret_o[1], mla_c, pool_o[1], mod, ctx_row, l, *mlp_w, n_ctx, False)
        x = _mix_mlp(x, ret_o[0], mla_l, pool_o[0], mod, lat_row, l, *mlp_w, mlp_tile, last)
    return x
```

```python
import functools
import math

import jax
import jax.numpy as jnp
import numpy as np
from jax import lax
from jax.experimental import pallas as pl
from jax.experimental.pallas import tpu as pltpu

GRID_W = 64
RET_HEADS = 4
RET_DIM = 64
RET_W = RET_HEADS * RET_DIM
RET_CHUNK = 128
MLA_HEADS = 8
MLA_NOPE = 64
MLA_ROPE = 32
MLA_V = 64
MLA_Q_RANK = 256
MLA_KV_RANK = 128
MLA_W = MLA_HEADS * MLA_V
POOL_GROUPS = 4
POOL_WINDOWS = (2, 4, 8, 16)
POOL_GDIM = 64
POOL_W = POOL_GROUPS * POOL_GDIM
ROPE_BASE = 10000.0
EPS = 1e-6

LANES = 128
LAT_TILE = 1024
MLP_TILE = 1024
IN_SUB = 256
ATTN_TILE = 1024
Q_SUB = 256
KEY_CHUNK = 256
SCORE_LOOKAHEAD = 5
POOL_TILE = 256
POOL_EVERY = 2
RET_LOOKAHEAD = 1
ADA_COL_BLOCKS = 4
PREP_TILE = 256
HEAD_PAD = 128
VT_ROWS = MLA_V + 16
POOL_HALO = 128
FF_CHUNK = 1024
MLP_SUB = 256
VMEM_LIMIT = 56 * 1024 * 1024

COL_RET = 0
COL_CQ = 4 * RET_W
COL_CKV = COL_CQ + MLA_Q_RANK
COL_U = COL_CKV + MLA_KV_RANK
COL_KPE = COL_U + POOL_W
IN_COLS = COL_KPE + HEAD_PAD
UQ_ROT = MLA_HEADS * HEAD_PAD
UQ_COLS = UQ_ROT + MLA_HEADS * MLA_ROPE
UKV_V = MLA_HEADS * MLA_NOPE
UKV_COLS = UKV_V + MLA_HEADS * MLA_V

F32 = jnp.float32
BF16 = jnp.bfloat16


def _params(sem):
    return pltpu.CompilerParams(dimension_semantics=sem, vmem_limit_bytes=VMEM_LIMIT)


def _layer_spec(a, layer):
    tail = a.shape[1:]
    return pl.BlockSpec((1,) + tail, lambda *_: (layer,) + (0,) * len(tail), pipeline_mode=pl.Buffered(1))


def _dot(a, b):
    return jnp.dot(a, b, preferred_element_type=F32)


def _dot_nt(a, b):
    return lax.dot_general(a, b, (((1,), (1,)), ((), ())), preferred_element_type=F32)


def _rms(x, g):
    return x * lax.rsqrt(jnp.mean(x * x, axis=-1, keepdims=True) + EPS) * g


def _silu(x):
    return x * (1.0 / (1.0 + jnp.exp(-x)))


def _ada_kernel(c_ref, w_ref, b_ref, o_ref):
    s = _silu(c_ref[...]).astype(BF16)
    o_ref[0] = _dot(s, w_ref[0].astype(BF16)) + b_ref[0]


def _ada(cc, w_ada, b_ada):
    depth, d, n = w_ada.shape
    rows = cc.shape[0]
    tn = n // ADA_COL_BLOCKS
    assert n % ADA_COL_BLOCKS == 0 and tn % LANES == 0
    return pl.pallas_call(
        _ada_kernel,
        grid=(depth, n // tn),
        in_specs=[
            pl.BlockSpec((rows, d), lambda l, j: (0, 0)),
            pl.BlockSpec((1, d, tn), lambda l, j: (l, 0, j)),
            pl.BlockSpec((1, 1, tn), lambda l, j: (l, 0, j)),
        ],
        out_specs=pl.BlockSpec((1, rows, tn), lambda l, j: (l, 0, j)),
        out_shape=jax.ShapeDtypeStruct((depth, rows, n), F32),
        compiler_params=_params(("parallel", "parallel")),
        name="ada_mod",
    )(cc, w_ada, b_ada.reshape(depth, 1, n))


def _inproj_kernel(*refs, rope):
    if rope:
        (x_ref, mod_ref, g_ref, w_ref, gq_ref, wq_ref, gkv_ref, wkv_ref, c_ref, s_ref,
         ret_ref, u_ref, q_ref, k_ref, vt_ref) = refs
    else:
        (x_ref, mod_ref, g_ref, w_ref, gq_ref, wq_ref, gkv_ref, wkv_ref,
         ret_ref, u_ref, q_ref, k_ref, vt_ref) = refs
    nh = MLA_HEADS
    lane = lax.broadcasted_iota(jnp.int32, (1, HEAD_PAD), 1)
    nope_lanes = lane < MLA_NOPE
    rope_lanes = jnp.logical_and(lane >= MLA_NOPE, lane < MLA_NOPE + MLA_ROPE)
    sub = min(IN_SUB, x_ref.shape[1])
    den_rows = (lax.broadcasted_iota(jnp.int32, (VT_ROWS - MLA_V, sub), 0) == 0).astype(vt_ref.dtype)
    q_scale = (MLA_NOPE + MLA_ROPE) ** -0.5 * math.log2(math.e)

    def project(rows):
        h = _rms(x_ref[0, rows, :], g_ref[0]) * (1.0 + mod_ref[0, 0, 1:2, :]) + mod_ref[0, 0, 0:1, :]
        p = _dot(h.astype(BF16), w_ref[0])
        ret_ref[0, rows, :] = p[:, COL_RET:COL_CQ].astype(ret_ref.dtype)
        u_ref[0, rows, :] = p[:, COL_U:COL_KPE].astype(u_ref.dtype)
        return p

    def up_project(p, rows):
        yq = _rms(p[:, COL_CQ:COL_CKV], gq_ref[0]).astype(BF16)
        if rope:
            cos, sin = c_ref[rows, :], s_ref[rows, :]
            q2 = _dot(yq, wq_ref[0])
        else:
            q2 = _dot(yq, wq_ref[0, :, 0:UQ_ROT])
        ykv = _rms(p[:, COL_CKV:COL_U], gkv_ref[0]).astype(BF16)
        kv = _dot(ykv, wkv_ref[0])
        kx = p[:, COL_KPE:IN_COLS]
        kpe = pltpu.roll(kx, MLA_NOPE, 1)
        if rope:
            kpe = kpe * cos + pltpu.roll(kx, MLA_NOPE - MLA_ROPE, 1) * sin
        kpe = jnp.where(rope_lanes, kpe, 0.0)
        for hh in range(nh):
            qh = q2[:, hh * HEAD_PAD:(hh + 1) * HEAD_PAD]
            if rope:
                blk, j = divmod(hh * MLA_ROPE, HEAD_PAD)
                rot = q2[:, UQ_ROT + blk * HEAD_PAD:UQ_ROT + (blk + 1) * HEAD_PAD]
                shift = (MLA_NOPE - j) % HEAD_PAD
                qh = qh * cos + (pltpu.roll(rot, shift, 1) if shift else rot) * sin
            q_ref[0, hh, rows, :] = (qh * q_scale).astype(q_ref.dtype)
            blk, j = divmod(hh * MLA_NOPE, HEAD_PAD)
            kn = kv[:, blk * HEAD_PAD:(blk + 1) * HEAD_PAD]
            if j:
                kn = pltpu.roll(kn, HEAD_PAD - j, 1)
            k_ref[0, hh, rows, :] = jnp.where(nope_lanes, kn, kpe).astype(k_ref.dtype)
        for blk in range(nh * MLA_V // HEAD_PAD):
            vt = kv[:, UKV_V + blk * HEAD_PAD:UKV_V + (blk + 1) * HEAD_PAD].T.astype(vt_ref.dtype)
            for j in range(HEAD_PAD // MLA_V):
                hh = blk * (HEAD_PAD // MLA_V) + j
                vt_ref[0, hh, 0:MLA_V, rows] = vt[j * MLA_V:(j + 1) * MLA_V, :]
                vt_ref[0, hh, MLA_V:VT_ROWS, rows] = den_rows

    subs = [slice(r0, r0 + sub) for r0 in range(0, x_ref.shape[1], sub)]
    p_next = project(subs[0])
    for i, rows in enumerate(subs):
        p = p_next
        if i + 1 < len(subs):
            p_next = project(subs[i + 1])
        up_project(p, rows)


def _inproj(x, mod, mod_row, layer, g, w, gq, wq, gkv, wkv, tabs, tile):
    b, r, d = x.shape
    rope = tabs is not None
    nh = MLA_HEADS
    row = lambda width: pl.BlockSpec((1, tile, width), lambda i, j: (i, j, 0))
    head = pl.BlockSpec((1, nh, tile, HEAD_PAD), lambda i, j: (i, 0, j, 0))
    head_t = pl.BlockSpec((1, nh, VT_ROWS, tile), lambda i, j: (i, 0, 0, j))
    assert tile % min(IN_SUB, tile) == 0
    in_specs = [
        row(d),
        pl.BlockSpec((1, 1) + mod.shape[2:], lambda i, j: (layer, mod_row(i), 0, 0)),
    ] + [_layer_spec(a, layer) for a in (g, w, gq, wq, gkv, wkv)]
    args = [x, mod, g, w, gq, wq, gkv, wkv]
    if rope:
        in_specs += [pl.BlockSpec((tile, HEAD_PAD), lambda i, j: (j, 0))] * 2
        args += list(tabs)
    hshape = jax.ShapeDtypeStruct((b, nh, r, HEAD_PAD), BF16)
    return pl.pallas_call(
        functools.partial(_inproj_kernel, rope=rope),
        grid=(b, r // tile),
        in_specs=in_specs,
        out_specs=[row(4 * RET_W), row(POOL_W), head, head, head_t],
        out_shape=[jax.ShapeDtypeStruct((b, r, 4 * RET_W), BF16),
                   jax.ShapeDtypeStruct((b, r, POOL_W), BF16), hshape, hshape,
                   jax.ShapeDtypeStruct((b, nh, VT_ROWS, r), BF16)],
        compiler_params=_params(("parallel", "parallel")),
        name="in_proj",
    )(*args)


def _ret_pool_kernel(*refs, layer, n_lat, n_ctx, need_ctx):
    n_seg = 2 if need_ctx else 1
    it = iter(refs)
    lg_ref, rl_ref, rc_ref, c_ref, sa_ref, sb_ref = [next(it) for _ in range(6)]
    u_refs = [next(it) for _ in range(n_seg)]
    band_ref, w_ref, s_ref = next(it), next(it), next(it)
    ro_refs = [next(it) for _ in range(n_seg)]
    po_refs = [next(it) for _ in range(n_seg)]
    q_s, k_s, u_s, st_s, pad_s = it
    ch = RET_CHUNK
    pairs = RET_HEADS // 2
    n_all = n_lat + n_ctx
    lane = lax.broadcasted_iota(jnp.int32, (1, LANES), 1)
    lo = lane < RET_DIM
    sub_lo = lax.broadcasted_iota(jnp.int32, (LANES, 1), 0) < RET_DIM
    row = lax.broadcasted_iota(jnp.int32, (ch, 1), 0).astype(F32)
    col = lax.broadcasted_iota(jnp.int32, (1, ch), 1).astype(F32)
    diff = row - col
    head_of_row = lax.broadcasted_iota(jnp.int32, (LANES, 1), 0) // RET_DIM
    blockdiag = (head_of_row == lane // RET_DIM).astype(F32)
    k_scale = RET_DIM ** -0.5
    half = RET_DIM // 2

    def pair_tables(p):
        lf_a, lf_b = lg_ref[layer, 0, 2 * p], lg_ref[layer, 0, 2 * p + 1]
        lb_a, lb_b = lg_ref[layer, 1, 2 * p], lg_ref[layer, 1, 2 * p + 1]
        lf = jnp.where(lo, lf_a, lf_b)
        lb = jnp.where(lo, lb_a, lb_b)
        lf_t = jnp.where(sub_lo, lf_a, lf_b)
        lb_t = jnp.where(sub_lo, lb_a, lb_b)

        def decay(l_f, l_b):
            return jnp.where(diff >= 0, jnp.exp(l_f * jnp.maximum(diff, 0.0)),
                             jnp.exp(l_b * jnp.maximum(-diff, 0.0)))

        return dict(
            d_ab=jnp.concatenate([decay(lf_a, lb_a), decay(lf_b, lb_b)], axis=0) * k_scale,
            wq_fb=jnp.concatenate([jnp.exp(lf * (row + 1.0)), jnp.exp(lb * (ch - row))], axis=1),
            wk_fb_t=jnp.concatenate([jnp.exp(lf_t * (ch - 1.0 - col)), jnp.exp(lb_t * col)], axis=0) * k_scale,
            gf=jnp.exp(lf * ch), gb=jnp.exp(lb * ch))

    tables = [pair_tables(p) for p in range(pairs)]

    def source(n):
        if n < n_lat:
            return rl_ref, slice(n * ch, (n + 1) * ch), True
        return rc_ref, slice((n - n_lat) * ch, (n - n_lat + 1) * ch), False

    def part(p, which):
        c0 = (which * pairs + p) * LANES
        return slice(c0, c0 + LANES)

    lo_pool = lane < POOL_GDIM
    halo = POOL_HALO

    def pool_fill(u_ref, length):
        pad_s[0:halo, :] = jnp.zeros((halo, POOL_W), BF16)
        pad_s[halo:halo + length, :] = u_ref[0]
        pad_s[halo + length:2 * halo + length, :] = jnp.zeros((halo, POOL_W), BF16)

    def pool_slab(u_ref, o_ref, length, r0):
        t = (lax.broadcasted_iota(jnp.int32, (POOL_TILE, 1), 0) + r0).astype(F32)
        cols = []
        for hf in range(POOL_W // LANES):
            cs = slice(hf * LANES, (hf + 1) * LANES)
            tokens = pad_s[r0:r0 + POOL_TILE + 2 * halo, cs]

            def inv_count(w):
                if r0 - w // 2 >= 0 and r0 + POOL_TILE - w // 2 + w <= length:
                    return 1.0 / w
                cnt = (jnp.clip(t - w // 2 + w, 0.0, float(length))
                       - jnp.clip(t - w // 2, 0.0, float(length)))
                return 1.0 / cnt

            w_n, w_w = POOL_WINDOWS[2 * hf], POOL_WINDOWS[2 * hf + 1]
            pooled = jnp.where(lo_pool, _dot(band_ref[2 * hf], tokens) * inv_count(w_n),
                               _dot(band_ref[2 * hf + 1], tokens) * inv_count(w_w))
            cols.append(pooled - u_ref[0, r0:r0 + POOL_TILE, cs].astype(F32))
        pooled = jnp.concatenate(cols, axis=-1).astype(BF16)
        y = _dot(pooled, w_ref[0]) * s_ref[0]
        o_ref[0, r0:r0 + POOL_TILE, :] = y.astype(o_ref.dtype)

    pool_tasks = []
    for u_ref, o_ref in zip(u_refs, po_refs):
        length = u_ref.shape[1]
        pool_tasks.append(functools.partial(pool_fill, u_ref, length))
        pool_tasks += [functools.partial(pool_slab, u_ref, o_ref, length, r0)
                       for r0 in range(0, length, POOL_TILE)]
    pool_tasks.reverse()

    for n in range(n_all):
        src_ref, src_rows, roped = source(n)
        rows = slice(n * ch, (n + 1) * ch)
        for p in range(pairs):
            q = src_ref[0, src_rows, part(p, 0)].astype(F32)
            k = src_ref[0, src_rows, part(p, 1)].astype(F32)
            if roped:
                c, sa, sb = c_ref[src_rows, :], sa_ref[src_rows, :], sb_ref[src_rows, :]
                q = q * c + pltpu.roll(q, LANES - half, 1) * sa + pltpu.roll(q, half, 1) * sb
                k = k * c + pltpu.roll(k, LANES - half, 1) * sa + pltpu.roll(k, half, 1) * sb
            q_s[p, rows, :] = q
            k_s[p, rows, :] = k
            kt = k.T
            kt2 = jnp.concatenate([kt, kt], axis=0) * tables[p]["wk_fb_t"]
            u_s[p, n] = _dot(kt2.astype(BF16), src_ref[0, src_rows, part(p, 2)])
        if n % POOL_EVERY == POOL_EVERY - 1 and pool_tasks:
            pool_tasks.pop()()

    order_f = list(range(n_lat, n_all)) + list(range(n_lat))
    order_b = list(range(n_all - 1, -1, -1))
    for p in range(pairs):
        s = jnp.zeros((LANES, LANES), F32)
        for n in order_f:
            st_s[p, n, 0:LANES, :] = (s * blockdiag).astype(BF16)
            s = tables[p]["gf"] * s + u_s[p, n, 0:LANES, :]
        s = jnp.zeros((LANES, LANES), F32)
        for n in order_b:
            st_s[p, n, LANES:2 * LANES, :] = (s * blockdiag).astype(BF16)
            s = tables[p]["gb"] * s + u_s[p, n, LANES:2 * LANES, :]

    n_out = n_all if need_ctx else n_lat

    def scores_and_state(n):
        rows = slice(n * ch, (n + 1) * ch)
        out = []
        for p in range(pairs):
            q = q_s[p, rows, :]
            kb = k_s[p, rows, :].astype(BF16)
            q_ab = jnp.concatenate([jnp.where(lo, q, 0.0), jnp.where(lo, 0.0, q)], axis=0).astype(BF16)
            a_ab = _dot_nt(q_ab, kb)
            carried = _dot((jnp.concatenate([q, q], axis=1) * tables[p]["wq_fb"]).astype(BF16), st_s[p, n])
            out.append((a_ab, carried))
        return out

    def finish(n, staged):
        src_ref, src_rows, _ = source(n)
        dst_ref = ro_refs[0] if n < n_lat else ro_refs[1]
        for p, (a_ab, carried) in enumerate(staged):
            vb = src_ref[0, src_rows, part(p, 2)]
            gate = src_ref[0, src_rows, part(p, 3)].astype(F32)
            o_ab = _dot((a_ab * tables[p]["d_ab"]).astype(BF16), vb)
            o = jnp.where(lo, o_ab[0:ch], o_ab[ch:2 * ch]) + carried
            inv = 1.0 / RET_DIM
            mu = jnp.where(lo, jnp.sum(jnp.where(lo, o, 0.0), axis=-1, keepdims=True),
                           jnp.sum(jnp.where(lo, 0.0, o), axis=-1, keepdims=True)) * inv
            dlt = o - mu
            sq = dlt * dlt
            var = jnp.where(lo, jnp.sum(jnp.where(lo, sq, 0.0), axis=-1, keepdims=True),
                            jnp.sum(jnp.where(lo, 0.0, sq), axis=-1, keepdims=True)) * inv
            dst_ref[0, src_rows, p * LANES:(p + 1) * LANES] = (
                _silu(gate) * (dlt * lax.rsqrt(var + EPS))).astype(dst_ref.dtype)

    staged = {}
    for t in range(n_out + RET_LOOKAHEAD):
        if t < n_out:
            staged[t] = scores_and_state(t)
        n = t - RET_LOOKAHEAD
        if n < 0:
            continue
        if pool_tasks:
            pool_tasks.pop()()
        finish(n, staged.pop(n))
    while pool_tasks:
        pool_tasks.pop()()


def _ret_pool(lg, layer, ret_lat, ret_ctx, tabs, us, band, w_bd, scale, need_ctx):
    b, rl, _ = ret_lat.shape
    rc = ret_ctx.shape[1]
    ch = RET_CHUNK
    n_lat, n_ctx = rl // ch, rc // ch
    n_all = n_lat + n_ctx
    pairs = RET_HEADS // 2
    lengths = tuple(u.shape[1] for u in us)
    assert len(us) == (2 if need_ctx else 1)
    kern = functools.partial(_ret_pool_kernel, layer=layer, n_lat=n_lat, n_ctx=n_ctx, need_ctx=need_ctx)
    tab_spec = pl.BlockSpec((rl, LANES), lambda i: (0, 0), pipeline_mode=pl.Buffered(1))
    rows = lambda n, width: pl.BlockSpec((1, n, width), lambda i: (i, 0, 0))
    out_specs = [rows(n, RET_W) for n in lengths] + [rows(n, POOL_W) for n in lengths]
    out_shape = ([jax.ShapeDtypeStruct((b, n, RET_W), BF16) for n in lengths]
                 + [jax.ShapeDtypeStruct((b, n, POOL_W), BF16) for n in lengths])
    outs = pl.pallas_call(
        kern,
        grid=(b,),
        in_specs=[
            pl.BlockSpec(memory_space=pltpu.SMEM),
            rows(rl, 4 * RET_W), rows(rc, 4 * RET_W),
            tab_spec, tab_spec, tab_spec,
            *[rows(n, POOL_W) for n in lengths],
            pl.BlockSpec(band.shape, lambda i: (0, 0, 0), pipeline_mode=pl.Buffered(1)),
            _layer_spec(w_bd, layer), _layer_spec(scale, layer),
        ],
        out_specs=out_specs,
        out_shape=out_shape,
        scratch_shapes=[
            pltpu.VMEM((pairs, rl + rc, LANES), F32), pltpu.VMEM((pairs, rl + rc, LANES), F32),
            pltpu.VMEM((pairs, n_all, 2 * LANES, LANES), F32),
            pltpu.VMEM((pairs, n_all, 2 * LANES, LANES), BF16),
            pltpu.VMEM((max(lengths) + 2 * POOL_HALO, POOL_W), BF16),
        ],
        compiler_params=_params(("parallel",)),
        name="retention_pool",
    )(lg, ret_lat, ret_ctx, *tabs, *us, band, w_bd, scale)
    n = len(lengths)
    return outs[:n], outs[n:]


def _attn_kernel(*refs, use_lat):
    if use_lat:
        q_ref, kl_ref, vtl_ref, kc_ref, vtc_ref, o_ref = refs
    else:
        q_ref, kc_ref, vtc_ref, o_ref = refs

    sources = [(kc_ref, vtc_ref)]
    if use_lat:
        sources.append((kl_ref, vtl_ref))

    n_sub = q_ref.shape[2] // Q_SUB
    items = [((qs, h), k_ref, vt_ref, c0) for qs in range(n_sub) for h in range(MLA_HEADS)
             for k_ref, vt_ref in sources for c0 in range(0, k_ref.shape[2], KEY_CHUNK)]
    last_item = {g: i for i, (g, _, _, _) in enumerate(items)}

    scores, m_run, acc, done = {}, {}, {}, {}
    for t in range(len(items) + SCORE_LOOKAHEAD):
        if t < len(items):
            (qs, h), k_ref, _, c0 = items[t]
            q = q_ref[0, h, qs * Q_SUB:(qs + 1) * Q_SUB, :]
            scores[t] = _dot_nt(k_ref[0, h, c0:c0 + KEY_CHUNK, :], q)
        i = t - SCORE_LOOKAHEAD
        if i < 0:
            continue
        g, _, vt_ref, c0 = items[i]
        qs, h = g
        s = scores.pop(i)
        m_new = jnp.max(s, axis=0, keepdims=True)
        if g in m_run:
            m_new = jnp.maximum(m_run[g], m_new)
        pv = _dot(vt_ref[0, h, :, c0:c0 + KEY_CHUNK], jnp.exp2(s - m_new).astype(BF16))
        acc[g] = pv if g not in acc else acc[g] * jnp.exp2(m_run[g] - m_new) + pv
        m_run[g] = m_new
        if i == last_item[g]:
            a = acc.pop(g)
            done[g] = a[0:MLA_V, :] * (1.0 / a[MLA_V:MLA_V + 1, :])
            if h % 2 == 1:
                pair_t = jnp.concatenate([done.pop((qs, h - 1)), done.pop(g)], axis=0)
                o_ref[0, qs * Q_SUB:(qs + 1) * Q_SUB, (h // 2) * HEAD_PAD:(h // 2 + 1) * HEAD_PAD] = (
                    pair_t.T.astype(o_ref.dtype))


def _attention(q, k_ctx, vt_ctx, k_lat=None, vt_lat=None):
    b, nh, rows, _ = q.shape
    use_lat = k_lat is not None
    tile = min(ATTN_TILE, rows)

    def whole(a):
        return pl.BlockSpec((1,) + a.shape[1:], lambda i, j: (i, 0, 0, 0))

    in_specs = [pl.BlockSpec((1, nh, tile, HEAD_PAD), lambda i, j: (i, 0, j, 0))]
    args = [q]
    if use_lat:
        in_specs += [whole(k_lat), whole(vt_lat)]
        args += [k_lat, vt_lat]
    in_specs += [whole(k_ctx), whole(vt_ctx)]
    args += [k_ctx, vt_ctx]
    return pl.pallas_call(
        functools.partial(_attn_kernel, use_lat=use_lat),
        grid=(b, rows // tile),
        in_specs=in_specs,
        out_specs=pl.BlockSpec((1, tile, MLA_W), lambda i, j: (i, j, 0)),
        out_shape=jax.ShapeDtypeStruct((b, rows, MLA_W), BF16),
        compiler_params=_params(("parallel", "arbitrary")),
        name="mla_attention",
    )(*args)


def _mix_mlp_kernel(x_ref, r_ref, m_ref, p_ref, wo_ref, mod_ref, g_ref, w1_ref, w2_ref, gf_ref, o_ref,
                    *, final_norm):
    def mix(rows):
        y = (_dot(r_ref[0, rows, :], wo_ref[0, 0:RET_W, :])
             + _dot(m_ref[0, rows, :], wo_ref[0, RET_W:RET_W + MLA_W, :])
             + _dot(p_ref[0, rows, :], wo_ref[0, RET_W + MLA_W:, :]))
        x = x_ref[0, rows, :] + mod_ref[0, 0, 2:3, :] * y
        h = (_rms(x, g_ref[0]) * (1.0 + mod_ref[0, 0, 4:5, :]) + mod_ref[0, 0, 3:4, :]).astype(BF16)
        return x, h

    def mlp(x, h, rows):
        acc = None
        for c0 in range(0, w1_ref.shape[2], FF_CHUNK):
            a = jnp.maximum(_dot(h, w1_ref[0, :, c0:c0 + FF_CHUNK]), 0.0)
            part = _dot((a * a).astype(BF16), w2_ref[0, c0:c0 + FF_CHUNK, :])
            acc = part if acc is None else acc + part
        y = x + mod_ref[0, 0, 5:6, :] * acc
        if final_norm:
            y = _rms(y, gf_ref[...])
        o_ref[0, rows, :] = y

    sub = min(MLP_SUB, x_ref.shape[1])
    subs = [slice(r0, r0 + sub) for r0 in range(0, x_ref.shape[1], sub)]
    staged = mix(subs[0])
    for i, rows in enumerate(subs):
        x, h = staged
        if i + 1 < len(subs):
            staged = mix(subs[i + 1])
        mlp(x, h, rows)


def _mix_mlp(x, ret, mla, pool, mod, mod_row, layer, w_out, g, w1, w2, g_final, tile, final_norm):
    b, r, d = x.shape
    row = lambda width: pl.BlockSpec((1, tile, width), lambda i, j: (i, j, 0))
    return pl.pallas_call(
        functools.partial(_mix_mlp_kernel, final_norm=final_norm),
        grid=(b, r // tile),
        in_specs=[
            row(d), row(RET_W), row(MLA_W), row(POOL_W),
            _layer_spec(w_out, layer),
            pl.BlockSpec((1, 1) + mod.shape[2:], lambda i, j: (layer, mod_row(i), 0, 0)),
            _layer_spec(g, layer), _layer_spec(w1, layer), _layer_spec(w2, layer),
            pl.BlockSpec(g_final.shape, lambda i, j: (0, 0)),
        ],
        out_specs=row(d),
        out_shape=jax.ShapeDtypeStruct((b, r, d), F32),
        compiler_params=_params(("parallel", "parallel")),
        name="mix_mlp",
    )(x, ret, mla, pool, w_out, mod, g, w1, w2, g_final)


def _prep_w_uq(w_uq):
    depth, rank, _ = w_uq.shape
    half = MLA_ROPE // 2
    heads = w_uq.astype(BF16).reshape(depth, rank, MLA_HEADS, MLA_NOPE + MLA_ROPE)
    main = jnp.pad(heads, ((0, 0), (0, 0), (0, 0), (0, HEAD_PAD - MLA_NOPE - MLA_ROPE)))
    rope = heads[..., MLA_NOPE:]
    partner = jnp.concatenate([-rope[..., half:], rope[..., :half]], axis=-1)
    return jnp.concatenate([main.reshape(depth, rank, UQ_ROT),
                            partner.reshape(depth, rank, MLA_HEADS * MLA_ROPE)], axis=-1)


def _prep_w_ukv(w_ukv):
    depth, rank, _ = w_ukv.shape
    heads = w_ukv.astype(BF16).reshape(depth, rank, MLA_HEADS, MLA_NOPE + MLA_V)
    return jnp.concatenate([heads[..., :MLA_NOPE].reshape(depth, rank, UKV_V),
                            heads[..., MLA_NOPE:].reshape(depth, rank, UKV_COLS - UKV_V)], axis=-1)


def _prep_w_in_kernel(w_ref, o_ref):
    half = MLA_ROPE // 2
    o_ref[0, :, 0:COL_U] = w_ref[0, :, 0:COL_U].astype(o_ref.dtype)
    o_ref[0, :, COL_U:COL_KPE] = w_ref[0, :, COL_U + MLA_ROPE:COL_U + MLA_ROPE + POOL_W].astype(o_ref.dtype)
    v = w_ref[0, :, COL_U:COL_U + HEAD_PAD]
    lane = lax.broadcasted_iota(jnp.int32, (1, HEAD_PAD), 1)
    kx = jnp.where(lane < MLA_ROPE, v,
                   jnp.where(lane < MLA_ROPE + half, -pltpu.roll(v, half, 1),
                             jnp.where(lane < 2 * MLA_ROPE, pltpu.roll(v, MLA_ROPE + half, 1), 0.0)))
    o_ref[0, :, COL_KPE:IN_COLS] = kx.astype(o_ref.dtype)


def _prep_w_in(w_in):
    depth, d, n = w_in.shape
    assert n == COL_U + MLA_ROPE + POOL_W
    tile = min(PREP_TILE, d)
    assert d % tile == 0
    return pl.pallas_call(
        _prep_w_in_kernel,
        grid=(depth, d // tile),
        in_specs=[pl.BlockSpec((1, tile, n), lambda l, i: (l, i, 0))],
        out_specs=pl.BlockSpec((1, tile, IN_COLS), lambda l, i: (l, i, 0)),
        out_shape=jax.ShapeDtypeStruct((depth, d, IN_COLS), BF16),
        compiler_params=_params(("parallel", "parallel")),
        name="prep_w_in",
    )(w_in)


def _rope_angles(n_rows, dim):
    pos = np.arange(n_rows)
    n_freq = dim // 4
    inv = np.float32(ROPE_BASE) ** (-np.arange(n_freq, dtype=np.float32) / np.float32(n_freq))
    return np.concatenate([(pos // GRID_W).astype(np.float32)[:, None] * inv,
                           (pos % GRID_W).astype(np.float32)[:, None] * inv], axis=-1)


def _ret_rope_tables(n_rows):
    ang = _rope_angles(n_rows, RET_DIM)
    cos, sin = np.cos(ang), np.sin(ang)
    zero = np.zeros_like(sin)
    reps = LANES // RET_DIM
    c = np.tile(np.concatenate([cos, cos], axis=-1), (1, reps))
    sa = np.tile(np.concatenate([-sin, zero], axis=-1), (1, reps))
    sb = np.tile(np.concatenate([zero, sin], axis=-1), (1, reps))
    return tuple(jnp.asarray(a, F32) for a in (c, sa, sb))


def _mla_rope_tables(n_rows):
    ang = _rope_angles(n_rows, MLA_ROPE)
    cos, sin = np.cos(ang), np.sin(ang)
    c = np.ones((n_rows, HEAD_PAD), np.float32)
    s = np.zeros((n_rows, HEAD_PAD), np.float32)
    c[:, MLA_NOPE:MLA_NOPE + MLA_ROPE] = np.concatenate([cos, cos], axis=-1)
    s[:, MLA_NOPE:MLA_NOPE + MLA_ROPE] = np.concatenate([sin, sin], axis=-1)
    return jnp.asarray(c, F32), jnp.asarray(s, F32)


def _pool_bands():
    t = np.arange(POOL_TILE)[:, None]
    s = np.arange(POOL_TILE + 2 * POOL_HALO)[None, :]
    bands = [(s >= t + POOL_HALO - w // 2) & (s < t + POOL_HALO - w // 2 + w) for w in POOL_WINDOWS]
    return jnp.asarray(np.stack(bands).astype(np.float32), BF16)


def _block_diag(w_pool):
    depth, g, c, _ = w_pool.shape
    rows = []
    for i in range(g):
        blocks = [w_pool[:, i] if j == i else jnp.zeros((depth, c, c), w_pool.dtype) for j in range(g)]
        rows.append(jnp.concatenate(blocks, axis=2))
    return jnp.concatenate(rows, axis=1)


def kernel(x, c, ctx, c_ctx, w_ada, b_ada, norm_mix, w_in, q_norm, w_uq, kv_norm, w_ukv,
           ret_decay_logit, w_pool, pool_scale, w_out, norm_mlp, w_ff1, w_ff2, norm_final):
    b, n_lat, d = x.shape
    n_ctx = ctx.shape[1]
    depth = w_ada.shape[0]
    lat_tile, mlp_tile = min(LAT_TILE, n_lat), min(MLP_TILE, n_lat)
    assert n_lat % lat_tile == 0 and n_lat % mlp_tile == 0 and n_lat % min(ATTN_TILE, n_lat) == 0
    assert n_lat % GRID_W == 0 and n_lat % Q_SUB == 0
    assert n_ctx % RET_CHUNK == 0 and n_ctx % POOL_TILE == 0 and n_lat % POOL_TILE == 0
    assert d == 4 * RET_W and w_ff1.shape[2] % FF_CHUNK == 0

    ret_tabs = _ret_rope_tables(n_lat)
    mla_tabs = _mla_rope_tables(n_lat)

    rows = -(-(b + 1) // 8) * 8
    cc = jnp.zeros((rows, d), F32).at[:b].set(c).at[b].set(c_ctx)
    mod = _ada(cc, w_ada, b_ada).reshape(depth, rows, 6, d)
    lat_row, ctx_row = (lambda i: i), (lambda i: b)
    lg = jax.nn.log_sigmoid(ret_decay_logit.astype(F32))

    def rows3(a):
        return a.reshape(depth, 1, a.shape[-1])

    proj_w = (rows3(norm_mix), _prep_w_in(w_in), rows3(q_norm), _prep_w_uq(w_uq),
              rows3(kv_norm), _prep_w_ukv(w_ukv))
    pool_w = (_pool_bands(), _block_diag(w_pool).astype(BF16), rows3(pool_scale))
    mlp_w = (w_out.astype(BF16), rows3(norm_mlp), w_ff1.astype(BF16), w_ff2.astype(BF16), norm_final[None])

    h_ctx = ctx
    for l in range(depth):
        last = l == depth - 1
        ret_l, u_l, q_l, k_l, vt_l = _inproj(x, mod, lat_row, l, *proj_w, mla_tabs, lat_tile)
        ret_c, u_c, q_c, k_c, vt_c = _inproj(h_ctx, mod, ctx_row, l, *proj_w, None, n_ctx)
        ret_o, pool_o = _ret_pool(lg, l, ret_l, ret_c, ret_tabs, (u_l,) if last else (u_l, u_c), *pool_w, not last)
        mla_l = _attention(q_l, k_c, vt_c, k_l, vt_l)
        if not last:
            mla_c = _attention(q_c, k_c, vt_c)
            h_ctx = _mix_mlp(h_ctx, ret_o[1], mla_c, pool_o[1], mod, ctx_row, l, *mlp_w, n_ctx, False)
        x = _mix_mlp(x, ret_o[0], mla_l, pool_o[0], mod, lat_row, l, *mlp_w, mlp_tile, last)
    return x
```

```python
import functools
import math

import jax
import jax.numpy as jnp
import numpy as np
from jax import lax
from jax.experimental import pallas as pl
from jax.experimental.pallas import tpu as pltpu

GRID_W = 64
RET_HEADS = 4
RET_DIM = 64
RET_W = RET_HEADS * RET_DIM
RET_CHUNK = 128
MLA_HEADS = 8
MLA_NOPE = 64
MLA_ROPE = 32
MLA_V = 64
MLA_Q_RANK = 256
MLA_KV_RANK = 128
MLA_W = MLA_HEADS * MLA_V
POOL_GROUPS = 4
POOL_WINDOWS = (2, 4, 8, 16)
POOL_GDIM = 64
POOL_W = POOL_GROUPS * POOL_GDIM
ROPE_BASE = 10000.0
EPS = 1e-6

LANES = 128
LAT_TILE = 1024
MLP_TILE = 1024
IN_SUB = 256
ATTN_TILE = 1024
Q_SUB = 256
KEY_CHUNK = 256
SCORE_LOOKAHEAD = 5
POOL_TILE = 256
POOL_EVERY = 2
RET_LOOKAHEAD = 1
ADA_COL_BLOCKS = 4
PREP_TILE = 256
HEAD_PAD = 128
VT_ROWS = MLA_V + 16
POOL_HALO = 128
FF_CHUNK = 1024
MLP_SUB = 256
VMEM_LIMIT = 56 * 1024 * 1024

COL_RET = 0
COL_CQ = 4 * RET_W
COL_CKV = COL_CQ + MLA_Q_RANK
COL_U = COL_CKV + MLA_KV_RANK
COL_KPE = COL_U + POOL_W
IN_COLS = COL_KPE + HEAD_PAD
UQ_ROT = MLA_HEADS * HEAD_PAD
UQ_COLS = UQ_ROT + MLA_HEADS * MLA_ROPE
UKV_V = MLA_HEADS * MLA_NOPE
UKV_COLS = UKV_V + MLA_HEADS * MLA_V

F32 = jnp.float32
BF16 = jnp.bfloat16


def _params(sem):
    return pltpu.CompilerParams(dimension_semantics=sem, vmem_limit_bytes=VMEM_LIMIT)


def _layer_spec(a, layer):
    tail = a.shape[1:]
    return pl.BlockSpec((1,) + tail, lambda *_: (layer,) + (0,) * len(tail), pipeline_mode=pl.Buffered(1))


def _dot(a, b):
    return jnp.dot(a, b, preferred_element_type=F32)


def _dot_nt(a, b):
    return lax.dot_general(a, b, (((1,), (1,)), ((), ())), preferred_element_type=F32)


def _rms(x, g):
    return x * lax.rsqrt(jnp.mean(x * x, axis=-1, keepdims=True) + EPS) * g


def _silu(x):
    return x * (1.0 / (1.0 + jnp.exp(-x)))


def _ada_kernel(c_ref, w_ref, b_ref, o_ref):
    s = _silu(c_ref[...]).astype(BF16)
    o_ref[0] = _dot(s, w_ref[0].astype(BF16)) + b_ref[0]


def _ada(cc, w_ada, b_ada):
    depth, d, n = w_ada.shape
    rows = cc.shape[0]
    tn = n // ADA_COL_BLOCKS
    assert n % ADA_COL_BLOCKS == 0 and tn % LANES == 0
    return pl.pallas_call(
        _ada_kernel,
        grid=(depth, n // tn),
        in_specs=[
            pl.BlockSpec((rows, d), lambda l, j: (0, 0)),
            pl.BlockSpec((1, d, tn), lambda l, j: (l, 0, j)),
            pl.BlockSpec((1, 1, tn), lambda l, j: (l, 0, j)),
        ],
        out_specs=pl.BlockSpec((1, rows, tn), lambda l, j: (l, 0, j)),
        out_shape=jax.ShapeDtypeStruct((depth, rows, n), F32),
        compiler_params=_params(("parallel", "parallel")),
        name="ada_mod",
    )(cc, w_ada, b_ada.reshape(depth, 1, n))


def _inproj_kernel(*refs, rope, full):
    x_ref, mod_ref, g_ref, w_ref, gq_ref, wq_ref, gkv_ref, wkv_ref = refs[:8]
    c_ref, s_ref = refs[8:10] if rope else (None, None)
    if full:
        ret_ref, u_ref, q_ref, k_ref, vt_ref = refs[-5:]
    else:
        ret_ref, k_ref, vt_ref = refs[-3:]
    nh = MLA_HEADS
    lane = lax.broadcasted_iota(jnp.int32, (1, HEAD_PAD), 1)
    nope_lanes = lane < MLA_NOPE
    rope_lanes = jnp.logical_and(lane >= MLA_NOPE, lane < MLA_NOPE + MLA_ROPE)
    sub = min(IN_SUB, x_ref.shape[1])
    den_rows = (lax.broadcasted_iota(jnp.int32, (VT_ROWS - MLA_V, sub), 0) == 0).astype(vt_ref.dtype)
    q_scale = (MLA_NOPE + MLA_ROPE) ** -0.5 * math.log2(math.e)

    def project(rows):
        h = (_rms(x_ref[0, rows, :], g_ref[0]) * (1.0 + mod_ref[0, 0, 1:2, :]) + mod_ref[0, 0, 0:1, :]).astype(BF16)
        if full:
            p = _dot(h, w_ref[0])
            ret_ref[0, rows, :] = p[:, COL_RET:COL_CQ].astype(ret_ref.dtype)
            u_ref[0, rows, :] = p[:, COL_U:COL_KPE].astype(u_ref.dtype)
            return p[:, COL_CQ:COL_CKV], p[:, COL_CKV:COL_U], p[:, COL_KPE:IN_COLS]
        ret_ref[0, rows, :] = _dot(h, w_ref[0, :, RET_W:3 * RET_W]).astype(ret_ref.dtype)
        return None, _dot(h, w_ref[0, :, COL_CKV:COL_U]), _dot(h, w_ref[0, :, COL_KPE:IN_COLS])

    def up_project(p, rows):
        cq, ckv, kx = p
        if full:
            yq = _rms(cq, gq_ref[0]).astype(BF16)
            if rope:
                q2 = _dot(yq, wq_ref[0])
            else:
                q2 = _dot(yq, wq_ref[0, :, 0:UQ_ROT])
        if rope:
            cos, sin = c_ref[rows, :], s_ref[rows, :]
        ykv = _rms(ckv, gkv_ref[0]).astype(BF16)
        kv = _dot(ykv, wkv_ref[0])
        kpe = pltpu.roll(kx, MLA_NOPE, 1)
        if rope:
            kpe = kpe * cos + pltpu.roll(kx, MLA_NOPE - MLA_ROPE, 1) * sin
        kpe = jnp.where(rope_lanes, kpe, 0.0)
        for hh in range(nh):
            if full:
                qh = q2[:, hh * HEAD_PAD:(hh + 1) * HEAD_PAD]
                if rope:
                    blk, j = divmod(hh * MLA_ROPE, HEAD_PAD)
                    rot = q2[:, UQ_ROT + blk * HEAD_PAD:UQ_ROT + (blk + 1) * HEAD_PAD]
                    shift = (MLA_NOPE - j) % HEAD_PAD
                    qh = qh * cos + (pltpu.roll(rot, shift, 1) if shift else rot) * sin
                q_ref[0, hh, rows, :] = (qh * q_scale).astype(q_ref.dtype)
            blk, j = divmod(hh * MLA_NOPE, HEAD_PAD)
            kn = kv[:, blk * HEAD_PAD:(blk + 1) * HEAD_PAD]
            if j:
                kn = pltpu.roll(kn, HEAD_PAD - j, 1)
            k_ref[0, hh, rows, :] = jnp.where(nope_lanes, kn, kpe).astype(k_ref.dtype)
        for blk in range(nh * MLA_V // HEAD_PAD):
            vt = kv[:, UKV_V + blk * HEAD_PAD:UKV_V + (blk + 1) * HEAD_PAD].T.astype(vt_ref.dtype)
            for j in range(HEAD_PAD // MLA_V):
                hh = blk * (HEAD_PAD // MLA_V) + j
                vt_ref[0, hh, 0:MLA_V, rows] = vt[j * MLA_V:(j + 1) * MLA_V, :]
                vt_ref[0, hh, MLA_V:VT_ROWS, rows] = den_rows

    subs = [slice(r0, r0 + sub) for r0 in range(0, x_ref.shape[1], sub)]
    p_next = project(subs[0])
    for i, rows in enumerate(subs):
        p = p_next
        if i + 1 < len(subs):
            p_next = project(subs[i + 1])
        up_project(p, rows)


def _inproj(x, mod, mod_row, layer, g, w, gq, wq, gkv, wkv, tabs, tile, full=True):
    b, r, d = x.shape
    rope = tabs is not None
    assert full or not rope
    nh = MLA_HEADS
    row = lambda width: pl.BlockSpec((1, tile, width), lambda i, j: (i, j, 0))
    head = pl.BlockSpec((1, nh, tile, HEAD_PAD), lambda i, j: (i, 0, j, 0))
    head_t = pl.BlockSpec((1, nh, VT_ROWS, tile), lambda i, j: (i, 0, 0, j))
    assert tile % min(IN_SUB, tile) == 0
    in_specs = [
        row(d),
        pl.BlockSpec((1, 1) + mod.shape[2:], lambda i, j: (layer, mod_row(i), 0, 0)),
    ] + [_layer_spec(a, layer) for a in (g, w, gq, wq, gkv, wkv)]
    args = [x, mod, g, w, gq, wq, gkv, wkv]
    if rope:
        in_specs += [pl.BlockSpec((tile, HEAD_PAD), lambda i, j: (j, 0))] * 2
        args += list(tabs)
    hshape = jax.ShapeDtypeStruct((b, nh, r, HEAD_PAD), BF16)
    vt_shape = jax.ShapeDtypeStruct((b, nh, VT_ROWS, r), BF16)
    if full:
        out_specs = [row(4 * RET_W), row(POOL_W), head, head, head_t]
        out_shape = [jax.ShapeDtypeStruct((b, r, 4 * RET_W), BF16),
                     jax.ShapeDtypeStruct((b, r, POOL_W), BF16), hshape, hshape, vt_shape]
    else:
        out_specs = [row(2 * RET_W), head, head_t]
        out_shape = [jax.ShapeDtypeStruct((b, r, 2 * RET_W), BF16), hshape, vt_shape]
    return pl.pallas_call(
        functools.partial(_inproj_kernel, rope=rope, full=full),
        grid=(b, r // tile),
        in_specs=in_specs,
        out_specs=out_specs,
        out_shape=out_shape,
        compiler_params=_params(("parallel", "parallel")),
        name="in_proj",
    )(*args)


def _ret_pool_kernel(*refs, layer, n_lat, n_ctx, need_ctx):
    n_seg = 2 if need_ctx else 1
    it = iter(refs)
    lg_ref, rl_ref, rc_ref, c_ref, sa_ref, sb_ref = [next(it) for _ in range(6)]
    u_refs = [next(it) for _ in range(n_seg)]
    band_ref, w_ref, s_ref = next(it), next(it), next(it)
    ro_refs = [next(it) for _ in range(n_seg)]
    po_refs = [next(it) for _ in range(n_seg)]
    q_s, k_s, u_s, st_s, pad_s = it
    ch = RET_CHUNK
    pairs = RET_HEADS // 2
    n_all = n_lat + n_ctx
    lane = lax.broadcasted_iota(jnp.int32, (1, LANES), 1)
    lo = lane < RET_DIM
    sub_lo = lax.broadcasted_iota(jnp.int32, (LANES, 1), 0) < RET_DIM
    row = lax.broadcasted_iota(jnp.int32, (ch, 1), 0).astype(F32)
    col = lax.broadcasted_iota(jnp.int32, (1, ch), 1).astype(F32)
    diff = row - col
    head_of_row = lax.broadcasted_iota(jnp.int32, (LANES, 1), 0) // RET_DIM
    blockdiag = (head_of_row == lane // RET_DIM).astype(F32)
    k_scale = RET_DIM ** -0.5
    half = RET_DIM // 2

    def pair_tables(p):
        lf_a, lf_b = lg_ref[layer, 0, 2 * p], lg_ref[layer, 0, 2 * p + 1]
        lb_a, lb_b = lg_ref[layer, 1, 2 * p], lg_ref[layer, 1, 2 * p + 1]
        lf = jnp.where(lo, lf_a, lf_b)
        lb = jnp.where(lo, lb_a, lb_b)
        lf_t = jnp.where(sub_lo, lf_a, lf_b)
        lb_t = jnp.where(sub_lo, lb_a, lb_b)

        def decay(l_f, l_b):
            return jnp.where(diff >= 0, jnp.exp(l_f * jnp.maximum(diff, 0.0)),
                             jnp.exp(l_b * jnp.maximum(-diff, 0.0)))

        return dict(
            d_ab=jnp.concatenate([decay(lf_a, lb_a), decay(lf_b, lb_b)], axis=0) * k_scale,
            wq_fb=jnp.concatenate([jnp.exp(lf * (row + 1.0)), jnp.exp(lb * (ch - row))], axis=1),
            wk_fb_t=jnp.concatenate([jnp.exp(lf_t * (ch - 1.0 - col)), jnp.exp(lb_t * col)], axis=0) * k_scale,
            gf=jnp.exp(lf * ch), gb=jnp.exp(lb * ch))

    tables = [pair_tables(p) for p in range(pairs)]

    def source(n):
        if n < n_lat:
            return rl_ref, slice(n * ch, (n + 1) * ch), True
        return rc_ref, slice((n - n_lat) * ch, (n - n_lat + 1) * ch), False

    ctx_kv_only = rc_ref.shape[2] == 2 * RET_W
    assert ctx_kv_only == (not need_ctx)

    def part(p, which, src_ref=None):
        if src_ref is rc_ref and ctx_kv_only:
            which -= 1
        c0 = (which * pairs + p) * LANES
        return slice(c0, c0 + LANES)

    lo_pool = lane < POOL_GDIM
    halo = POOL_HALO

    def pool_fill(u_ref, length):
        pad_s[0:halo, :] = jnp.zeros((halo, POOL_W), BF16)
        pad_s[halo:halo + length, :] = u_ref[0]
        pad_s[halo + length:2 * halo + length, :] = jnp.zeros((halo, POOL_W), BF16)

    def pool_slab(u_ref, o_ref, length, r0):
        t = (lax.broadcasted_iota(jnp.int32, (POOL_TILE, 1), 0) + r0).astype(F32)
        cols = []
        for hf in range(POOL_W // LANES):
            cs = slice(hf * LANES, (hf + 1) * LANES)
            tokens = pad_s[r0:r0 + POOL_TILE + 2 * halo, cs]

            def inv_count(w):
                if r0 - w // 2 >= 0 and r0 + POOL_TILE - w // 2 + w <= length:
                    return 1.0 / w
                cnt = (jnp.clip(t - w // 2 + w, 0.0, float(length))
                       - jnp.clip(t - w // 2, 0.0, float(length)))
                return 1.0 / cnt

            w_n, w_w = POOL_WINDOWS[2 * hf], POOL_WINDOWS[2 * hf + 1]
            pooled = jnp.where(lo_pool, _dot(band_ref[2 * hf], tokens) * inv_count(w_n),
                               _dot(band_ref[2 * hf + 1], tokens) * inv_count(w_w))
            cols.append(pooled - u_ref[0, r0:r0 + POOL_TILE, cs].astype(F32))
        pooled = jnp.concatenate(cols, axis=-1).astype(BF16)
        y = _dot(pooled, w_ref[0]) * s_ref[0]
        o_ref[0, r0:r0 + POOL_TILE, :] = y.astype(o_ref.dtype)

    pool_tasks = []
    for u_ref, o_ref in zip(u_refs, po_refs):
        length = u_ref.shape[1]
        pool_tasks.append(functools.partial(pool_fill, u_ref, length))
        pool_tasks += [functools.partial(pool_slab, u_ref, o_ref, length, r0)
                       for r0 in range(0, length, POOL_TILE)]
    pool_tasks.reverse()

    for n in range(n_all):
        src_ref, src_rows, roped = source(n)
        rows = slice(n * ch, (n + 1) * ch)
        queried = n < n_lat or need_ctx
        for p in range(pairs):
            k = src_ref[0, src_rows, part(p, 1, src_ref)].astype(F32)
            if roped:
                c, sa, sb = c_ref[src_rows, :], sa_ref[src_rows, :], sb_ref[src_rows, :]
                k = k * c + pltpu.roll(k, LANES - half, 1) * sa + pltpu.roll(k, half, 1) * sb
            if queried:
                q = src_ref[0, src_rows, part(p, 0, src_ref)].astype(F32)
                if roped:
                    q = q * c + pltpu.roll(q, LANES - half, 1) * sa + pltpu.roll(q, half, 1) * sb
                q_s[p, rows, :] = q
                k_s[p, rows, :] = k
            kt = k.T
            kt2 = jnp.concatenate([kt, kt], axis=0) * tables[p]["wk_fb_t"]
            u_s[p, n] = _dot(kt2.astype(BF16), src_ref[0, src_rows, part(p, 2, src_ref)])
        if n % POOL_EVERY == POOL_EVERY - 1 and pool_tasks:
            pool_tasks.pop()()

    order_f = list(range(n_lat, n_all)) + list(range(n_lat))
    order_b = list(range(n_all - 1, -1, -1))
    for p in range(pairs):
        s = jnp.zeros((LANES, LANES), F32)
        for n in order_f:
            st_s[p, n, 0:LANES, :] = (s * blockdiag).astype(BF16)
            s = tables[p]["gf"] * s + u_s[p, n, 0:LANES, :]
        s = jnp.zeros((LANES, LANES), F32)
        for n in order_b:
            st_s[p, n, LANES:2 * LANES, :] = (s * blockdiag).astype(BF16)
            s = tables[p]["gb"] * s + u_s[p, n, LANES:2 * LANES, :]

    n_out = n_all if need_ctx else n_lat

    def scores_and_state(n):
        rows = slice(n * ch, (n + 1) * ch)
        out = []
        for p in range(pairs):
            q = q_s[p, rows, :]
            kb = k_s[p, rows, :].astype(BF16)
            q_ab = jnp.concatenate([jnp.where(lo, q, 0.0), jnp.where(lo, 0.0, q)], axis=0).astype(BF16)
            a_ab = _dot_nt(q_ab, kb)
            carried = _dot((jnp.concatenate([q, q], axis=1) * tables[p]["wq_fb"]).astype(BF16), st_s[p, n])
            out.append((a_ab, carried))
        return out

    def finish(n, staged):
        src_ref, src_rows, _ = source(n)
        dst_ref = ro_refs[0] if n < n_lat else ro_refs[1]
        for p, (a_ab, carried) in enumerate(staged):
            vb = src_ref[0, src_rows, part(p, 2)]
            gate = src_ref[0, src_rows, part(p, 3)].astype(F32)
            o_ab = _dot((a_ab * tables[p]["d_ab"]).astype(BF16), vb)
            o = jnp.where(lo, o_ab[0:ch], o_ab[ch:2 * ch]) + carried
            inv = 1.0 / RET_DIM
            mu = jnp.where(lo, jnp.sum(jnp.where(lo, o, 0.0), axis=-1, keepdims=True),
                           jnp.sum(jnp.where(lo, 0.0, o), axis=-1, keepdims=True)) * inv
            dlt = o - mu
            sq = dlt * dlt
            var = jnp.where(lo, jnp.sum(jnp.where(lo, sq, 0.0), axis=-1, keepdims=True),
                            jnp.sum(jnp.where(lo, 0.0, sq), axis=-1, keepdims=True)) * inv
            dst_ref[0, src_rows, p * LANES:(p + 1) * LANES] = (
                _silu(gate) * (dlt * lax.rsqrt(var + EPS))).astype(dst_ref.dtype)

    staged = {}
    for t in range(n_out + RET_LOOKAHEAD):
        if t < n_out:
            staged[t] = scores_and_state(t)
        n = t - RET_LOOKAHEAD
        if n < 0:
            continue
        if pool_tasks:
            pool_tasks.pop()()
        finish(n, staged.pop(n))
    while pool_tasks:
        pool_tasks.pop()()


def _ret_pool(lg, layer, ret_lat, ret_ctx, tabs, us, band, w_bd, scale, need_ctx):
    b, rl, _ = ret_lat.shape
    rc = ret_ctx.shape[1]
    ch = RET_CHUNK
    n_lat, n_ctx = rl // ch, rc // ch
    n_all = n_lat + n_ctx
    pairs = RET_HEADS // 2
    lengths = tuple(u.shape[1] for u in us)
    assert len(us) == (2 if need_ctx else 1)
    kern = functools.partial(_ret_pool_kernel, layer=layer, n_lat=n_lat, n_ctx=n_ctx, need_ctx=need_ctx)
    tab_spec = pl.BlockSpec((rl, LANES), lambda i: (0, 0), pipeline_mode=pl.Buffered(1))
    rows = lambda n, width: pl.BlockSpec((1, n, width), lambda i: (i, 0, 0))
    out_specs = [rows(n, RET_W) for n in lengths] + [rows(n, POOL_W) for n in lengths]
    out_shape = ([jax.ShapeDtypeStruct((b, n, RET_W), BF16) for n in lengths]
                 + [jax.ShapeDtypeStruct((b, n, POOL_W), BF16) for n in lengths])
    outs = pl.pallas_call(
        kern,
        grid=(b,),
        in_specs=[
            pl.BlockSpec(memory_space=pltpu.SMEM),
            rows(rl, 4 * RET_W), rows(rc, ret_ctx.shape[2]),
            tab_spec, tab_spec, tab_spec,
            *[rows(n, POOL_W) for n in lengths],
            pl.BlockSpec(band.shape, lambda i: (0, 0, 0), pipeline_mode=pl.Buffered(1)),
            _layer_spec(w_bd, layer), _layer_spec(scale, layer),
        ],
        out_specs=out_specs,
        out_shape=out_shape,
        scratch_shapes=[
            pltpu.VMEM((pairs, rl + rc, LANES), F32), pltpu.VMEM((pairs, rl + rc, LANES), F32),
            pltpu.VMEM((pairs, n_all, 2 * LANES, LANES), F32),
            pltpu.VMEM((pairs, n_all, 2 * LANES, LANES), BF16),
            pltpu.VMEM((max(lengths) + 2 * POOL_HALO, POOL_W), BF16),
        ],
        compiler_params=_params(("parallel",)),
        name="retention_pool",
    )(lg, ret_lat, ret_ctx, *tabs, *us, band, w_bd, scale)
    n = len(lengths)
    return outs[:n], outs[n:]


def _attn_kernel(*refs, use_lat):
    if use_lat:
        q_ref, kl_ref, vtl_ref, kc_ref, vtc_ref, o_ref = refs
    else:
        q_ref, kc_ref, vtc_ref, o_ref = refs

    sources = [(kc_ref, vtc_ref)]
    if use_lat:
        sources.append((kl_ref, vtl_ref))

    n_sub = q_ref.shape[2] // Q_SUB
    items = [((qs, h), k_ref, vt_ref, c0) for qs in range(n_sub) for h in range(MLA_HEADS)
             for k_ref, vt_ref in sources for c0 in range(0, k_ref.shape[2], KEY_CHUNK)]
    last_item = {g: i for i, (g, _, _, _) in enumerate(items)}

    scores, m_run, acc, done = {}, {}, {}, {}
    for t in range(len(items) + SCORE_LOOKAHEAD):
        if t < len(items):
            (qs, h), k_ref, _, c0 = items[t]
            q = q_ref[0, h, qs * Q_SUB:(qs + 1) * Q_SUB, :]
            scores[t] = _dot_nt(k_ref[0, h, c0:c0 + KEY_CHUNK, :], q)
        i = t - SCORE_LOOKAHEAD
        if i < 0:
            continue
        g, _, vt_ref, c0 = items[i]
        qs, h = g
        s = scores.pop(i)
        m_new = jnp.max(s, axis=0, keepdims=True)
        if g in m_run:
            m_new = jnp.maximum(m_run[g], m_new)
        pv = _dot(vt_ref[0, h, :, c0:c0 + KEY_CHUNK], jnp.exp2(s - m_new).astype(BF16))
        acc[g] = pv if g not in acc else acc[g] * jnp.exp2(m_run[g] - m_new) + pv
        m_run[g] = m_new
        if i == last_item[g]:
            a = acc.pop(g)
            done[g] = a[0:MLA_V, :] * (1.0 / a[MLA_V:MLA_V + 1, :])
            if h % 2 == 1:
                pair_t = jnp.concatenate([done.pop((qs, h - 1)), done.pop(g)], axis=0)
                o_ref[0, qs * Q_SUB:(qs + 1) * Q_SUB, (h // 2) * HEAD_PAD:(h // 2 + 1) * HEAD_PAD] = (
                    pair_t.T.astype(o_ref.dtype))


def _attention(q, k_ctx, vt_ctx, k_lat=None, vt_lat=None):
    b, nh, rows, _ = q.shape
    use_lat = k_lat is not None
    tile = min(ATTN_TILE, rows)

    def whole(a):
        return pl.BlockSpec((1,) + a.shape[1:], lambda i, j: (i, 0, 0, 0))

    in_specs = [pl.BlockSpec((1, nh, tile, HEAD_PAD), lambda i, j: (i, 0, j, 0))]
    args = [q]
    if use_lat:
        in_specs += [whole(k_lat), whole(vt_lat)]
        args += [k_lat, vt_lat]
    in_specs += [whole(k_ctx), whole(vt_ctx)]
    args += [k_ctx, vt_ctx]
    return pl.pallas_call(
        functools.partial(_attn_kernel, use_lat=use_lat),
        grid=(b, rows // tile),
        in_specs=in_specs,
        out_specs=pl.BlockSpec((1, tile, MLA_W), lambda i, j: (i, j, 0)),
        out_shape=jax.ShapeDtypeStruct((b, rows, MLA_W), BF16),
        compiler_params=_params(("parallel", "arbitrary")),
        name="mla_attention",
    )(*args)


def _mix_mlp_kernel(x_ref, r_ref, m_ref, p_ref, wo_ref, mod_ref, g_ref, w1_ref, w2_ref, gf_ref, o_ref,
                    *, final_norm):
    def mix(rows):
        y = (_dot(r_ref[0, rows, :], wo_ref[0, 0:RET_W, :])
             + _dot(m_ref[0, rows, :], wo_ref[0, RET_W:RET_W + MLA_W, :])
             + _dot(p_ref[0, rows, :], wo_ref[0, RET_W + MLA_W:, :]))
        x = x_ref[0, rows, :] + mod_ref[0, 0, 2:3, :] * y
        h = (_rms(x, g_ref[0]) * (1.0 + mod_ref[0, 0, 4:5, :]) + mod_ref[0, 0, 3:4, :]).astype(BF16)
        return x, h

    def mlp(x, h, rows):
        acc = None
        for c0 in range(0, w1_ref.shape[2], FF_CHUNK):
            a = jnp.maximum(_dot(h, w1_ref[0, :, c0:c0 + FF_CHUNK]), 0.0)
            part = _dot((a * a).astype(BF16), w2_ref[0, c0:c0 + FF_CHUNK, :])
            acc = part if acc is None else acc + part
        y = x + mod_ref[0, 0, 5:6, :] * acc
        if final_norm:
            y = _rms(y, gf_ref[...])
        o_ref[0, rows, :] = y

    sub = min(MLP_SUB, x_ref.shape[1])
    subs = [slice(r0, r0 + sub) for r0 in range(0, x_ref.shape[1], sub)]
    staged = mix(subs[0])
    for i, rows in enumerate(subs):
        x, h = staged
        if i + 1 < len(subs):
            staged = mix(subs[i + 1])
        mlp(x, h, rows)


def _mix_mlp(x, ret, mla, pool, mod, mod_row, layer, w_out, g, w1, w2, g_final, tile, final_norm):
    b, r, d = x.shape
    row = lambda width: pl.BlockSpec((1, tile, width), lambda i, j: (i, j, 0))
    return pl.pallas_call(
        functools.partial(_mix_mlp_kernel, final_norm=final_norm),
        grid=(b, r // tile),
        in_specs=[
            row(d), row(RET_W), row(MLA_W), row(POOL_W),
            _layer_spec(w_out, layer),
            pl.BlockSpec((1, 1) + mod.shape[2:], lambda i, j: (layer, mod_row(i), 0, 0)),
            _layer_spec(g, layer), _layer_spec(w1, layer), _layer_spec(w2, layer),
            pl.BlockSpec(g_final.shape, lambda i, j: (0, 0)),
        ],
        out_specs=row(d),
        out_shape=jax.ShapeDtypeStruct((b, r, d), F32),
        compiler_params=_params(("parallel", "parallel")),
        name="mix_mlp",
    )(x, ret, mla, pool, w_out, mod, g, w1, w2, g_final)


def _prep_w_uq(w_uq):
    depth, rank, _ = w_uq.shape
    half = MLA_ROPE // 2
    heads = w_uq.astype(BF16).reshape(depth, rank, MLA_HEADS, MLA_NOPE + MLA_ROPE)
    main = jnp.pad(heads, ((0, 0), (0, 0), (0, 0), (0, HEAD_PAD - MLA_NOPE - MLA_ROPE)))
    rope = heads[..., MLA_NOPE:]
    partner = jnp.concatenate([-rope[..., half:], rope[..., :half]], axis=-1)
    return jnp.concatenate([main.reshape(depth, rank, UQ_ROT),
                            partner.reshape(depth, rank, MLA_HEADS * MLA_ROPE)], axis=-1)


def _prep_w_ukv(w_ukv):
    depth, rank, _ = w_ukv.shape
    heads = w_ukv.astype(BF16).reshape(depth, rank, MLA_HEADS, MLA_NOPE + MLA_V)
    return jnp.concatenate([heads[..., :MLA_NOPE].reshape(depth, rank, UKV_V),
                            heads[..., MLA_NOPE:].reshape(depth, rank, UKV_COLS - UKV_V)], axis=-1)


def _prep_w_in_kernel(w_ref, o_ref):
    half = MLA_ROPE // 2
    o_ref[0, :, 0:COL_U] = w_ref[0, :, 0:COL_U].astype(o_ref.dtype)
    o_ref[0, :, COL_U:COL_KPE] = w_ref[0, :, COL_U + MLA_ROPE:COL_U + MLA_ROPE + POOL_W].astype(o_ref.dtype)
    v = w_ref[0, :, COL_U:COL_U + HEAD_PAD]
    lane = lax.broadcasted_iota(jnp.int32, (1, HEAD_PAD), 1)
    kx = jnp.where(lane < MLA_ROPE, v,
                   jnp.where(lane < MLA_ROPE + half, -pltpu.roll(v, half, 1),
                             jnp.where(lane < 2 * MLA_ROPE, pltpu.roll(v, MLA_ROPE + half, 1), 0.0)))
    o_ref[0, :, COL_KPE:IN_COLS] = kx.astype(o_ref.dtype)


def _prep_w_in(w_in):
    depth, d, n = w_in.shape
    assert n == COL_U + MLA_ROPE + POOL_W
    tile = min(PREP_TILE, d)
    assert d % tile == 0
    return pl.pallas_call(
        _prep_w_in_kernel,
        grid=(depth, d // tile),
        in_specs=[pl.BlockSpec((1, tile, n), lambda l, i: (l, i, 0))],
        out_specs=pl.BlockSpec((1, tile, IN_COLS), lambda l, i: (l, i, 0)),
        out_shape=jax.ShapeDtypeStruct((depth, d, IN_COLS), BF16),
        compiler_params=_params(("parallel", "parallel")),
        name="prep_w_in",
    )(w_in)


def _rope_angles(n_rows, dim):
    pos = np.arange(n_rows)
    n_freq = dim // 4
    inv = np.float32(ROPE_BASE) ** (-np.arange(n_freq, dtype=np.float32) / np.float32(n_freq))
    return np.concatenate([(pos // GRID_W).astype(np.float32)[:, None] * inv,
                           (pos % GRID_W).astype(np.float32)[:, None] * inv], axis=-1)


def _ret_rope_tables(n_rows):
    ang = _rope_angles(n_rows, RET_DIM)
    cos, sin = np.cos(ang), np.sin(ang)
    zero = np.zeros_like(sin)
    reps = LANES // RET_DIM
    c = np.tile(np.concatenate([cos, cos], axis=-1), (1, reps))
    sa = np.tile(np.concatenate([-sin, zero], axis=-1), (1, reps))
    sb = np.tile(np.concatenate([zero, sin], axis=-1), (1, reps))
    return tuple(jnp.asarray(a, F32) for a in (c, sa, sb))


def _mla_rope_tables(n_rows):
    ang = _rope_angles(n_rows, MLA_ROPE)
    cos, sin = np.cos(ang), np.sin(ang)
    c = np.ones((n_rows, HEAD_PAD), np.float32)
    s = np.zeros((n_rows, HEAD_PAD), np.float32)
    c[:, MLA_NOPE:MLA_NOPE + MLA_ROPE] = np.concatenate([cos, cos], axis=-1)
    s[:, MLA_NOPE:MLA_NOPE + MLA_ROPE] = np.concatenate([sin, sin], axis=-1)
    return jnp.asarray(c, F32), jnp.asarray(s, F32)


def _pool_bands():
    t = np.arange(POOL_TILE)[:, None]
    s = np.arange(POOL_TILE + 2 * POOL_HALO)[None, :]
    bands = [(s >= t + POOL_HALO - w // 2) & (s < t + POOL_HALO - w // 2 + w) for w in POOL_WINDOWS]
    return jnp.asarray(np.stack(bands).astype(np.float32), BF16)


def _block_diag(w_pool):
    depth, g, c, _ = w_pool.shape
    rows = []
    for i in range(g):
        blocks = [w_pool[:, i] if j == i else jnp.zeros((depth, c, c), w_pool.dtype) for j in range(g)]
        rows.append(jnp.concatenate(blocks, axis=2))
    return jnp.concatenate(rows, axis=1)


def kernel(x, c, ctx, c_ctx, w_ada, b_ada, norm_mix, w_in, q_norm, w_uq, kv_norm, w_ukv,
           ret_decay_logit, w_pool, pool_scale, w_out, norm_mlp, w_ff1, w_ff2, norm_final):
    b, n_lat, d = x.shape
    n_ctx = ctx.shape[1]
    depth = w_ada.shape[0]
    lat_tile, mlp_tile = min(LAT_TILE, n_lat), min(MLP_TILE, n_lat)
    assert n_lat % lat_tile == 0 and n_lat % mlp_tile == 0 and n_lat % min(ATTN_TILE, n_lat) == 0
    assert n_lat % GRID_W == 0 and n_lat % Q_SUB == 0
    assert n_ctx % RET_CHUNK == 0 and n_ctx % POOL_TILE == 0 and n_lat % POOL_TILE == 0
    assert d == 4 * RET_W and w_ff1.shape[2] % FF_CHUNK == 0

    ret_tabs = _ret_rope_tables(n_lat)
    mla_tabs = _mla_rope_tables(n_lat)

    rows = -(-(b + 1) // 8) * 8
    cc = jnp.zeros((rows, d), F32).at[:b].set(c).at[b].set(c_ctx)
    mod = _ada(cc, w_ada, b_ada).reshape(depth, rows, 6, d)
    lat_row, ctx_row = (lambda i: i), (lambda i: b)
    lg = jax.nn.log_sigmoid(ret_decay_logit.astype(F32))

    def rows3(a):
        return a.reshape(depth, 1, a.shape[-1])

    proj_w = (rows3(norm_mix), _prep_w_in(w_in), rows3(q_norm), _prep_w_uq(w_uq),
              rows3(kv_norm), _prep_w_ukv(w_ukv))
    pool_w = (_pool_bands(), _block_diag(w_pool).astype(BF16), rows3(pool_scale))
    mlp_w = (w_out.astype(BF16), rows3(norm_mlp), w_ff1.astype(BF16), w_ff2.astype(BF16), norm_final[None])

    h_ctx = ctx
    for l in range(depth):
        last = l == depth - 1
        ret_l, u_l, q_l, k_l, vt_l = _inproj(x, mod, lat_row, l, *proj_w, mla_tabs, lat_tile)
        if last:
            ret_c, k_c, vt_c = _inproj(h_ctx, mod, ctx_row, l, *proj_w, None, n_ctx, full=False)
            us = (u_l,)
        else:
            ret_c, u_c, q_c, k_c, vt_c = _inproj(h_ctx, mod, ctx_row, l, *proj_w, None, n_ctx)
            us = (u_l, u_c)
        ret_o, pool_o = _ret_pool(lg, l, ret_l, ret_c, ret_tabs, us, *pool_w, not last)
        mla_l = _attention(q_l, k_c, vt_c, k_l, vt_l)
        if not last:
            mla_c = _attention(q_c, k_c, vt_c)
            h_ctx = _mix_mlp(h_ctx, ret_o[1], mla_c, pool_o[1], mod, ctx_row, l, *mlp_w, n_ctx, False)
        x = _mix_mlp(x, ret_o[0], mla_l, pool_o[0], mod, lat_row, l, *mlp_w, mlp_tile, last)
    return x
```
